```python
import math
import jax, jax.numpy as jnp
from jax import lax
import numpy as np

D_MODEL = 2048
BATCH = 16
SEQ = 2048
DEPTH = 4

GRID_W = 64
CTX_LEN = 256
N_MIXERS = 4
MIXER_GLA, MIXER_GQA, MIXER_DIFF, MIXER_FOURIER = 0, 1, 2, 3
NORM_EPS = 1e-6
ROPE_THETA = 10000.0
Q_BLOCK = 128
HEAD_DIM = 128
FFN_HIDDEN = -(-8 * D_MODEL // (3 * 256)) * 256

GLA_HEADS = 4
GLA_DK = D_MODEL // 2 // GLA_HEADS
GLA_DV = D_MODEL // GLA_HEADS
GLA_QK = GLA_HEADS * GLA_DK
GLA_RANK = 16
GLA_TAU = 16.0
GLA_CHUNK = 64
GLA_SPLITS = (GLA_QK, 2 * GLA_QK, 2 * GLA_QK + D_MODEL, 2 * GLA_QK + 2 * D_MODEL,
              2 * GLA_QK + 2 * D_MODEL + GLA_RANK)
GLA_IN_DIM = 2 * GLA_QK + 2 * D_MODEL + 2 * GLA_RANK

GQA_HEADS = D_MODEL // HEAD_DIM
GQA_KV_HEADS = 4
GQA_GROUP = GQA_HEADS // GQA_KV_HEADS
GQA_IN_DIM = (GQA_HEADS + 2 * GQA_KV_HEADS) * HEAD_DIM

DIFF_HEADS = D_MODEL // (2 * HEAD_DIM)
DIFF_DH = HEAD_DIM
DIFF_IN_DIM = 3 * D_MODEL

FNET_GROUPS = 4
FNET_GROUP_DIM = D_MODEL // FNET_GROUPS

kernel_name = "hybrid_interleaved_prefix_dit_block"


def rms_norm(x, gain):
    xf = x.astype(jnp.float32)
    y = xf * lax.rsqrt(jnp.mean(xf * xf, axis=-1, keepdims=True) + NORM_EPS)
    return (y * gain.astype(jnp.float32)).astype(x.dtype)


def modulate(h, shift, scale):
    return h * (1 + scale) + shift


def swiglu(h, w_in, w_out):
    gate, up = jnp.split(h @ w_in, 2, axis=-1)
    return (jax.nn.silu(gate) * up) @ w_out


def axial_rope_tables(n_tokens, dim, dtype):
    t = jnp.arange(n_tokens)
    row = (t // GRID_W).astype(jnp.float32)
    col = (t % GRID_W).astype(jnp.float32)
    half = dim // 2
    inv_freq = ROPE_THETA ** (-jnp.arange(0, half, 2, dtype=jnp.float32) / half)
    ang_r = row[:, None] * inv_freq[None, :]
    ang_c = col[:, None] * inv_freq[None, :]
    ang = jnp.concatenate([ang_r, ang_r, ang_c, ang_c], axis=-1)
    return jnp.cos(ang).astype(dtype), jnp.sin(ang).astype(dtype)


def apply_axial_rope(x, cos, sin):
    x1, x2, x3, x4 = jnp.split(x, 4, axis=-1)
    rot = jnp.concatenate([-x2, x1, -x4, x3], axis=-1)
    return x * cos[:, None, :] + rot * sin[:, None, :]


def sweep_query_blocks(fn, *qs):
    b, s = qs[0].shape[:2]
    nb = s // Q_BLOCK
    blocks = tuple(jnp.moveaxis(q.reshape(b, nb, Q_BLOCK, *q.shape[2:]), 1, 0) for q in qs)
    out = lax.map(lambda qb: fn(*qb), blocks)
    return jnp.moveaxis(out, 0, 1).reshape(b, s, *out.shape[3:])


def gla_chunked(q, k, v, g, state0):
    b_, l_, h_, dk = q.shape
    dv = v.shape[-1]
    n = l_ // GLA_CHUNK

    def chunks(a):
        return a.reshape(b_, n, GLA_CHUNK, h_, a.shape[-1]).transpose(1, 0, 3, 2, 4)

    qc, kc, vc, gc = chunks(q), chunks(k), chunks(v), chunks(g)
    cum = jnp.cumsum(gc, axis=3)
    cum_last = cum[:, :, :, -1:, :]
    cum_mid = cum[:, :, :, GLA_CHUNK // 2 - 1:GLA_CHUNK // 2, :]
    a = jnp.einsum('nbhid,nbhjd->nbhij', qc * jnp.exp(cum - cum_mid), kc * jnp.exp(cum_mid - cum))
    mask = jnp.tril(jnp.ones((GLA_CHUNK, GLA_CHUNK), dtype=bool))
    o_intra = jnp.einsum('nbhij,nbhje->nbhie', jnp.where(mask, a, 0.0), vc)
    q_inter = qc * jnp.exp(cum)
    k_carry = kc * jnp.exp(cum_last - cum)
    decay = jnp.exp(cum_last[:, :, :, 0, :])

    def step(state, xs):
        qi, ki, vi, di = xs
        o = jnp.einsum('bhcd,bhde->bhce', qi, state)
        state = state * di[..., None] + jnp.einsum('bhcd,bhce->bhde', ki, vi)
        return state, o

    state_final, o_inter = lax.scan(step, state0, (q_inter, k_carry, vc, decay))
    o = (o_intra + o_inter).transpose(1, 0, 3, 2, 4).reshape(b_, l_, h_, dv)
    return o, state_final


def gla_mixer(h_lat, h_ctx, w_in, wg_f, bg_f, wg_b, bg_b, out_norm, w_out, ctx_out):
    def project(h):
        b_, l_, _ = h.shape
        q, k, v, r, zf, zb = jnp.split(h @ w_in, GLA_SPLITS, axis=-1)
        heads = lambda a, d: a.astype(jnp.float32).reshape(b_, l_, GLA_HEADS, d)
        gate = lambda z, w, bias: (jax.nn.log_sigmoid((z @ w + bias).astype(jnp.float32)) / GLA_TAU
                                   ).reshape(b_, l_, GLA_HEADS, GLA_DK)
        return (heads(q, GLA_DK) * GLA_DK ** -0.5, heads(k, GLA_DK), heads(v, GLA_DV), r,
                gate(zf, wg_f, bg_f), gate(zb, wg_b, bg_b))

    flip = lambda a: a[:, ::-1]

    def bidir(q, k, v, gf, gb, s_f, s_b):
        o_f, s_f = gla_chunked(q, k, v, gf, s_f)
        o_b, s_b = gla_chunked(flip(q), flip(k), flip(v), flip(gb), s_b)
        return o_f + flip(o_b), s_f, s_b

    def finish(o, r):
        b_, l_ = r.shape[:2]
        o = rms_norm(o, out_norm).reshape(b_, l_, D_MODEL).astype(r.dtype)
        return (o * jax.nn.silu(r)) @ w_out

    qc, kc, vc, rc, gcf, gcb = project(h_ctx)
    zero = jnp.zeros((h_ctx.shape[0], GLA_HEADS, GLA_DK, GLA_DV), jnp.float32)
    o_c, s_f, s_b = bidir(qc, kc, vc, gcf, gcb, zero, zero)
    ql, kl, vl, rl, glf, glb = project(h_lat)
    o_l, _, _ = bidir(ql, kl, vl, glf, glb, s_f, s_b)
    y_ctx = finish(o_c, rc) if ctx_out else None
    return finish(o_l, rl), y_ctx


def gqa_attend(q, k, v):
    s = jnp.einsum('bqhgd,bkhd->bhgqk', q, k).astype(jnp.float32) * HEAD_DIM ** -0.5
    p = jax.nn.softmax(s, axis=-1).astype(v.dtype)
    return jnp.einsum('bhgqk,bkhd->bqhgd', p, v)


def gqa_mixer(h_lat, h_ctx, w_in, q_norm, k_norm, w_out, ctx_out):
    b_, s_, _ = h_lat.shape

    def project(h):
        l_ = h.shape[1]
        q, k, v = jnp.split(h @ w_in, (GQA_HEADS * HEAD_DIM, (GQA_HEADS + GQA_KV_HEADS) * HEAD_DIM), axis=-1)
        q = rms_norm(q.reshape(b_, l_, GQA_HEADS, HEAD_DIM), q_norm)
        k = rms_norm(k.reshape(b_, l_, GQA_KV_HEADS, HEAD_DIM), k_norm)
        return q, k, v.reshape(b_, l_, GQA_KV_HEADS, HEAD_DIM)

    group = lambda q: q.reshape(b_, q.shape[1], GQA_KV_HEADS, GQA_GROUP, HEAD_DIM)
    q_c, k_c, v_c = project(h_ctx)
    q_l, k_l, v_l = project(h_lat)
    cos, sin = axial_rope_tables(s_, HEAD_DIM, q_l.dtype)
    q_l, k_l = apply_axial_rope(q_l, cos, sin), apply_axial_rope(k_l, cos, sin)
    k_all = jnp.concatenate([k_c, k_l], axis=1)
    v_all = jnp.concatenate([v_c, v_l], axis=1)
    o_l = sweep_query_blocks(lambda qb: gqa_attend(qb, k_all, v_all), group(q_l))
    y_lat = o_l.reshape(b_, s_, D_MODEL) @ w_out
    y_ctx = None
    if ctx_out:
        y_ctx = gqa_attend(group(q_c), k_c, v_c).reshape(b_, q_c.shape[1], D_MODEL) @ w_out
    return y_lat, y_ctx


def diff_attend(q, k, v, lam):
    s = jnp.einsum('bqhmd,bkhmd->bhmqk', q, k).astype(jnp.float32) * DIFF_DH ** -0.5
    p = jax.nn.softmax(s, axis=-1)
    w = (p[:, :, 0] - lam * p[:, :, 1]).astype(v.dtype)
    return jnp.einsum('bhqk,bkhe->bqhe', w, v)


def diff_mixer(h_lat, h_ctx, w_in, q_norm, k_norm, lq1, lk1, lq2, lk2, out_norm, w_out, layer_idx, ctx_out):
    b_, s_, _ = h_lat.shape
    lam_init = 0.8 - 0.6 * math.exp(-0.3 * layer_idx)
    f32 = jnp.float32
    lam = (jnp.exp(jnp.sum(lq1.astype(f32) * lk1.astype(f32)))
           - jnp.exp(jnp.sum(lq2.astype(f32) * lk2.astype(f32))) + lam_init)

    def project(h):
        l_ = h.shape[1]
        q, k, v = jnp.split(h @ w_in, (D_MODEL, 2 * D_MODEL), axis=-1)
        q = rms_norm(q.reshape(b_, l_, 2 * DIFF_HEADS, DIFF_DH), q_norm)
        k = rms_norm(k.reshape(b_, l_, 2 * DIFF_HEADS, DIFF_DH), k_norm)
        return q, k, v.reshape(b_, l_, DIFF_HEADS, 2 * DIFF_DH)

    pair = lambda a: a.reshape(b_, a.shape[1], DIFF_HEADS, 2, DIFF_DH)

    def finish(o):
        o = rms_norm(o, out_norm) * (1.0 - lam_init)
        return o.reshape(b_, o.shape[1], D_MODEL) @ w_out

    q_c, k_c, v_c = project(h_ctx)
    q_l, k_l, v_l = project(h_lat)
    cos, sin = axial_rope_tables(s_, DIFF_DH, q_l.dtype)
    q_l, k_l = apply_axial_rope(q_l, cos, sin), apply_axial_rope(k_l, cos, sin)
    k_all = pair(jnp.concatenate([k_c, k_l], axis=1))
    v_all = jnp.concatenate([v_c, v_l], axis=1)
    o_l = sweep_query_blocks(lambda qb: diff_attend(qb, k_all, v_all, lam), pair(q_l))
    y_ctx = finish(diff_attend(pair(q_c), pair(k_c), v_c, lam)) if ctx_out else None
    return finish(o_l), y_ctx


def fnet_mixer(h, w_out):
    b_, l_, _ = h.shape
    hg = h.astype(jnp.float32).reshape(b_, l_, FNET_GROUPS, FNET_GROUP_DIM)
    y = jnp.real(jnp.fft.fftn(hg, axes=(1, 3), norm="ortho"))
    return y.reshape(b_, l_, D_MODEL).astype(h.dtype) @ w_out


def setup_inputs(seed: int = 0) -> dict:
    key = jax.random.key(seed)
    counter = [0]

    def nxt():
        counter[0] += 1
        return jax.random.fold_in(key, counter[0])

    normal = lambda shape: jax.random.normal(nxt(), shape, jnp.float32)
    dense = lambda shape, s=1.0: normal(shape) * (s * shape[0] ** -0.5)
    gain = lambda n: 1.0 + 0.02 * normal((n,))
    small = lambda n, s=0.02: s * normal((n,))

    p = {
        "x": normal((BATCH, SEQ, D_MODEL)),
        "c": normal((BATCH, D_MODEL)),
        "ctx": normal((BATCH, CTX_LEN, D_MODEL)),
        "c_ctx": normal((D_MODEL,)),
    }
    for i in range(DEPTH):
        kind = i % N_MIXERS
        pre = f"l{i}_"
        p[pre + "mod_w"] = dense((D_MODEL, 6 * D_MODEL), 0.5)
        p[pre + "mod_b"] = small(6 * D_MODEL)
        p[pre + "norm1"] = gain(D_MODEL)
        if kind == MIXER_GLA:
            p[pre + "gla_w_in"] = dense((D_MODEL, GLA_IN_DIM))
            p[pre + "gla_wg_f"] = dense((GLA_RANK, GLA_QK))
            p[pre + "gla_bg_f"] = small(GLA_QK, 0.1)
            p[pre + "gla_wg_b"] = dense((GLA_RANK, GLA_QK))
            p[pre + "gla_bg_b"] = small(GLA_QK, 0.1)
            p[pre + "gla_out_norm"] = gain(GLA_DV)
            p[pre + "gla_w_out"] = dense((D_MODEL, D_MODEL))
        elif kind == MIXER_GQA:
            p[pre + "gqa_w_in"] = dense((D_MODEL, GQA_IN_DIM))
            p[pre + "gqa_q_norm"] = gain(HEAD_DIM)
            p[pre + "gqa_k_norm"] = gain(HEAD_DIM)
            p[pre + "gqa_w_out"] = dense((D_MODEL, D_MODEL))
        elif kind == MIXER_DIFF:
            p[pre + "diff_w_in"] = dense((D_MODEL, DIFF_IN_DIM))
            p[pre + "diff_q_norm"] = gain(DIFF_DH)
            p[pre + "diff_k_norm"] = gain(DIFF_DH)
            p[pre + "diff_lq1"] = small(DIFF_DH, 0.1)
            p[pre + "diff_lk1"] = small(DIFF_DH, 0.1)
            p[pre + "diff_lq2"] = small(DIFF_DH, 0.1)
            p[pre + "diff_lk2"] = small(DIFF_DH, 0.1)
            p[pre + "diff_out_norm"] = gain(2 * DIFF_DH)
            p[pre + "diff_w_out"] = dense((D_MODEL, D_MODEL))
        else:
            p[pre + "fnet_w_out"] = dense((D_MODEL, D_MODEL))
        p[pre + "norm2"] = gain(D_MODEL)
        p[pre + "ffn_w_in"] = dense((D_MODEL, 2 * FFN_HIDDEN))
        p[pre + "ffn_w_out"] = dense((FFN_HIDDEN, D_MODEL))
    return p


def reference(x, c, ctx, c_ctx,
              l0_mod_w, l0_mod_b, l0_norm1, l0_gla_w_in, l0_gla_wg_f, l0_gla_bg_f, l0_gla_wg_b, l0_gla_bg_b,
              l0_gla_out_norm, l0_gla_w_out, l0_norm2, l0_ffn_w_in, l0_ffn_w_out,
              l1_mod_w, l1_mod_b, l1_norm1, l1_gqa_w_in, l1_gqa_q_norm, l1_gqa_k_norm, l1_gqa_w_out,
              l1_norm2, l1_ffn_w_in, l1_ffn_w_out,
              l2_mod_w, l2_mod_b, l2_norm1, l2_diff_w_in, l2_diff_q_norm, l2_diff_k_norm, l2_diff_lq1, l2_diff_lk1,
              l2_diff_lq2, l2_diff_lk2, l2_diff_out_norm, l2_diff_w_out, l2_norm2, l2_ffn_w_in, l2_ffn_w_out,
              l3_mod_w, l3_mod_b, l3_norm1, l3_fnet_w_out, l3_norm2, l3_ffn_w_in, l3_ffn_w_out):
    common = [
        (l0_mod_w, l0_mod_b, l0_norm1, l0_norm2, l0_ffn_w_in, l0_ffn_w_out),
        (l1_mod_w, l1_mod_b, l1_norm1, l1_norm2, l1_ffn_w_in, l1_ffn_w_out),
        (l2_mod_w, l2_mod_b, l2_norm1, l2_norm2, l2_ffn_w_in, l2_ffn_w_out),
        (l3_mod_w, l3_mod_b, l3_norm1, l3_norm2, l3_ffn_w_in, l3_ffn_w_out),
    ]
    mixer_params = [
        (l0_gla_w_in, l0_gla_wg_f, l0_gla_bg_f, l0_gla_wg_b, l0_gla_bg_b, l0_gla_out_norm, l0_gla_w_out),
        (l1_gqa_w_in, l1_gqa_q_norm, l1_gqa_k_norm, l1_gqa_w_out),
        (l2_diff_w_in, l2_diff_q_norm, l2_diff_k_norm, l2_diff_lq1, l2_diff_lk1, l2_diff_lq2, l2_diff_lk2,
         l2_diff_out_norm, l2_diff_w_out),
        (l3_fnet_w_out,),
    ]
    x_lat, x_ctx = x, ctx
    s_lat = jax.nn.silu(c)
    s_ctx = jax.nn.silu(c_ctx)
    for i in range(DEPTH):
        kind = i % N_MIXERS
        mod_w, mod_b, n1, n2, f_in, f_out = common[i]
        mp = mixer_params[i]
        ctx_out = any((j % N_MIXERS) != MIXER_FOURIER for j in range(i + 1, DEPTH))
        ctx_in = ctx_out or kind != MIXER_FOURIER
        sh1, sc1, g1, sh2, sc2, g2 = jnp.split((s_lat @ mod_w + mod_b)[:, None, :], 6, axis=-1)
        h_lat = modulate(rms_norm(x_lat, n1), sh1, sc1)
        h_ctx = None
        if ctx_in:
            csh1, csc1, cg1, csh2, csc2, cg2 = jnp.split(s_ctx @ mod_w + mod_b, 6, axis=-1)
            h_ctx = modulate(rms_norm(x_ctx, n1), csh1, csc1)
        if kind == MIXER_GLA:
            y_lat, y_ctx = gla_mixer(h_lat, h_ctx, *mp, ctx_out=ctx_out)
        elif kind == MIXER_GQA:
            y_lat, y_ctx = gqa_mixer(h_lat, h_ctx, *mp, ctx_out=ctx_out)
        elif kind == MIXER_DIFF:
            y_lat, y_ctx = diff_mixer(h_lat, h_ctx, *mp, layer_idx=i, ctx_out=ctx_out)
        else:
            y_lat = fnet_mixer(h_lat, mp[0])
            y_ctx = fnet_mixer(h_ctx, mp[0]) if ctx_out else None
        x_lat = x_lat + g1 * y_lat
        x_lat = x_lat + g2 * swiglu(modulate(rms_norm(x_lat, n2), sh2, sc2), f_in, f_out)
        if ctx_out:
            x_ctx = x_ctx + cg1 * y_ctx
            x_ctx = x_ctx + cg2 * swiglu(modulate(rms_norm(x_ctx, n2), csh2, csc2), f_in, f_out)
    return x_lat
```

```python
import functools
import math

import jax
import jax.numpy as jnp
from jax import lax
from jax.experimental import pallas as pl
from jax.experimental.pallas import tpu as pltpu

F32 = jnp.float32
BF16 = jnp.bfloat16

NORM_EPS = 1e-6
ROPE_THETA = 10000.0
GRID_W = 64
HEAD_DIM = 128
GQA_KV_HEADS = 4
GLA_HEADS = 4
GLA_RANK = 16
GLA_TAU = 16.0
GLA_CHUNK = 64
FNET_GROUPS = 4

LANES = 128
VMEM_LIMIT = 56 * 1024 * 1024

NT_DIMS = (((1,), (1,)), ((), ()))
TN_DIMS = (((0,), (0,)), ((), ()))


def _params(*sem):
    return pltpu.CompilerParams(dimension_semantics=sem, vmem_limit_bytes=VMEM_LIMIT)


def _rms(x, eps=NORM_EPS):
    return x * lax.rsqrt(jnp.mean(x * x, axis=-1, keepdims=True) + eps)


def _norm_mod(x, gain, shift, scale):
    return (_rms(x) * gain) * (1.0 + scale) + shift


def _mm_kernel(*refs, prologue, epilogue, n_rope_tiles, rope):
    it = iter(refs)
    x_ref = next(it)
    if prologue == "norm_mod":
        gain_ref, shift_ref, scale_ref = next(it), next(it), next(it)
    w_ref = next(it)
    if epilogue == "bias":
        b_ref = next(it)
    elif epilogue == "resid":
        res_ref, gate_ref = next(it), next(it)
    elif epilogue == "normrope":
        hg_ref = next(it)
        if rope:
            cos_ref, sin_ref = next(it), next(it)
    o_ref = next(it)
    xn_ref = next(it)
    j = pl.program_id(1)

    if prologue == "cast":
        a = x_ref[...].astype(BF16)
    else:
        @pl.when(j == 0)
        def _():
            xv = x_ref[...]
            if prologue == "norm_mod":
                xv = _norm_mod(xv, gain_ref[...], shift_ref[0], scale_ref[0])
            else:
                xv = jax.nn.silu(xv)
            xn_ref[...] = xv.astype(BF16)

        a = xn_ref[...]
    acc = jnp.dot(a, w_ref[...].astype(BF16), preferred_element_type=F32)
    if epilogue == "store":
        o_ref[...] = acc.astype(o_ref.dtype)
    elif epilogue == "bias":
        o_ref[...] = (acc + b_ref[...]).astype(o_ref.dtype)
    elif epilogue == "resid":
        o_ref[...] = (res_ref[...] + gate_ref[0] * acc).astype(o_ref.dtype)
    elif epilogue == "normrope":
        tn = acc.shape[1]

        @pl.when(j < n_rope_tiles)
        def _():
            hg = hg_ref[0]
            heads = []
            for h in range(tn // HEAD_DIM):
                sl = slice(h * HEAD_DIM, (h + 1) * HEAD_DIM)
                heads.append(_rms(acc[:, sl]) * hg[:, sl])
            y = jnp.concatenate(heads, axis=-1)
            if rope:
                quarter = HEAD_DIM // 4
                lane = lax.broadcasted_iota(jnp.int32, y.shape, 1)
                even = (lane // quarter) % 2 == 0
                partner = jnp.where(even, pltpu.roll(y, tn - quarter, 1), pltpu.roll(y, quarter, 1))
                y = y * cos_ref[...] + partner * sin_ref[...]
            o_ref[...] = y.astype(o_ref.dtype)

        @pl.when(j >= n_rope_tiles)
        def _():
            o_ref[...] = acc.astype(o_ref.dtype)


def _mm(x, w, *, tm, tn, out_dtype, prologue="cast", epilogue="store", rows_per_batch=None,
        gain=None, shift=None, scale=None, bias=None, res=None, gate=None,
        hgain=None, cos=None, sin=None, n_rope_tiles=0):
    t, k = x.shape
    n = w.shape[1]
    assert t % tm == 0 and n % tn == 0, (t, tm, n, tn)
    rows_per_batch = rows_per_batch or t
    assert rows_per_batch % tm == 0
    tiles_per_batch = rows_per_batch // tm
    rope = cos is not None

    def batch_of(i):
        return i // tiles_per_batch

    args = [x]
    specs = [pl.BlockSpec((tm, k), lambda i, j: (i, 0))]
    if prologue == "norm_mod":
        nb = shift.shape[0]
        bsel = (lambda i: batch_of(i)) if nb > 1 else (lambda i: 0)
        args += [gain.reshape(1, k), shift, scale]
        specs += [pl.BlockSpec((1, k), lambda i, j: (0, 0)),
                  pl.BlockSpec((1, 1, k), lambda i, j: (bsel(i), 0, 0)),
                  pl.BlockSpec((1, 1, k), lambda i, j: (bsel(i), 0, 0))]
    args.append(w)
    specs.append(pl.BlockSpec((k, tn), lambda i, j: (0, j)))
    if epilogue == "bias":
        args.append(bias.reshape(1, n))
        specs.append(pl.BlockSpec((1, tn), lambda i, j: (0, j)))
    elif epilogue == "resid":
        nb = gate.shape[0]
        gsel = (lambda i: batch_of(i)) if nb > 1 else (lambda i: 0)
        args += [res, gate]
        specs += [pl.BlockSpec((tm, tn), lambda i, j: (i, j)),
                  pl.BlockSpec((1, 1, tn), lambda i, j: (gsel(i), 0, j))]
    elif epilogue == "normrope":
        args.append(hgain)
        specs.append(pl.BlockSpec((1, 1, tn), lambda i, j: (j, 0, 0)))
        if rope:
            args += [cos, sin]
            specs += [pl.BlockSpec((tm, tn), lambda i, j: (i % tiles_per_batch, 0))] * 2
    kern = functools.partial(_mm_kernel, prologue=prologue, epilogue=epilogue,
                             n_rope_tiles=n_rope_tiles, rope=rope)
    return pl.pallas_call(
        kern,
        grid=(t // tm, n // tn),
        in_specs=specs,
        out_specs=pl.BlockSpec((tm, tn), lambda i, j: (i, j)),
        out_shape=jax.ShapeDtypeStruct((t, n), out_dtype),
        scratch_shapes=[pltpu.VMEM((tm, k), BF16)],
        compiler_params=_params("parallel", "arbitrary"),
    )(*args)


def _ffn_kernel(x_ref, gain_ref, shift_ref, scale_ref, gate_ref, wg_ref, wu_ref, wo_ref, o_ref, xn_ref, acc_ref):
    j = pl.program_id(1)

    @pl.when(j == 0)
    def _():
        xn_ref[...] = _norm_mod(x_ref[...], gain_ref[...], shift_ref[0], scale_ref[0]).astype(BF16)
        acc_ref[...] = jnp.zeros_like(acc_ref)

    xn = xn_ref[...]
    g = jnp.dot(xn, wg_ref[...], preferred_element_type=F32)
    u = jnp.dot(xn, wu_ref[...], preferred_element_type=F32)
    a = (jax.nn.silu(g) * u).astype(BF16)
    acc_ref[...] += jnp.dot(a, wo_ref[...], preferred_element_type=F32)

    @pl.when(j == pl.num_programs(1) - 1)
    def _():
        o_ref[...] = x_ref[...] + gate_ref[0] * acc_ref[...]


def _ffn(x, gain, shift, scale, gate, w_in, w_out, *, tm, th, rows_per_batch=None):
    t, d = x.shape
    hidden = w_out.shape[0]
    assert t % tm == 0 and hidden % th == 0
    nh = hidden // th
    rows_per_batch = rows_per_batch or t
    tiles_per_batch = rows_per_batch // tm
    nb = shift.shape[0]
    bsel = (lambda i: i // tiles_per_batch) if nb > 1 else (lambda i: 0)
    vec = pl.BlockSpec((1, 1, d), lambda i, j: (bsel(i), 0, 0))
    return pl.pallas_call(
        _ffn_kernel,
        grid=(t // tm, nh),
        in_specs=[pl.BlockSpec((tm, d), lambda i, j: (i, 0)),
                  pl.BlockSpec((1, d), lambda i, j: (0, 0)),
                  vec, vec, vec,
                  pl.BlockSpec((d, th), lambda i, j: (0, j)),
                  pl.BlockSpec((d, th), lambda i, j: (0, nh + j)),
                  pl.BlockSpec((th, d), lambda i, j: (j, 0))],
        out_specs=pl.BlockSpec((tm, d), lambda i, j: (i, 0)),
        out_shape=jax.ShapeDtypeStruct((t, d), F32),
        scratch_shapes=[pltpu.VMEM((tm, d), BF16), pltpu.VMEM((tm, d), F32)],
        compiler_params=_params("parallel", "arbitrary"),
    )(x, gain.reshape(1, d), shift, scale, gate, w_in, w_in, w_out)


def _fill_kv(k_s, v_s, kv_refs):
    off = 0
    for k_ref, v_ref in kv_refs:
        n = k_ref.shape[0]
        k_s[off:off + n, :] = k_ref[...]
        v_s[off:off + n, :] = v_ref[...]
        off += n


def _softmax_parts(q, k):
    s = lax.dot_general(q, k, NT_DIMS, preferred_element_type=F32)
    p = jnp.exp(s - jnp.max(s, axis=-1, keepdims=True))
    return p, jnp.sum(p, axis=-1, keepdims=True)


def _gqa_kernel(*refs, n_kv_src, n_group):
    q_ref = refs[0]
    kv_refs = [(refs[1 + 2 * s], refs[2 + 2 * s]) for s in range(n_kv_src)]
    o_ref, k_s, v_s = refs[1 + 2 * n_kv_src:]

    @pl.when(pl.program_id(2) == 0)
    def _():
        _fill_kv(k_s, v_s, kv_refs)

    k = k_s[...]
    v = v_s[...]
    for g in range(n_group):
        sl = slice(g * HEAD_DIM, (g + 1) * HEAD_DIM)
        p, l = _softmax_parts(q_ref[:, sl], k)
        o = jnp.dot(p.astype(BF16), v, preferred_element_type=F32)
        o_ref[:, sl] = (o / l).astype(o_ref.dtype)


def _gqa_attention(q_src, kv_srcs, *, batch, n_q_heads, n_kv_heads, tq, q_rows):
    group = n_q_heads // n_kv_heads
    gw = group * HEAD_DIM
    nq = q_rows // tq
    k_blk0 = n_q_heads
    v_blk0 = n_q_heads + n_kv_heads
    specs = [pl.BlockSpec((tq, gw), lambda b, h, i: (b * nq + i, h))]
    args = [q_src]
    total = 0
    for src in kv_srcs:
        rows = src.shape[0] // batch
        total += rows
        specs += [pl.BlockSpec((rows, HEAD_DIM), lambda b, h, i: (b, k_blk0 + h)),
                  pl.BlockSpec((rows, HEAD_DIM), lambda b, h, i: (b, v_blk0 + h))]
        args += [src, src]
    kern = functools.partial(_gqa_kernel, n_kv_src=len(kv_srcs), n_group=group)
    return pl.pallas_call(
        kern,
        grid=(batch, n_kv_heads, nq),
        in_specs=specs,
        out_specs=pl.BlockSpec((tq, gw), lambda b, h, i: (b * nq + i, h)),
        out_shape=jax.ShapeDtypeStruct((batch * q_rows, n_q_heads * HEAD_DIM), BF16),
        scratch_shapes=[pltpu.VMEM((total, HEAD_DIM), BF16), pltpu.VMEM((total, HEAD_DIM), BF16)],
        compiler_params=_params("parallel", "parallel", "arbitrary"),
    )(*args)


def _diff_kernel(*refs, n_kv_src, lam_init):
    q_ref = refs[0]
    kv_refs = [(refs[1 + 2 * s], refs[2 + 2 * s]) for s in range(n_kv_src)]
    lq1, lk1, lq2, lk2, gain_ref, o_ref, k_s, v_s = refs[1 + 2 * n_kv_src:]

    @pl.when(pl.program_id(2) == 0)
    def _():
        _fill_kv(k_s, v_s, kv_refs)

    lam = (jnp.exp(jnp.sum(lq1[...] * lk1[...], axis=-1, keepdims=True))
           - jnp.exp(jnp.sum(lq2[...] * lk2[...], axis=-1, keepdims=True)) + lam_init)
    dh = HEAD_DIM
    p0, l0 = _softmax_parts(q_ref[:, 0:dh], k_s[:, 0:dh])
    p1, l1 = _softmax_parts(q_ref[:, dh:2 * dh], k_s[:, dh:2 * dh])
    w = p0 * (1.0 / l0) - p1 * (lam / l1)
    o = jnp.dot(w.astype(BF16), v_s[...], preferred_element_type=F32)
    o_ref[...] = ((_rms(o) * gain_ref[...]) * (1.0 - lam_init)).astype(o_ref.dtype)


def _diff_attention(q_src, kv_srcs, lams, out_gain, *, batch, n_heads, tq, q_rows, lam_init):
    hw = 2 * HEAD_DIM
    nq = q_rows // tq
    specs = [pl.BlockSpec((tq, hw), lambda b, h, i: (b * nq + i, h))]
    args = [q_src]
    total = 0
    for src in kv_srcs:
        rows = src.shape[0] // batch
        total += rows
        specs += [pl.BlockSpec((rows, hw), lambda b, h, i: (b, n_heads + h)),
                  pl.BlockSpec((rows, hw), lambda b, h, i: (b, 2 * n_heads + h))]
        args += [src, src]
    small = pl.BlockSpec((1, HEAD_DIM), lambda b, h, i: (0, 0))
    specs += [small] * 4 + [pl.BlockSpec((1, hw), lambda b, h, i: (0, 0))]
    args += [v.reshape(1, HEAD_DIM) for v in lams] + [out_gain.reshape(1, hw)]
    kern = functools.partial(_diff_kernel, n_kv_src=len(kv_srcs), lam_init=lam_init)
    return pl.pallas_call(
        kern,
        grid=(batch, n_heads, nq),
        in_specs=specs,
        out_specs=pl.BlockSpec((tq, hw), lambda b, h, i: (b * nq + i, h)),
        out_shape=jax.ShapeDtypeStruct((batch * q_rows, n_heads * hw), BF16),
        scratch_shapes=[pltpu.VMEM((total, hw), BF16), pltpu.VMEM((total, hw), BF16)],
        compiler_params=_params("parallel", "parallel", "arbitrary"),
    )(*args)


def _gla_chunk(q, k, v, z, wg, bg, state_ref, *, reverse):
    c = q.shape[0]
    glog = jnp.dot(z, wg, preferred_element_type=F32) + bg
    g = jax.nn.log_sigmoid(glog) / GLA_TAU
    row = lax.broadcasted_iota(jnp.int32, (c, c), 0)
    col = lax.broadcasted_iota(jnp.int32, (c, c), 1)
    seen = (col >= row) if reverse else (col <= row)
    tri = jnp.where(seen, 1.0, 0.0).astype(BF16)
    g_hi = g.astype(BF16)
    g_lo = (g - g_hi.astype(F32)).astype(BF16)
    cum = (jnp.dot(tri, g_hi, preferred_element_type=F32)
           + jnp.dot(tri, g_lo, preferred_element_type=F32))
    mid = c // 2 if reverse else c // 2 - 1
    last = 0 if reverse else c - 1
    cum_mid = cum[mid:mid + 1, :]
    cum_last = cum[last:last + 1, :]
    qs = (q * jnp.exp(cum - cum_mid)).astype(BF16)
    ks = (k * jnp.exp(cum_mid - cum)).astype(BF16)
    a = lax.dot_general(qs, ks, NT_DIMS, preferred_element_type=F32)
    a = jnp.where(seen, a, 0.0).astype(BF16)
    state = state_ref[...]
    o = (jnp.dot(a, v, preferred_element_type=F32)
         + jnp.dot((q * jnp.exp(cum)).astype(BF16), state.astype(BF16), preferred_element_type=F32))
    k_carry = (k * jnp.exp(cum_last - cum)).astype(BF16)
    ones = jnp.ones((c, LANES), BF16)
    dlog = (lax.dot_general(g_hi, ones, TN_DIMS, preferred_element_type=F32)
            + lax.dot_general(g_lo, ones, TN_DIMS, preferred_element_type=F32))
    decay = jnp.exp(dlog)
    decay = jnp.concatenate([decay] * (state.shape[1] // LANES), axis=-1)
    state_ref[...] = state * decay + lax.dot_general(k_carry, v, TN_DIMS, preferred_element_type=F32)
    return o


def _gla_kernel(qc_ref, kc_ref, vc_ref, rc_ref, zc_ref, ql_ref, kl_ref, vl_ref, rl_ref, zl_ref,
                wgf_ref, bgf_ref, wgb_ref, bgb_ref, gain_ref, oc_ref, ol_ref, state_ref, of_ref):
    c = GLA_CHUNK
    n_ctx = qc_ref.shape[0] // c
    n_lat = ql_ref.shape[0] // c
    q_scale = qc_ref.shape[1] ** -0.5

    def load(refs, ci):
        q_ref, k_ref, v_ref, r_ref, z_ref = refs
        rows = pl.ds(pl.multiple_of(ci * c, c), c)
        return (q_ref[rows, :].astype(F32) * q_scale, k_ref[rows, :].astype(F32), v_ref[rows, :],
                r_ref[rows, :].astype(F32), z_ref[rows, :])

    ctx_refs = (qc_ref, kc_ref, vc_ref, rc_ref, zc_ref)
    lat_refs = (ql_ref, kl_ref, vl_ref, rl_ref, zl_ref)

    def fwd(refs, base):
        def body(ci, carry):
            q, k, v, _, z = load(refs, ci)
            o = _gla_chunk(q, k, v, z, wgf_ref[0], bgf_ref[0], state_ref, reverse=False)
            of_ref[pl.ds(pl.multiple_of((base + ci) * c, c), c), :] = o
            return carry
        return body

    def bwd(refs, base, out_ref, n):
        def body(t, carry):
            ci = n - 1 - t
            q, k, v, r, z = load(refs, ci)
            o = _gla_chunk(q, k, v, z, wgb_ref[0], bgb_ref[0], state_ref, reverse=True)
            o = o + of_ref[pl.ds(pl.multiple_of((base + ci) * c, c), c), :]
            y = (_rms(o) * gain_ref[...]) * jax.nn.silu(r)
            out_ref[pl.ds(pl.multiple_of(ci * c, c), c), :] = y.astype(out_ref.dtype)
            return carry
        return body

    state_ref[...] = jnp.zeros_like(state_ref)
    lax.fori_loop(0, n_ctx, fwd(ctx_refs, 0), 0)
    lax.fori_loop(0, n_lat, fwd(lat_refs, n_ctx), 0)
    state_ref[...] = jnp.zeros_like(state_ref)
    lax.fori_loop(0, n_ctx, bwd(ctx_refs, 0, oc_ref, n_ctx), 0)
    lax.fori_loop(0, n_lat, bwd(lat_refs, n_ctx, ol_ref, n_lat), 0)


def _gla(p_ctx, p_lat, wgf, bgf, wgb, bgb, out_gain, *, batch, dk, dv):
    h = GLA_HEADS
    rows_c = p_ctx.shape[0] // batch
    rows_l = p_lat.shape[0] // batch
    zblk = (2 * h * dk + 2 * h * dv) // LANES
    k0 = h
    v0 = (2 * h * dk) // dv
    r0 = v0 + h

    def stream(rows):
        return [pl.BlockSpec((rows, dk), lambda b, hh: (b, hh)),
                pl.BlockSpec((rows, dk), lambda b, hh: (b, k0 + hh)),
                pl.BlockSpec((rows, dv), lambda b, hh: (b, v0 + hh)),
                pl.BlockSpec((rows, dv), lambda b, hh: (b, r0 + hh)),
                pl.BlockSpec((rows, LANES), lambda b, hh: (b, zblk))]

    wspec = pl.BlockSpec((1, LANES, dk), lambda b, hh: (hh, 0, 0))
    bspec = pl.BlockSpec((1, 1, dk), lambda b, hh: (hh, 0, 0))
    return pl.pallas_call(
        _gla_kernel,
        grid=(batch, h),
        in_specs=stream(rows_c) + stream(rows_l) + [wspec, bspec, wspec, bspec,
                                                    pl.BlockSpec((1, dv), lambda b, hh: (0, 0))],
        out_specs=[pl.BlockSpec((rows_c, dv), lambda b, hh: (b, hh)),
                   pl.BlockSpec((rows_l, dv), lambda b, hh: (b, hh))],
        out_shape=[jax.ShapeDtypeStruct((batch * rows_c, h * dv), BF16),
                   jax.ShapeDtypeStruct((batch * rows_l, h * dv), BF16)],
        scratch_shapes=[pltpu.VMEM((dk, dv), F32), pltpu.VMEM((rows_c + rows_l, dv), F32)],
        compiler_params=_params("parallel", "parallel"),
    )(*([p_ctx] * 5 + [p_lat] * 5 + [wgf, bgf, wgb, bgb, out_gain.reshape(1, dv)]))


def _dft_tables(n):
    idx = jnp.arange(n, dtype=jnp.int32)
    ang = ((idx[:, None] * idx[None, :]) % n).astype(F32) * (2.0 * math.pi / n)
    return jnp.cos(ang).astype(BF16), jnp.sin(ang).astype(BF16)


def _fnet_chan_kernel(x_ref, gain_ref, shift_ref, scale_ref, cc_ref, sc_ref, p_ref, q_ref, xn_ref):
    j = pl.program_id(1)
    gd = cc_ref.shape[0]

    @pl.when(j == 0)
    def _():
        h = _norm_mod(x_ref[...], gain_ref[...], shift_ref[0], scale_ref[0]).astype(BF16)
        for g in range(xn_ref.shape[0]):
            xn_ref[g] = h[:, g * gd:(g + 1) * gd]

    hg = xn_ref[j]
    p_ref[...] = jnp.dot(hg, cc_ref[...], preferred_element_type=F32).astype(p_ref.dtype)
    q_ref[...] = jnp.dot(hg, sc_ref[...], preferred_element_type=F32).astype(q_ref.dtype)


def _fnet_seq_kernel(cs_ref, ss_ref, p_ref, q_ref, o_ref, *, inv_norm):
    acc = (jnp.dot(cs_ref[...], p_ref[...], preferred_element_type=F32)
           - jnp.dot(ss_ref[...], q_ref[...], preferred_element_type=F32))
    o_ref[...] = (acc * inv_norm).astype(o_ref.dtype)


def _fnet(x, gain, shift, scale, *, batch, tm):
    t, d = x.shape
    s = t // batch
    gd = d // FNET_GROUPS
    tiles_per_batch = s // tm
    cc, sc = _dft_tables(gd)
    cs, ss = _dft_tables(s)
    vec = pl.BlockSpec((1, 1, d), lambda i, j: (i // tiles_per_batch, 0, 0))
    tab = pl.BlockSpec((gd, gd), lambda i, j: (0, 0))
    blk = pl.BlockSpec((tm, gd), lambda i, j: (i, j))
    p, q = pl.pallas_call(
        _fnet_chan_kernel,
        grid=(t // tm, FNET_GROUPS),
        in_specs=[pl.BlockSpec((tm, d), lambda i, j: (i, 0)),
                  pl.BlockSpec((1, d), lambda i, j: (0, 0)), vec, vec, tab, tab],
        out_specs=[blk, blk],
        out_shape=[jax.ShapeDtypeStruct((t, d), BF16)] * 2,
        scratch_shapes=[pltpu.VMEM((FNET_GROUPS, tm, gd), BF16)],
        compiler_params=_params("parallel", "arbitrary"),
    )(x, gain.reshape(1, d), shift, scale, cc, sc)
    rows = pl.BlockSpec((tm, s), lambda b, j, i: (i, 0))
    cols = pl.BlockSpec((s, gd), lambda b, j, i: (b, j))
    return pl.pallas_call(
        functools.partial(_fnet_seq_kernel, inv_norm=float((s * gd) ** -0.5)),
        grid=(batch, d // gd, tiles_per_batch),
        in_specs=[rows, rows, cols, cols],
        out_specs=pl.BlockSpec((tm, gd), lambda b, j, i: (b * tiles_per_batch + i, j)),
        out_shape=jax.ShapeDtypeStruct((t, d), BF16),
        compiler_params=_params("parallel", "parallel", "arbitrary"),
    )(cs, ss, p, q)


def _rope_tables(n_tokens, width):
    t = jnp.arange(n_tokens)
    row = (t // GRID_W).astype(F32)
    col = (t % GRID_W).astype(F32)
    half = HEAD_DIM // 2
    inv_freq = ROPE_THETA ** (-jnp.arange(0, half, 2, dtype=F32) / half)
    ang_r = row[:, None] * inv_freq[None, :]
    ang_c = col[:, None] * inv_freq[None, :]
    ang = jnp.concatenate([ang_r, ang_r, ang_c, ang_c], axis=-1)
    sign = jnp.concatenate([-jnp.ones((half // 2,), F32), jnp.ones((half // 2,), F32)] * 2)
    reps = width // HEAD_DIM
    return jnp.tile(jnp.cos(ang), (1, reps)), jnp.tile(jnp.sin(ang) * sign, (1, reps))


def _head_gains(q_gain, k_gain, n_q_tiles, n_k_tiles, n_v_tiles, tn):
    reps = tn // HEAD_DIM
    qg = jnp.tile(q_gain.astype(F32) * HEAD_DIM ** -0.5, reps)
    kg = jnp.tile(k_gain.astype(F32), reps)
    rows = [qg] * n_q_tiles + [kg] * n_k_tiles + [jnp.ones((tn,), F32)] * n_v_tiles
    return jnp.stack(rows)[:, None, :]


def _gate_weights(wg, bg, lane0, dk):
    r = wg.shape[0]
    w = wg.reshape(r, GLA_HEADS, dk).transpose(1, 0, 2)
    w = jnp.pad(w, ((0, 0), (lane0, LANES - lane0 - r), (0, 0))).astype(BF16)
    return w, bg.reshape(GLA_HEADS, 1, dk).astype(F32)


def kernel(x, c, ctx, c_ctx, l0_mod_w, l0_mod_b, l0_norm1, l0_gla_w_in, l0_gla_wg_f, l0_gla_bg_f, l0_gla_wg_b, l0_gla_bg_b, l0_gla_out_norm, l0_gla_w_out, l0_norm2, l0_ffn_w_in, l0_ffn_w_out, l1_mod_w, l1_mod_b, l1_norm1, l1_gqa_w_in, l1_gqa_q_norm, l1_gqa_k_norm, l1_gqa_w_out, l1_norm2, l1_ffn_w_in, l1_ffn_w_out, l2_mod_w, l2_mod_b, l2_norm1, l2_diff_w_in, l2_diff_q_norm, l2_diff_k_norm, l2_diff_lq1, l2_diff_lk1, l2_diff_lq2, l2_diff_lk2, l2_diff_out_norm, l2_diff_w_out, l2_norm2, l2_ffn_w_in, l2_ffn_w_out, l3_mod_w, l3_mod_b, l3_norm1, l3_fnet_w_out, l3_norm2, l3_ffn_w_in, l3_ffn_w_out):
    b, s, d = x.shape
    n_ctx = ctx.shape[1]
    xl = x.reshape(b * s, d)
    xc = ctx.reshape(b * n_ctx, d)
    tm = math.gcd(s, 512)
    tm_c = math.gcd(b * n_ctx, 512)
    th = math.gcd(l0_ffn_w_out.shape[0], 512)
    tq = math.gcd(s, 256)

    n_cond = -(-(b + 1) // 16) * 16
    cond = jnp.concatenate([c, c_ctx[None, :], jnp.zeros((n_cond - b - 1, d), F32)], axis=0)

    def modulation(mod_w, mod_b):
        m = _mm(cond, mod_w, tm=n_cond, tn=512, out_dtype=F32, prologue="silu", epilogue="bias", bias=mod_b)
        lat = [m[:b, k * d:(k + 1) * d].reshape(b, 1, d) for k in range(6)]
        cx = [m[b:b + 1, k * d:(k + 1) * d].reshape(1, 1, d) for k in range(6)]
        return lat, cx

    def tiling(rows):
        return dict(tm=tm, rows_per_batch=s) if rows == s else dict(tm=tm_c, rows_per_batch=None)

    def proj(xs, w, n1, sh, sc, rows, tn, **kw):
        return _mm(xs, w, tn=tn, out_dtype=BF16, prologue="norm_mod", gain=n1, shift=sh, scale=sc,
                   **tiling(rows), **kw)

    def out_resid(y, w, xs, gate, rows):
        return _mm(y, w.astype(BF16), tn=512, out_dtype=F32, epilogue="resid", res=xs, gate=gate, **tiling(rows))

    def ffn(xs, n2, sh, sc, gate, w_in, w_out, rows):
        return _ffn(xs, n2, sh, sc, gate, w_in.astype(BF16), w_out.astype(BF16), th=th, **tiling(rows))

    (sh1, sc1, g1, sh2, sc2, g2), (csh1, csc1, cg1, csh2, csc2, cg2) = modulation(l0_mod_w, l0_mod_b)
    dk = l0_gla_wg_f.shape[1] // GLA_HEADS
    dv = d // GLA_HEADS
    n_in = l0_gla_w_in.shape[1]
    tn0 = 1280
    n_pad = -(-(n_in - 2 * GLA_RANK + LANES) // tn0) * tn0
    w0 = jnp.pad(l0_gla_w_in, ((0, 0), (0, n_pad - n_in))).astype(BF16)
    pl0 = proj(xl, w0, l0_norm1, sh1, sc1, s, tn0)
    pc0 = proj(xc, w0, l0_norm1, csh1, csc1, n_ctx, tn0)
    wgf, bgf = _gate_weights(l0_gla_wg_f, l0_gla_bg_f, 0, dk)
    wgb, bgb = _gate_weights(l0_gla_wg_b, l0_gla_bg_b, GLA_RANK, dk)
    yc, yl = _gla(pc0, pl0, wgf, bgf, wgb, bgb, l0_gla_out_norm, batch=b, dk=dk, dv=dv)
    xl = out_resid(yl, l0_gla_w_out, xl, g1, s)
    xc = out_resid(yc, l0_gla_w_out, xc, cg1, n_ctx)
    xl = ffn(xl, l0_norm2, sh2, sc2, g2, l0_ffn_w_in, l0_ffn_w_out, s)
    xc = ffn(xc, l0_norm2, csh2, csc2, cg2, l0_ffn_w_in, l0_ffn_w_out, n_ctx)

    (sh1, sc1, g1, sh2, sc2, g2), (csh1, csc1, cg1, csh2, csc2, cg2) = modulation(l1_mod_w, l1_mod_b)
    tn1 = 512
    n_heads = d // HEAD_DIM
    cos, sin = _rope_tables(s, tn1)
    nq_t = d // tn1
    nk_t = GQA_KV_HEADS * HEAD_DIM // tn1
    hg1 = _head_gains(l1_gqa_q_norm, l1_gqa_k_norm, nq_t, nk_t, nk_t, tn1)
    w1 = l1_gqa_w_in.astype(BF16)
    pl1 = proj(xl, w1, l1_norm1, sh1, sc1, s, tn1, epilogue="normrope", hgain=hg1, cos=cos, sin=sin,
               n_rope_tiles=nq_t + nk_t)
    pc1 = proj(xc, w1, l1_norm1, csh1, csc1, n_ctx, tn1, epilogue="normrope", hgain=hg1,
               n_rope_tiles=nq_t + nk_t)
    yl = _gqa_attention(pl1, [pl1, pc1], batch=b, n_q_heads=n_heads, n_kv_heads=GQA_KV_HEADS, tq=tq, q_rows=s)
    yc = _gqa_attention(pc1, [pc1], batch=b, n_q_heads=n_heads, n_kv_heads=GQA_KV_HEADS, tq=n_ctx, q_rows=n_ctx)
    xl = out_resid(yl, l1_gqa_w_out, xl, g1, s)
    xc = out_resid(yc, l1_gqa_w_out, xc, cg1, n_ctx)
    xl = ffn(xl, l1_norm2, sh2, sc2, g2, l1_ffn_w_in, l1_ffn_w_out, s)
    xc = ffn(xc, l1_norm2, csh2, csc2, cg2, l1_ffn_w_in, l1_ffn_w_out, n_ctx)

    (sh1, sc1, g1, sh2, sc2, g2), (csh1, csc1, _, _, _, _) = modulation(l2_mod_w, l2_mod_b)
    lam_init = 0.8 - 0.6 * math.exp(-0.3 * 2)
    hg2 = _head_gains(l2_diff_q_norm, l2_diff_k_norm, nq_t, nq_t, nq_t, tn1)
    w2 = l2_diff_w_in.astype(BF16)
    pl2 = proj(xl, w2, l2_norm1, sh1, sc1, s, tn1, epilogue="normrope", hgain=hg2, cos=cos, sin=sin,
               n_rope_tiles=2 * nq_t)
    pc2 = proj(xc, w2, l2_norm1, csh1, csc1, n_ctx, tn1, epilogue="normrope", hgain=hg2, n_rope_tiles=2 * nq_t)
    yl = _diff_attention(pl2, [pl2, pc2], (l2_diff_lq1, l2_diff_lk1, l2_diff_lq2, l2_diff_lk2),
                         l2_diff_out_norm, batch=b, n_heads=n_heads // 2, tq=tq, q_rows=s, lam_init=lam_init)
    xl = out_resid(yl, l2_diff_w_out, xl, g1, s)
    xl = ffn(xl, l2_norm2, sh2, sc2, g2, l2_ffn_w_in, l2_ffn_w_out, s)

    (sh1, sc1, g1, sh2, sc2, g2), _ = modulation(l3_mod_w, l3_mod_b)
    yl = _fnet(xl, l3_norm1, sh1, sc1, batch=b, tm=tm)
    xl = out_resid(yl, l3_fnet_w_out, xl, g1, s)
    xl = ffn(xl, l3_norm2, sh2, sc2, g2, l3_ffn_w_in, l3_ffn_w_out, s)
    return xl.reshape(b, s, d)
```

```python
import functools
import math

import jax
import jax.numpy as jnp
from jax import lax
from jax.experimental import pallas as pl
from jax.experimental.pallas import tpu as pltpu

F32 = jnp.float32
BF16 = jnp.bfloat16

NORM_EPS = 1e-6
ROPE_THETA = 10000.0
GRID_W = 64
HEAD_DIM = 128
GQA_KV_HEADS = 4
GLA_HEADS = 4
GLA_RANK = 16
GLA_TAU = 16.0
GLA_CHUNK = 64
GLA_BLOCK = 4 * GLA_CHUNK
FNET_GROUPS = 4

LANES = 128
BF16_ROWS = 16
STRIP_UNROLL = 8
VMEM_LIMIT = 56 * 1024 * 1024

NT_DIMS = (((1,), (1,)), ((), ()))
TN_DIMS = (((0,), (0,)), ((), ()))


def _params(*sem):
    return pltpu.CompilerParams(dimension_semantics=sem, vmem_limit_bytes=VMEM_LIMIT)


def _rms(x, eps=NORM_EPS):
    return x * lax.rsqrt(jnp.mean(x * x, axis=-1, keepdims=True) + eps)


def _norm_mod_rows(x_ref, xn_ref, gain, shift, scale):
    mult = gain * (1.0 + scale)

    def strip(r, carry):
        rows = pl.ds(pl.multiple_of(r * BF16_ROWS, BF16_ROWS), BF16_ROWS)
        xn_ref[rows, :] = (_rms(x_ref[rows, :]) * mult + shift).astype(BF16)
        return carry

    lax.fori_loop(0, x_ref.shape[0] // BF16_ROWS, strip, 0, unroll=STRIP_UNROLL)


ROPE_STRIP = 256
MXU_WIDTH = 256


def _head_group_matrices():
    src = lax.broadcasted_iota(jnp.int32, (MXU_WIDTH, MXU_WIDTH), 0)
    dst = lax.broadcasted_iota(jnp.int32, (MXU_WIDTH, MXU_WIDTH), 1)
    quarter = HEAD_DIM // 4
    partner = jnp.where((dst // quarter) % 2 == 0, dst + quarter, dst - quarter)
    ones = (src // HEAD_DIM == dst // HEAD_DIM).astype(BF16)
    return ones, (src == partner).astype(BF16)


def _mm_kernel(*refs, prologue, epilogue, n_rope_tiles, rope):
    it = iter(refs)
    x_ref = next(it)
    if prologue == "norm_mod":
        gain_ref, shift_ref, scale_ref = next(it), next(it), next(it)
    w_ref = next(it)
    if epilogue == "bias":
        b_ref = next(it)
    elif epilogue == "resid":
        res_ref, gate_ref = next(it), next(it)
    elif epilogue == "normrope":
        hg_ref, ones_ref = next(it), next(it)
        if rope:
            perm_ref, cos_ref, sin_ref = next(it), next(it), next(it)
    o_ref = next(it)
    if prologue != "cast":
        xn_ref = next(it)
    if epilogue == "normrope":
        acc_ref = next(it)
    j = pl.program_id(1)

    if prologue == "cast":
        a = x_ref[...].astype(BF16)
    else:
        @pl.when(j == 0)
        def _():
            if prologue == "norm_mod":
                _norm_mod_rows(x_ref, xn_ref, gain_ref[...], shift_ref[0], scale_ref[0])
            else:
                xn_ref[...] = jax.nn.silu(x_ref[...]).astype(BF16)

        a = xn_ref[...]
    acc = jnp.dot(a, w_ref[...].astype(BF16), preferred_element_type=F32)
    if epilogue == "store":
        o_ref[...] = acc.astype(o_ref.dtype)
    elif epilogue == "bias":
        o_ref[...] = (acc + b_ref[...]).astype(o_ref.dtype)
    elif epilogue == "resid":
        o_ref[...] = (res_ref[...] + gate_ref[0] * acc).astype(o_ref.dtype)
    elif epilogue == "normrope":
        tm, tn = acc.shape

        @pl.when(j < n_rope_tiles)
        def _():
            acc_ref[...] = acc
            hg = hg_ref[0]
            pw = ones_ref.shape[0]

            def per_group(val, mat_ref):
                return jnp.concatenate([jnp.dot(val[:, c0:c0 + pw], mat_ref[...], preferred_element_type=F32)
                                        for c0 in range(0, tn, pw)], axis=-1)

            def strip(r, carry):
                rows = pl.ds(pl.multiple_of(r * ROPE_STRIP, ROPE_STRIP), ROPE_STRIP)
                blk = acc_ref[rows, :]
                sq = blk * blk
                sq_hi = sq.astype(BF16)
                sq_lo = (sq - sq_hi.astype(F32)).astype(BF16)
                ss = per_group(sq_hi, ones_ref) + per_group(sq_lo, ones_ref)
                y = blk * lax.rsqrt(ss * (1.0 / HEAD_DIM) + NORM_EPS) * hg
                if rope:
                    partner = per_group(y.astype(BF16), perm_ref)
                    y = y * cos_ref[rows, :] + partner * sin_ref[rows, :]
                o_ref[rows, :] = y.astype(o_ref.dtype)
                return carry

            lax.fori_loop(0, tm // ROPE_STRIP, strip, 0)

        @pl.when(j >= n_rope_tiles)
        def _():
            o_ref[...] = acc.astype(o_ref.dtype)


def _mm(x, w, *, tm, tn, out_dtype, name, prologue="cast", epilogue="store", rows_per_batch=None,
        gain=None, shift=None, scale=None, bias=None, res=None, gate=None,
        hgain=None, cos=None, sin=None, n_rope_tiles=0):
    t, k = x.shape
    n = w.shape[1]
    assert t % tm == 0 and n % tn == 0, (t, tm, n, tn)
    rows_per_batch = rows_per_batch or t
    assert rows_per_batch % tm == 0
    tiles_per_batch = rows_per_batch // tm
    rope = cos is not None
    once = dict(pipeline_mode=pl.Buffered(1))

    def batch_of(i):
        return i // tiles_per_batch

    args = [x]
    specs = [pl.BlockSpec((tm, k), lambda i, j: (i, 0))]
    if prologue == "norm_mod":
        nb = shift.shape[0]
        bsel = (lambda i: batch_of(i)) if nb > 1 else (lambda i: 0)
        args += [gain.reshape(1, k), shift, scale]
        specs += [pl.BlockSpec((1, k), lambda i, j: (0, 0)),
                  pl.BlockSpec((1, 1, k), lambda i, j: (bsel(i), 0, 0)),
                  pl.BlockSpec((1, 1, k), lambda i, j: (bsel(i), 0, 0))]
    args.append(w)
    specs.append(pl.BlockSpec((k, tn), lambda i, j: (0, j), **(once if tn == n else {})))
    if epilogue == "bias":
        args.append(bias.reshape(1, n))
        specs.append(pl.BlockSpec((1, tn), lambda i, j: (0, j)))
    elif epilogue == "resid":
        nb = gate.shape[0]
        gsel = (lambda i: batch_of(i)) if nb > 1 else (lambda i: 0)
        args += [res, gate]
        specs += [pl.BlockSpec((tm, tn), lambda i, j: (i, j)),
                  pl.BlockSpec((1, 1, tn), lambda i, j: (gsel(i), 0, j))]
    elif epilogue == "normrope":
        ones, perm = _head_group_matrices()
        const = pl.BlockSpec(ones.shape, lambda i, j: (0, 0))
        args += [hgain, ones]
        specs += [pl.BlockSpec((1, 1, tn), lambda i, j: (j, 0, 0)), const]
        if rope:
            args += [perm, cos, sin]
            specs += [const] + [pl.BlockSpec((tm, tn), lambda i, j: (i % tiles_per_batch, 0))] * 2
    scratch = []
    if prologue != "cast":
        scratch.append(pltpu.VMEM((tm, k), BF16))
    if epilogue == "normrope":
        scratch.append(pltpu.VMEM((tm, tn), F32))
    kern = functools.partial(_mm_kernel, prologue=prologue, epilogue=epilogue,
                             n_rope_tiles=n_rope_tiles, rope=rope)
    return pl.pallas_call(
        kern,
        grid=(t // tm, n // tn),
        in_specs=specs,
        out_specs=pl.BlockSpec((tm, tn), lambda i, j: (i, j)),
        out_shape=jax.ShapeDtypeStruct((t, n), out_dtype),
        scratch_shapes=scratch,
        compiler_params=_params("parallel", "arbitrary"),
        name=name,
    )(*args)


FFN_ROW_SPLIT = 2


def _ffn_kernel(x_ref, gain_ref, shift_ref, scale_ref, gate_ref, wg_ref, wu_ref, wo_ref, o_ref, xn_ref):
    j = pl.program_id(1)

    @pl.when(j == 0)
    def _():
        _norm_mod_rows(x_ref, xn_ref, gain_ref[...], shift_ref[0], scale_ref[0])
        o_ref[...] = jnp.zeros_like(o_ref)

    hr = o_ref.shape[0] // FFN_ROW_SPLIT
    for h in range(FFN_ROW_SPLIT):
        rows = slice(h * hr, (h + 1) * hr)
        xn = xn_ref[rows, :]
        g = jnp.dot(xn, wg_ref[...], preferred_element_type=F32)
        u = jnp.dot(xn, wu_ref[...], preferred_element_type=F32)
        a = (jax.nn.silu(g) * u).astype(BF16)
        o_ref[rows, :] += jnp.dot(a, wo_ref[...], preferred_element_type=F32)

    @pl.when(j == pl.num_programs(1) - 1)
    def _():
        o_ref[...] = x_ref[...] + gate_ref[0] * o_ref[...]


def _ffn(x, gain, shift, scale, gate, w_in, w_out, *, tm, th, name, rows_per_batch=None):
    t, d = x.shape
    hidden = w_out.shape[0]
    assert t % tm == 0 and hidden % th == 0
    nh = hidden // th
    rows_per_batch = rows_per_batch or t
    tiles_per_batch = rows_per_batch // tm
    nb = shift.shape[0]
    bsel = (lambda i: i // tiles_per_batch) if nb > 1 else (lambda i: 0)
    vec = pl.BlockSpec((1, 1, d), lambda i, j: (bsel(i), 0, 0))
    return pl.pallas_call(
        _ffn_kernel,
        grid=(t // tm, nh),
        in_specs=[pl.BlockSpec((tm, d), lambda i, j: (i, 0), pipeline_mode=pl.Buffered(1)),
                  pl.BlockSpec((1, d), lambda i, j: (0, 0)),
                  vec, vec, vec,
                  pl.BlockSpec((d, th), lambda i, j: (0, j)),
                  pl.BlockSpec((d, th), lambda i, j: (0, nh + j)),
                  pl.BlockSpec((th, d), lambda i, j: (j, 0))],
        out_specs=pl.BlockSpec((tm, d), lambda i, j: (i, 0)),
        out_shape=jax.ShapeDtypeStruct((t, d), F32),
        scratch_shapes=[pltpu.VMEM((tm, d), BF16)],
        compiler_params=_params("parallel", "arbitrary"),
        name=name,
    )(x, gain.reshape(1, d), shift, scale, gate, w_in, w_in, w_out)


def _fill_kv(k_s, v_s, kv_refs):
    off = 0
    for k_ref, v_ref in kv_refs:
        n = k_ref.shape[0]
        k_s[off:off + n, :] = k_ref[...]
        v_s[off:off + n, :] = v_ref[...]
        off += n


def _softmax_parts(q, k):
    s = lax.dot_general(q, k, NT_DIMS, preferred_element_type=F32)
    p = jnp.exp2(s - jnp.max(s, axis=-1, keepdims=True))
    return p, jnp.sum(p, axis=-1, keepdims=True)


def _gqa_kernel(*refs, n_kv_src, n_group):
    q_ref = refs[0]
    kv_refs = [(refs[1 + 2 * s], refs[2 + 2 * s]) for s in range(n_kv_src)]
    o_ref, k_s, v_s = refs[1 + 2 * n_kv_src:]

    @pl.when(pl.program_id(2) == 0)
    def _():
        _fill_kv(k_s, v_s, kv_refs)

    k = k_s[...]
    v = v_s[...]
    for g in range(n_group):
        sl = slice(g * HEAD_DIM, (g + 1) * HEAD_DIM)
        p, l = _softmax_parts(q_ref[:, sl], k)
        o = jnp.dot(p.astype(BF16), v, preferred_element_type=F32)
        o_ref[:, sl] = (o / l).astype(o_ref.dtype)


def _gqa_attention(q_src, kv_srcs, *, batch, n_q_heads, n_kv_heads, tq, q_rows, name):
    group = n_q_heads // n_kv_heads
    gw = group * HEAD_DIM
    nq = q_rows // tq
    k_blk0 = n_q_heads
    v_blk0 = n_q_heads + n_kv_heads
    specs = [pl.BlockSpec((tq, gw), lambda b, h, i: (b * nq + i, h))]
    args = [q_src]
    total = 0
    for src in kv_srcs:
        rows = src.shape[0] // batch
        total += rows
        specs += [pl.BlockSpec((rows, HEAD_DIM), lambda b, h, i: (b, k_blk0 + h)),
                  pl.BlockSpec((rows, HEAD_DIM), lambda b, h, i: (b, v_blk0 + h))]
        args += [src, src]
    kern = functools.partial(_gqa_kernel, n_kv_src=len(kv_srcs), n_group=group)
    return pl.pallas_call(
        kern,
        grid=(batch, n_kv_heads, nq),
        in_specs=specs,
        out_specs=pl.BlockSpec((tq, gw), lambda b, h, i: (b * nq + i, h)),
        out_shape=jax.ShapeDtypeStruct((batch * q_rows, n_q_heads * HEAD_DIM), BF16),
        scratch_shapes=[pltpu.VMEM((total, HEAD_DIM), BF16), pltpu.VMEM((total, HEAD_DIM), BF16)],
        compiler_params=_params("parallel", "parallel", "arbitrary"),
        name=name,
    )(*args)


DIFF_SUBTILE = 256


def _diff_kernel(*refs, n_kv_src, lam_init):
    q_ref = refs[0]
    kv_refs = [(refs[1 + 2 * s], refs[2 + 2 * s]) for s in range(n_kv_src)]
    lq1, lk1, lq2, lk2, gain_ref, o_ref, k_s, v_s = refs[1 + 2 * n_kv_src:]

    @pl.when(pl.program_id(2) == 0)
    def _():
        _fill_kv(k_s, v_s, kv_refs)

    lam = (jnp.exp(jnp.sum(lq1[...] * lk1[...], axis=-1, keepdims=True))
           - jnp.exp(jnp.sum(lq2[...] * lk2[...], axis=-1, keepdims=True)) + lam_init)
    dh = HEAD_DIM
    for r0 in range(0, q_ref.shape[0], DIFF_SUBTILE):
        rows = slice(r0, r0 + DIFF_SUBTILE)
        p0, l0 = _softmax_parts(q_ref[rows, 0:dh], k_s[:, 0:dh])
        p1, l1 = _softmax_parts(q_ref[rows, dh:2 * dh], k_s[:, dh:2 * dh])
        w = p0 * (1.0 / l0) - p1 * (lam / l1)
        o = jnp.dot(w.astype(BF16), v_s[...], preferred_element_type=F32)
        o_ref[rows, :] = ((_rms(o) * gain_ref[...]) * (1.0 - lam_init)).astype(o_ref.dtype)


def _diff_attention(q_src, kv_srcs, lams, out_gain, *, batch, n_heads, tq, q_rows, lam_init, name):
    hw = 2 * HEAD_DIM
    nq = q_rows // tq
    specs = [pl.BlockSpec((tq, hw), lambda b, h, i: (b * nq + i, h))]
    args = [q_src]
    total = 0
    for src in kv_srcs:
        rows = src.shape[0] // batch
        total += rows
        specs += [pl.BlockSpec((rows, hw), lambda b, h, i: (b, n_heads + h)),
                  pl.BlockSpec((rows, hw), lambda b, h, i: (b, 2 * n_heads + h))]
        args += [src, src]
    small = pl.BlockSpec((1, HEAD_DIM), lambda b, h, i: (0, 0))
    specs += [small] * 4 + [pl.BlockSpec((1, hw), lambda b, h, i: (0, 0))]
    args += [v.reshape(1, HEAD_DIM) for v in lams] + [out_gain.reshape(1, hw)]
    kern = functools.partial(_diff_kernel, n_kv_src=len(kv_srcs), lam_init=lam_init)
    return pl.pallas_call(
        kern,
        grid=(batch, n_heads, nq),
        in_specs=specs,
        out_specs=pl.BlockSpec((tq, hw), lambda b, h, i: (b * nq + i, h)),
        out_shape=jax.ShapeDtypeStruct((batch * q_rows, n_heads * hw), BF16),
        scratch_shapes=[pltpu.VMEM((total, hw), BF16), pltpu.VMEM((total, hw), BF16)],
        compiler_params=_params("parallel", "parallel", "arbitrary"),
        name=name,
    )(*args)


def _gla_block(q, k, v, z, wg, bg, state_ref, *, reverse):
    r, dk = q.shape
    c = GLA_CHUNK
    nc = r // c
    g = jax.nn.log_sigmoid(jnp.dot(z, wg, preferred_element_type=F32) + bg) / GLA_TAU
    row = lax.broadcasted_iota(jnp.int32, (r, r), 0)
    col = lax.broadcasted_iota(jnp.int32, (r, r), 1)
    order = (col >= row) if reverse else (col <= row)
    tri_f = jnp.where(row // c == col // c, jnp.where(order, 1.0, 0.0), 0.0)
    seen = tri_f > 0.5
    tri = tri_f.astype(BF16)
    g_hi = g.astype(BF16)
    g_lo = (g - g_hi.astype(F32)).astype(BF16)
    cum = (jnp.dot(tri, g_hi, preferred_element_type=F32)
           + jnp.dot(tri, g_lo, preferred_element_type=F32))
    mid = c // 2 if reverse else c // 2 - 1
    last = 0 if reverse else c - 1

    def chunk_row(idx):
        return jnp.concatenate(
            [jnp.broadcast_to(cum[ci * c + idx:ci * c + idx + 1, :], (c, dk)) for ci in range(nc)], axis=0)

    cum_mid = chunk_row(mid)
    cum_last = chunk_row(last)
    qs = (q * jnp.exp(cum - cum_mid)).astype(BF16)
    ks = (k * jnp.exp(cum_mid - cum)).astype(BF16)
    a = lax.dot_general(qs, ks, NT_DIMS, preferred_element_type=F32)
    a = jnp.where(seen, a, 0.0).astype(BF16)
    o_intra = jnp.dot(a, v, preferred_element_type=F32)
    q_inter = (q * jnp.exp(cum)).astype(BF16)
    k_carry = (k * jnp.exp(cum_last - cum)).astype(BF16)
    st = state_ref[...]
    o_inter = [None] * nc
    for ci in (range(nc - 1, -1, -1) if reverse else range(nc)):
        rows = slice(ci * c, (ci + 1) * c)
        o_inter[ci] = lax.dot_general(q_inter[rows], st.astype(BF16), NT_DIMS, preferred_element_type=F32)
        decay = jnp.exp(cum[ci * c + last:ci * c + last + 1, :])
        st = st * decay + lax.dot_general(v[rows], k_carry[rows], TN_DIMS, preferred_element_type=F32)
    state_ref[...] = st
    return o_intra + jnp.concatenate(o_inter, axis=0)


def _gla_kernel(qc_ref, kc_ref, vc_ref, rc_ref, zc_ref, ql_ref, kl_ref, vl_ref, rl_ref, zl_ref,
                wgf_ref, bgf_ref, wgb_ref, bgb_ref, gain_ref, oc_ref, ol_ref, state_ref, of_ref):
    blk = GLA_BLOCK
    n_ctx = qc_ref.shape[0] // blk
    n_lat = ql_ref.shape[0] // blk
    q_scale = qc_ref.shape[1] ** -0.5

    def load(refs, bi):
        q_ref, k_ref, v_ref, r_ref, z_ref = refs
        rows = pl.ds(pl.multiple_of(bi * blk, blk), blk)
        return rows, q_ref[rows, :].astype(F32) * q_scale, k_ref[rows, :].astype(F32), v_ref[rows, :], z_ref[rows, :]

    ctx_refs = (qc_ref, kc_ref, vc_ref, rc_ref, zc_ref)
    lat_refs = (ql_ref, kl_ref, vl_ref, rl_ref, zl_ref)

    def fwd(refs, base):
        def body(bi, carry):
            _, q, k, v, z = load(refs, bi)
            o = _gla_block(q, k, v, z, wgf_ref[0], bgf_ref[0], state_ref, reverse=False)
            of_ref[pl.ds(pl.multiple_of((base + bi) * blk, blk), blk), :] = o
            return carry
        return body

    def bwd(refs, base, out_ref, n):
        def body(t, carry):
            bi = n - 1 - t
            rows, q, k, v, z = load(refs, bi)
            o = _gla_block(q, k, v, z, wgb_ref[0], bgb_ref[0], state_ref, reverse=True)
            o = o + of_ref[pl.ds(pl.multiple_of((base + bi) * blk, blk), blk), :]
            y = (_rms(o) * gain_ref[...]) * jax.nn.silu(refs[3][rows, :].astype(F32))
            out_ref[rows, :] = y.astype(out_ref.dtype)
            return carry
        return body

    state_ref[...] = jnp.zeros_like(state_ref)
    lax.fori_loop(0, n_ctx, fwd(ctx_refs, 0), 0)
    lax.fori_loop(0, n_lat, fwd(lat_refs, n_ctx), 0)
    state_ref[...] = jnp.zeros_like(state_ref)
    lax.fori_loop(0, n_ctx, bwd(ctx_refs, 0, oc_ref, n_ctx), 0)
    lax.fori_loop(0, n_lat, bwd(lat_refs, n_ctx, ol_ref, n_lat), 0)


def _gla(p_ctx, p_lat, wgf, bgf, wgb, bgb, out_gain, *, batch, dk, dv):
    h = GLA_HEADS
    rows_c = p_ctx.shape[0] // batch
    rows_l = p_lat.shape[0] // batch
    assert rows_c % GLA_BLOCK == 0 and rows_l % GLA_BLOCK == 0
    zblk = (2 * h * dk + 2 * h * dv) // LANES
    k0 = h
    v0 = (2 * h * dk) // dv
    r0 = v0 + h

    def stream(rows):
        return [pl.BlockSpec((rows, dk), lambda b, hh: (b, hh)),
                pl.BlockSpec((rows, dk), lambda b, hh: (b, k0 + hh)),
                pl.BlockSpec((rows, dv), lambda b, hh: (b, v0 + hh)),
                pl.BlockSpec((rows, dv), lambda b, hh: (b, r0 + hh)),
                pl.BlockSpec((rows, LANES), lambda b, hh: (b, zblk))]

    wspec = pl.BlockSpec((1, LANES, dk), lambda b, hh: (hh, 0, 0))
    bspec = pl.BlockSpec((1, 1, dk), lambda b, hh: (hh, 0, 0))
    return pl.pallas_call(
        _gla_kernel,
        grid=(batch, h),
        in_specs=stream(rows_c) + stream(rows_l) + [wspec, bspec, wspec, bspec,
                                                    pl.BlockSpec((1, dv), lambda b, hh: (0, 0))],
        out_specs=[pl.BlockSpec((rows_c, dv), lambda b, hh: (b, hh)),
                   pl.BlockSpec((rows_l, dv), lambda b, hh: (b, hh))],
        out_shape=[jax.ShapeDtypeStruct((batch * rows_c, h * dv), BF16),
                   jax.ShapeDtypeStruct((batch * rows_l, h * dv), BF16)],
        scratch_shapes=[pltpu.VMEM((dv, dk), F32), pltpu.VMEM((rows_c + rows_l, dv), F32)],
        compiler_params=_params("parallel", "parallel"),
        name="gla",
    )(*([p_ctx] * 5 + [p_lat] * 5 + [wgf, bgf, wgb, bgb, out_gain.reshape(1, dv)]))


def _dft_tables(n):
    idx = jnp.arange(n, dtype=jnp.int32)
    ang = ((idx[:, None] * idx[None, :]) % n).astype(F32) * (2.0 * math.pi / n)
    return jnp.cos(ang).astype(BF16), jnp.sin(ang).astype(BF16)


def _fnet_chan_kernel(x_ref, gain_ref, shift_ref, scale_ref, cc_ref, sc_ref, p_ref, q_ref, xn_ref):
    j = pl.program_id(1)
    gd = cc_ref.shape[0]

    @pl.when(j == 0)
    def _():
        _norm_mod_rows(x_ref, xn_ref, gain_ref[...], shift_ref[0], scale_ref[0])

    hg = xn_ref[:, pl.ds(pl.multiple_of(j * gd, gd), gd)]
    p_ref[...] = jnp.dot(hg, cc_ref[...], preferred_element_type=F32).astype(p_ref.dtype)
    q_ref[...] = jnp.dot(hg, sc_ref[...], preferred_element_type=F32).astype(q_ref.dtype)


def _fnet_seq_kernel(cs_ref, ss_ref, p_ref, q_ref, o_ref, *, inv_norm):
    acc = (jnp.dot(cs_ref[...], p_ref[...], preferred_element_type=F32)
           - jnp.dot(ss_ref[...], q_ref[...], preferred_element_type=F32))
    o_ref[...] = (acc * inv_norm).astype(o_ref.dtype)


def _fnet(x, gain, shift, scale, *, batch, tm):
    t, d = x.shape
    s = t // batch
    gd = d // FNET_GROUPS
    tiles_per_batch = s // tm
    cc, sc = _dft_tables(gd)
    cs, ss = _dft_tables(s)
    vec = pl.BlockSpec((1, 1, d), lambda i, j: (i // tiles_per_batch, 0, 0))
    tab = pl.BlockSpec((gd, gd), lambda i, j: (0, 0))
    blk = pl.BlockSpec((tm, gd), lambda i, j: (i, j))
    p, q = pl.pallas_call(
        _fnet_chan_kernel,
        grid=(t // tm, FNET_GROUPS),
        in_specs=[pl.BlockSpec((tm, d), lambda i, j: (i, 0)),
                  pl.BlockSpec((1, d), lambda i, j: (0, 0)), vec, vec, tab, tab],
        out_specs=[blk, blk],
        out_shape=[jax.ShapeDtypeStruct((t, d), BF16)] * 2,
        scratch_shapes=[pltpu.VMEM((tm, d), BF16)],
        compiler_params=_params("parallel", "arbitrary"),
        name="fnet_chan",
    )(x, gain.reshape(1, d), shift, scale, cc, sc)
    rows = pl.BlockSpec((tm, s), lambda b, j, i: (i, 0))
    cols = pl.BlockSpec((s, gd), lambda b, j, i: (b, j))
    return pl.pallas_call(
        functools.partial(_fnet_seq_kernel, inv_norm=float((s * gd) ** -0.5)),
        grid=(batch, d // gd, tiles_per_batch),
        in_specs=[rows, rows, cols, cols],
        out_specs=pl.BlockSpec((tm, gd), lambda b, j, i: (b * tiles_per_batch + i, j)),
        out_shape=jax.ShapeDtypeStruct((t, d), BF16),
        compiler_params=_params("parallel", "parallel", "arbitrary"),
        name="fnet_seq",
    )(cs, ss, p, q)


def _rope_tables(n_tokens, width):
    t = jnp.arange(n_tokens)
    row = (t // GRID_W).astype(F32)
    col = (t % GRID_W).astype(F32)
    half = HEAD_DIM // 2
    inv_freq = ROPE_THETA ** (-jnp.arange(0, half, 2, dtype=F32) / half)
    ang_r = row[:, None] * inv_freq[None, :]
    ang_c = col[:, None] * inv_freq[None, :]
    ang = jnp.concatenate([ang_r, ang_r, ang_c, ang_c], axis=-1)
    sign = jnp.concatenate([-jnp.ones((half // 2,), F32), jnp.ones((half // 2,), F32)] * 2)
    reps = width // HEAD_DIM
    return jnp.tile(jnp.cos(ang), (1, reps)), jnp.tile(jnp.sin(ang) * sign, (1, reps))


def _head_gains(q_gain, k_gain, n_q_tiles, n_k_tiles, n_v_tiles, tn):
    reps = tn // HEAD_DIM
    qg = jnp.tile(q_gain.astype(F32) * (HEAD_DIM ** -0.5 * math.log2(math.e)), reps)
    kg = jnp.tile(k_gain.astype(F32), reps)
    rows = [qg] * n_q_tiles + [kg] * n_k_tiles + [jnp.ones((tn,), F32)] * n_v_tiles
    return jnp.stack(rows)[:, None, :]


def _gate_weights(wg, bg, lane0, dk):
    r = wg.shape[0]
    w = wg.reshape(r, GLA_HEADS, dk).transpose(1, 0, 2)
    w = jnp.pad(w, ((0, 0), (lane0, LANES - lane0 - r), (0, 0))).astype(BF16)
    return w, bg.reshape(GLA_HEADS, 1, dk).astype(F32)


def kernel(x, c, ctx, c_ctx, l0_mod_w, l0_mod_b, l0_norm1, l0_gla_w_in, l0_gla_wg_f, l0_gla_bg_f, l0_gla_wg_b, l0_gla_bg_b, l0_gla_out_norm, l0_gla_w_out, l0_norm2, l0_ffn_w_in, l0_ffn_w_out, l1_mod_w, l1_mod_b, l1_norm1, l1_gqa_w_in, l1_gqa_q_norm, l1_gqa_k_norm, l1_gqa_w_out, l1_norm2, l1_ffn_w_in, l1_ffn_w_out, l2_mod_w, l2_mod_b, l2_norm1, l2_diff_w_in, l2_diff_q_norm, l2_diff_k_norm, l2_diff_lq1, l2_diff_lk1, l2_diff_lq2, l2_diff_lk2, l2_diff_out_norm, l2_diff_w_out, l2_norm2, l2_ffn_w_in, l2_ffn_w_out, l3_mod_w, l3_mod_b, l3_norm1, l3_fnet_w_out, l3_norm2, l3_ffn_w_in, l3_ffn_w_out):
    b, s, d = x.shape
    n_ctx = ctx.shape[1]
    xl = x.reshape(b * s, d)
    xc = ctx.reshape(b * n_ctx, d)
    tm = math.gcd(s, 1024)
    tm_c = math.gcd(b * n_ctx, 1024)
    tm_r = math.gcd(s, 512)
    tm_rc = math.gcd(b * n_ctx, 512)
    th = math.gcd(l0_ffn_w_out.shape[0], 512)
    tq = math.gcd(s, 256)

    n_cond = -(-(b + 1) // BF16_ROWS) * BF16_ROWS
    cond = jnp.concatenate([c, c_ctx[None, :], jnp.zeros((n_cond - b - 1, d), F32)], axis=0)

    def modulation(mod_w, mod_b):
        m = _mm(cond, mod_w, tm=n_cond, tn=512, out_dtype=F32, prologue="silu", epilogue="bias", bias=mod_b,
                name="modulation")
        lat = [m[:b, k * d:(k + 1) * d].reshape(b, 1, d) for k in range(6)]
        cx = [m[b:b + 1, k * d:(k + 1) * d].reshape(1, 1, d) for k in range(6)]
        return lat, cx

    def tiling(rows, resident=False):
        if rows == s:
            return dict(tm=tm_r if resident else tm, rows_per_batch=s)
        return dict(tm=tm_rc if resident else tm_c, rows_per_batch=None)

    def proj(xs, w, n1, sh, sc, rows, tn, name, **kw):
        return _mm(xs, w, tn=tn, out_dtype=BF16, prologue="norm_mod", gain=n1, shift=sh, scale=sc, name=name,
                   **tiling(rows), **kw)

    def out_resid(y, w, xs, gate, rows):
        return _mm(y, w.astype(BF16), tn=w.shape[1], out_dtype=F32, epilogue="resid", res=xs, gate=gate,
                   name="out_resid", **tiling(rows, resident=True))

    def ffn(xs, n2, sh, sc, gate, w_in, w_out, rows):
        return _ffn(xs, n2, sh, sc, gate, w_in.astype(BF16), w_out.astype(BF16), th=th, name="ffn", **tiling(rows))

    (sh1, sc1, g1, sh2, sc2, g2), (csh1, csc1, cg1, csh2, csc2, cg2) = modulation(l0_mod_w, l0_mod_b)
    dk = l0_gla_wg_f.shape[1] // GLA_HEADS
    dv = d // GLA_HEADS
    n_in = l0_gla_w_in.shape[1]
    tn0 = 1280
    n_pad = -(-(n_in - 2 * GLA_RANK + LANES) // tn0) * tn0
    w0 = jnp.pad(l0_gla_w_in, ((0, 0), (0, n_pad - n_in))).astype(BF16)
    pl0 = proj(xl, w0, l0_norm1, sh1, sc1, s, tn0, "gla_proj")
    pc0 = proj(xc, w0, l0_norm1, csh1, csc1, n_ctx, tn0, "gla_proj_ctx")
    wgf, bgf = _gate_weights(l0_gla_wg_f, l0_gla_bg_f, 0, dk)
    wgb, bgb = _gate_weights(l0_gla_wg_b, l0_gla_bg_b, GLA_RANK, dk)
    yc, yl = _gla(pc0, pl0, wgf, bgf, wgb, bgb, l0_gla_out_norm, batch=b, dk=dk, dv=dv)
    xl = out_resid(yl, l0_gla_w_out, xl, g1, s)
    xc = out_resid(yc, l0_gla_w_out, xc, cg1, n_ctx)
    xl = ffn(xl, l0_norm2, sh2, sc2, g2, l0_ffn_w_in, l0_ffn_w_out, s)
    xc = ffn(xc, l0_norm2, csh2, csc2, cg2, l0_ffn_w_in, l0_ffn_w_out, n_ctx)

    (sh1, sc1, g1, sh2, sc2, g2), (csh1, csc1, cg1, csh2, csc2, cg2) = modulation(l1_mod_w, l1_mod_b)
    tn1 = 512
    n_heads = d // HEAD_DIM
    cos, sin = _rope_tables(s, tn1)
    nq_t = d // tn1
    nk_t = GQA_KV_HEADS * HEAD_DIM // tn1
    hg1 = _head_gains(l1_gqa_q_norm, l1_gqa_k_norm, nq_t, nk_t, nk_t, tn1)
    w1 = l1_gqa_w_in.astype(BF16)
    pl1 = proj(xl, w1, l1_norm1, sh1, sc1, s, tn1, "gqa_proj", epilogue="normrope", hgain=hg1, cos=cos, sin=sin,
               n_rope_tiles=nq_t + nk_t)
    pc1 = proj(xc, w1, l1_norm1, csh1, csc1, n_ctx, tn1, "gqa_proj_ctx", epilogue="normrope", hgain=hg1,
               n_rope_tiles=nq_t + nk_t)
    yl = _gqa_attention(pl1, [pl1, pc1], batch=b, n_q_heads=n_heads, n_kv_heads=GQA_KV_HEADS, tq=tq, q_rows=s,
                        name="gqa_attn")
    yc = _gqa_attention(pc1, [pc1], batch=b, n_q_heads=n_heads, n_kv_heads=GQA_KV_HEADS, tq=n_ctx, q_rows=n_ctx,
                        name="gqa_attn_ctx")
    xl = out_resid(yl, l1_gqa_w_out, xl, g1, s)
    xc = out_resid(yc, l1_gqa_w_out, xc, cg1, n_ctx)
    xl = ffn(xl, l1_norm2, sh2, sc2, g2, l1_ffn_w_in, l1_ffn_w_out, s)
    xc = ffn(xc, l1_norm2, csh2, csc2, cg2, l1_ffn_w_in, l1_ffn_w_out, n_ctx)

    (sh1, sc1, g1, sh2, sc2, g2), (csh1, csc1, _, _, _, _) = modulation(l2_mod_w, l2_mod_b)
    lam_init = 0.8 - 0.6 * math.exp(-0.3 * 2)
    hg2 = _head_gains(l2_diff_q_norm, l2_diff_k_norm, nq_t, nq_t, nq_t, tn1)
    w2 = l2_diff_w_in.astype(BF16)
    pl2 = proj(xl, w2, l2_norm1, sh1, sc1, s, tn1, "diff_proj", epilogue="normrope", hgain=hg2, cos=cos, sin=sin,
               n_rope_tiles=2 * nq_t)
    pc2 = proj(xc, w2, l2_norm1, csh1, csc1, n_ctx, tn1, "diff_proj_ctx", epilogue="normrope", hgain=hg2,
               n_rope_tiles=2 * nq_t)
    yl = _diff_attention(pl2, [pl2, pc2], (l2_diff_lq1, l2_diff_lk1, l2_diff_lq2, l2_diff_lk2),
                         l2_diff_out_norm, batch=b, n_heads=n_heads // 2, tq=2 * tq, q_rows=s, lam_init=lam_init,
                         name="diff_attn")
    xl = out_resid(yl, l2_diff_w_out, xl, g1, s)
    xl = ffn(xl, l2_norm2, sh2, sc2, g2, l2_ffn_w_in, l2_ffn_w_out, s)

    (sh1, sc1, g1, sh2, sc2, g2), _ = modulation(l3_mod_w, l3_mod_b)
    yl = _fnet(xl, l3_norm1, sh1, sc1, batch=b, tm=tm_r)
    xl = out_resid(yl, l3_fnet_w_out, xl, g1, s)
    xl = ffn(xl, l3_norm2, sh2, sc2, g2, l3_ffn_w_in, l3_ffn_w_out, s)
    return xl.reshape(b, s, d)
```

```python
import functools
import math

import jax
import jax.numpy as jnp
from jax import lax
from jax.experimental import pallas as pl
from jax.experimental.pallas import tpu as pltpu

F32 = jnp.float32
BF16 = jnp.bfloat16

NORM_EPS = 1e-6
ROPE_THETA = 10000.0
GRID_W = 64
HEAD_DIM = 128
GQA_KV_HEADS = 4
GLA_HEADS = 4
GLA_RANK = 16
GLA_TAU = 16.0
GLA_CHUNK = 64
GLA_BLOCK = 4 * GLA_CHUNK
FNET_GROUPS = 4

LANES = 128
BF16_ROWS = 16
STRIP_UNROLL = 8
VMEM_LIMIT = 56 * 1024 * 1024

NT_DIMS = (((1,), (1,)), ((), ()))
TN_DIMS = (((0,), (0,)), ((), ()))


def _params(*sem):
    return pltpu.CompilerParams(dimension_semantics=sem, vmem_limit_bytes=VMEM_LIMIT)


def _rms(x, eps=NORM_EPS):
    return x * lax.rsqrt(jnp.mean(x * x, axis=-1, keepdims=True) + eps)


def _norm_mod_rows(x_ref, xn_ref, gain, shift, scale):
    mult = gain * (1.0 + scale)

    def strip(r, carry):
        rows = pl.ds(pl.multiple_of(r * BF16_ROWS, BF16_ROWS), BF16_ROWS)
        xn_ref[rows, :] = (_rms(x_ref[rows, :]) * mult + shift).astype(BF16)
        return carry

    lax.fori_loop(0, x_ref.shape[0] // BF16_ROWS, strip, 0, unroll=STRIP_UNROLL)


ROPE_STRIP = 256
MXU_WIDTH = 256


def _head_group_matrices():
    src = lax.broadcasted_iota(jnp.int32, (MXU_WIDTH, MXU_WIDTH), 0)
    dst = lax.broadcasted_iota(jnp.int32, (MXU_WIDTH, MXU_WIDTH), 1)
    quarter = HEAD_DIM // 4
    partner = jnp.where((dst // quarter) % 2 == 0, dst + quarter, dst - quarter)
    ones = (src // HEAD_DIM == dst // HEAD_DIM).astype(BF16)
    return ones, (src == partner).astype(BF16)


def _mm_kernel(*refs, prologue, epilogue, n_rope_tiles, rope):
    it = iter(refs)
    x_ref = next(it)
    if prologue == "norm_mod":
        gain_ref, shift_ref, scale_ref = next(it), next(it), next(it)
    w_ref = next(it)
    if epilogue == "bias":
        b_ref = next(it)
    elif epilogue == "resid":
        res_ref, gate_ref = next(it), next(it)
    elif epilogue == "normrope":
        hg_ref, ones_ref = next(it), next(it)
        if rope:
            perm_ref, cos_ref, sin_ref = next(it), next(it), next(it)
    o_ref = next(it)
    if prologue != "cast":
        xn_ref = next(it)
    if epilogue == "normrope":
        acc_ref = next(it)
    j = pl.program_id(1)

    if prologue == "cast":
        a = x_ref[...].astype(BF16)
    else:
        @pl.when(j == 0)
        def _():
            if prologue == "norm_mod":
                _norm_mod_rows(x_ref, xn_ref, gain_ref[...], shift_ref[0], scale_ref[0])
            else:
                xn_ref[...] = jax.nn.silu(x_ref[...]).astype(BF16)

        a = xn_ref[...]
    acc = jnp.dot(a, w_ref[...].astype(BF16), preferred_element_type=F32)
    if epilogue == "store":
        o_ref[...] = acc.astype(o_ref.dtype)
    elif epilogue == "bias":
        o_ref[...] = (acc + b_ref[...]).astype(o_ref.dtype)
    elif epilogue == "resid":
        o_ref[...] = (res_ref[...] + gate_ref[0] * acc).astype(o_ref.dtype)
    elif epilogue == "normrope":
        tm, tn = acc.shape

        @pl.when(j < n_rope_tiles)
        def _():
            acc_ref[...] = acc
            hg = hg_ref[0]
            pw = ones_ref.shape[0]

            def per_group(val, mat_ref):
                return jnp.concatenate([jnp.dot(val[:, c0:c0 + pw], mat_ref[...], preferred_element_type=F32)
                                        for c0 in range(0, tn, pw)], axis=-1)

            def strip(r, carry):
                rows = pl.ds(pl.multiple_of(r * ROPE_STRIP, ROPE_STRIP), ROPE_STRIP)
                blk = acc_ref[rows, :]
                sq = blk * blk
                sq_hi = sq.astype(BF16)
                sq_lo = (sq - sq_hi.astype(F32)).astype(BF16)
                ss = per_group(sq_hi, ones_ref) + per_group(sq_lo, ones_ref)
                y = blk * lax.rsqrt(ss * (1.0 / HEAD_DIM) + NORM_EPS) * hg
                if rope:
                    partner = per_group(y.astype(BF16), perm_ref)
                    y = y * cos_ref[rows, :] + partner * sin_ref[rows, :]
                o_ref[rows, :] = y.astype(o_ref.dtype)
                return carry

            lax.fori_loop(0, tm // ROPE_STRIP, strip, 0, unroll=4)

        @pl.when(j >= n_rope_tiles)
        def _():
            o_ref[...] = acc.astype(o_ref.dtype)


def _mm(x, w, *, tm, tn, out_dtype, name, prologue="cast", epilogue="store", rows_per_batch=None,
        gain=None, shift=None, scale=None, bias=None, res=None, gate=None,
        hgain=None, cos=None, sin=None, n_rope_tiles=0):
    t, k = x.shape
    n = w.shape[1]
    assert t % tm == 0 and n % tn == 0, (t, tm, n, tn)
    rows_per_batch = rows_per_batch or t
    assert rows_per_batch % tm == 0
    tiles_per_batch = rows_per_batch // tm
    rope = cos is not None
    once = dict(pipeline_mode=pl.Buffered(1))

    def batch_of(i):
        return i // tiles_per_batch

    args = [x]
    specs = [pl.BlockSpec((tm, k), lambda i, j: (i, 0))]
    if prologue == "norm_mod":
        nb = shift.shape[0]
        bsel = (lambda i: batch_of(i)) if nb > 1 else (lambda i: 0)
        args += [gain.reshape(1, k), shift, scale]
        specs += [pl.BlockSpec((1, k), lambda i, j: (0, 0)),
                  pl.BlockSpec((1, 1, k), lambda i, j: (bsel(i), 0, 0)),
                  pl.BlockSpec((1, 1, k), lambda i, j: (bsel(i), 0, 0))]
    args.append(w)
    specs.append(pl.BlockSpec((k, tn), lambda i, j: (0, j), **(once if tn == n else {})))
    if epilogue == "bias":
        args.append(bias.reshape(1, n))
        specs.append(pl.BlockSpec((1, tn), lambda i, j: (0, j)))
    elif epilogue == "resid":
        nb = gate.shape[0]
        gsel = (lambda i: batch_of(i)) if nb > 1 else (lambda i: 0)
        args += [res, gate]
        specs += [pl.BlockSpec((tm, tn), lambda i, j: (i, j)),
                  pl.BlockSpec((1, 1, tn), lambda i, j: (gsel(i), 0, j))]
    elif epilogue == "normrope":
        ones, perm = _head_group_matrices()
        const = pl.BlockSpec(ones.shape, lambda i, j: (0, 0))
        args += [hgain, ones]
        specs += [pl.BlockSpec((1, 1, tn), lambda i, j: (j, 0, 0)), const]
        if rope:
            args += [perm, cos, sin]
            specs += [const] + [pl.BlockSpec((tm, tn), lambda i, j: (i % tiles_per_batch, 0))] * 2
    scratch = []
    if prologue != "cast":
        scratch.append(pltpu.VMEM((tm, k), BF16))
    if epilogue == "normrope":
        scratch.append(pltpu.VMEM((tm, tn), F32))
    kern = functools.partial(_mm_kernel, prologue=prologue, epilogue=epilogue,
                             n_rope_tiles=n_rope_tiles, rope=rope)
    return pl.pallas_call(
        kern,
        grid=(t // tm, n // tn),
        in_specs=specs,
        out_specs=pl.BlockSpec((tm, tn), lambda i, j: (i, j)),
        out_shape=jax.ShapeDtypeStruct((t, n), out_dtype),
        scratch_shapes=scratch,
        compiler_params=_params("parallel", "arbitrary"),
        name=name,
    )(*args)


FFN_ROW_SPLIT = 1
FFN_HID_SPLIT = 2


def _ffn_kernel(x_ref, gain_ref, shift_ref, scale_ref, gate_ref, wg_ref, wu_ref, wo_ref, o_ref, xn_ref):
    j = pl.program_id(1)

    @pl.when(j == 0)
    def _():
        _norm_mod_rows(x_ref, xn_ref, gain_ref[...], shift_ref[0], scale_ref[0])
        o_ref[...] = jnp.zeros_like(o_ref)

    hr = o_ref.shape[0] // FFN_ROW_SPLIT
    hc = wg_ref.shape[1] // FFN_HID_SPLIT
    for h in range(FFN_ROW_SPLIT):
        rows = slice(h * hr, (h + 1) * hr)
        xn = xn_ref[rows, :]
        cols = [slice(c * hc, (c + 1) * hc) for c in range(FFN_HID_SPLIT)]
        gu = [(jnp.dot(xn, wg_ref[:, cs], preferred_element_type=F32),
               jnp.dot(xn, wu_ref[:, cs], preferred_element_type=F32)) for cs in cols]
        acc = None
        for cs, (g, u) in zip(cols, gu):
            a = (jax.nn.silu(g) * u).astype(BF16)
            part = jnp.dot(a, wo_ref[cs, :], preferred_element_type=F32)
            acc = part if acc is None else acc + part
        o_ref[rows, :] += acc

    @pl.when(j == pl.num_programs(1) - 1)
    def _():
        o_ref[...] = x_ref[...] + gate_ref[0] * o_ref[...]


def _ffn(x, gain, shift, scale, gate, w_in, w_out, *, tm, th, name, rows_per_batch=None):
    t, d = x.shape
    hidden = w_out.shape[0]
    assert t % tm == 0 and hidden % th == 0
    nh = hidden // th
    rows_per_batch = rows_per_batch or t
    tiles_per_batch = rows_per_batch // tm
    nb = shift.shape[0]
    bsel = (lambda i: i // tiles_per_batch) if nb > 1 else (lambda i: 0)
    vec = pl.BlockSpec((1, 1, d), lambda i, j: (bsel(i), 0, 0))
    return pl.pallas_call(
        _ffn_kernel,
        grid=(t // tm, nh),
        in_specs=[pl.BlockSpec((tm, d), lambda i, j: (i, 0), pipeline_mode=pl.Buffered(1)),
                  pl.BlockSpec((1, d), lambda i, j: (0, 0)),
                  vec, vec, vec,
                  pl.BlockSpec((d, th), lambda i, j: (0, j)),
                  pl.BlockSpec((d, th), lambda i, j: (0, nh + j)),
                  pl.BlockSpec((th, d), lambda i, j: (j, 0))],
        out_specs=pl.BlockSpec((tm, d), lambda i, j: (i, 0)),
        out_shape=jax.ShapeDtypeStruct((t, d), F32),
        scratch_shapes=[pltpu.VMEM((tm, d), BF16)],
        compiler_params=_params("parallel", "arbitrary"),
        name=name,
    )(x, gain.reshape(1, d), shift, scale, gate, w_in, w_in, w_out)


def _fill_kv(k_s, v_s, kv_refs):
    off = 0
    for k_ref, v_ref in kv_refs:
        n = k_ref.shape[0]
        k_s[off:off + n, :] = k_ref[...]
        v_s[off:off + n, :] = v_ref[...]
        off += n


ATTN_SUBTILE = 256


def _gqa_kernel(*refs, n_kv_src, n_group):
    q_ref = refs[0]
    kv_refs = [(refs[1 + 2 * s], refs[2 + 2 * s]) for s in range(n_kv_src)]
    o_ref, k_s, v_s = refs[1 + 2 * n_kv_src:]

    @pl.when(pl.program_id(2) == 0)
    def _():
        _fill_kv(k_s, v_s, kv_refs)

    k = k_s[...]
    v = v_s[...]
    sub = min(ATTN_SUBTILE, q_ref.shape[0])
    chains = [(slice(r0, r0 + sub), slice(g * HEAD_DIM, (g + 1) * HEAD_DIM))
              for r0 in range(0, q_ref.shape[0], sub) for g in range(n_group)]

    def scores(c):
        return lax.dot_general(q_ref[chains[c]], k, NT_DIMS, preferred_element_type=F32)

    s = scores(0)
    for c in range(len(chains)):
        s_next = scores(c + 1) if c + 1 < len(chains) else None
        p = jnp.exp2(s - jnp.max(s, axis=-1, keepdims=True))
        l = jnp.sum(p, axis=-1, keepdims=True)
        o = jnp.dot(p.astype(BF16), v, preferred_element_type=F32)
        o_ref[chains[c]] = (o / l).astype(o_ref.dtype)
        s = s_next


def _gqa_attention(q_src, kv_srcs, *, batch, n_q_heads, n_kv_heads, tq, q_rows, name):
    group = n_q_heads // n_kv_heads
    gw = group * HEAD_DIM
    nq = q_rows // tq
    k_blk0 = n_q_heads
    v_blk0 = n_q_heads + n_kv_heads
    specs = [pl.BlockSpec((tq, gw), lambda b, h, i: (b * nq + i, h))]
    args = [q_src]
    total = 0
    for src in kv_srcs:
        rows = src.shape[0] // batch
        total += rows
        specs += [pl.BlockSpec((rows, HEAD_DIM), lambda b, h, i: (b, k_blk0 + h)),
                  pl.BlockSpec((rows, HEAD_DIM), lambda b, h, i: (b, v_blk0 + h))]
        args += [src, src]
    kern = functools.partial(_gqa_kernel, n_kv_src=len(kv_srcs), n_group=group)
    return pl.pallas_call(
        kern,
        grid=(batch, n_kv_heads, nq),
        in_specs=specs,
        out_specs=pl.BlockSpec((tq, gw), lambda b, h, i: (b * nq + i, h)),
        out_shape=jax.ShapeDtypeStruct((batch * q_rows, n_q_heads * HEAD_DIM), BF16),
        scratch_shapes=[pltpu.VMEM((total, HEAD_DIM), BF16), pltpu.VMEM((total, HEAD_DIM), BF16)],
        compiler_params=_params("parallel", "parallel", "arbitrary"),
        name=name,
    )(*args)


def _diff_kernel(*refs, n_kv_src, lam_init):
    q_ref = refs[0]
    kv_refs = [(refs[1 + 2 * s], refs[2 + 2 * s]) for s in range(n_kv_src)]
    lq1, lk1, lq2, lk2, gain_ref, o_ref, k_s, v_s = refs[1 + 2 * n_kv_src:]

    @pl.when(pl.program_id(2) == 0)
    def _():
        _fill_kv(k_s, v_s, kv_refs)

    lam = (jnp.exp(jnp.sum(lq1[...] * lk1[...], axis=-1, keepdims=True))
           - jnp.exp(jnp.sum(lq2[...] * lk2[...], axis=-1, keepdims=True)) + lam_init)
    dh = HEAD_DIM
    sub = min(ATTN_SUBTILE, q_ref.shape[0])
    chains = [slice(r0, r0 + sub) for r0 in range(0, q_ref.shape[0], sub)]

    def scores(c):
        return [lax.dot_general(q_ref[chains[c], m * dh:(m + 1) * dh], k_s[:, m * dh:(m + 1) * dh], NT_DIMS,
                                preferred_element_type=F32) for m in range(2)]

    s = scores(0)
    for c in range(len(chains)):
        s_next = scores(c + 1) if c + 1 < len(chains) else None
        p = [jnp.exp2(sm - jnp.max(sm, axis=-1, keepdims=True)) for sm in s]
        l0, l1 = [jnp.sum(pm, axis=-1, keepdims=True) for pm in p]
        w = p[0] * (1.0 / l0) - p[1] * (lam / l1)
        o = jnp.dot(w.astype(BF16), v_s[...], preferred_element_type=F32)
        o_ref[chains[c], :] = ((_rms(o) * gain_ref[...]) * (1.0 - lam_init)).astype(o_ref.dtype)
        s = s_next


def _diff_attention(q_src, kv_srcs, lams, out_gain, *, batch, n_heads, tq, q_rows, lam_init, name):
    hw = 2 * HEAD_DIM
    nq = q_rows // tq
    specs = [pl.BlockSpec((tq, hw), lambda b, h, i: (b * nq + i, h))]
    args = [q_src]
    total = 0
    for src in kv_srcs:
        rows = src.shape[0] // batch
        total += rows
        specs += [pl.BlockSpec((rows, hw), lambda b, h, i: (b, n_heads + h)),
                  pl.BlockSpec((rows, hw), lambda b, h, i: (b, 2 * n_heads + h))]
        args += [src, src]
    small = pl.BlockSpec((1, HEAD_DIM), lambda b, h, i: (0, 0))
    specs += [small] * 4 + [pl.BlockSpec((1, hw), lambda b, h, i: (0, 0))]
    args += [v.reshape(1, HEAD_DIM) for v in lams] + [out_gain.reshape(1, hw)]
    kern = functools.partial(_diff_kernel, n_kv_src=len(kv_srcs), lam_init=lam_init)
    return pl.pallas_call(
        kern,
        grid=(batch, n_heads, nq),
        in_specs=specs,
        out_specs=pl.BlockSpec((tq, hw), lambda b, h, i: (b * nq + i, h)),
        out_shape=jax.ShapeDtypeStruct((batch * q_rows, n_heads * hw), BF16),
        scratch_shapes=[pltpu.VMEM((total, hw), BF16), pltpu.VMEM((total, hw), BF16)],
        compiler_params=_params("parallel", "parallel", "arbitrary"),
        name=name,
    )(*args)


def _gla_blocks(blocks):
    r, dk = blocks[0][0].shape
    c = GLA_CHUNK
    nc = r // c
    n = range(len(blocks))
    qs_, ks_, vs_, zs_, wgs, bgs, st_refs, revs = zip(*blocks)
    row = lax.broadcasted_iota(jnp.int32, (r, r), 0)
    col = lax.broadcasted_iota(jnp.int32, (r, r), 1)
    same = row // c == col // c
    tri_f = [jnp.where(same, jnp.where((col >= row) if revs[i] else (col <= row), 1.0, 0.0), 0.0) for i in n]
    tri = [t.astype(BF16) for t in tri_f]
    mid = [c // 2 if revs[i] else c // 2 - 1 for i in n]
    last = [0 if revs[i] else c - 1 for i in n]

    g = [jax.nn.log_sigmoid(jnp.dot(zs_[i], wgs[i], preferred_element_type=F32) + bgs[i]) / GLA_TAU for i in n]
    g_hi = [g[i].astype(BF16) for i in n]
    g_lo = [(g[i] - g_hi[i].astype(F32)).astype(BF16) for i in n]
    cum = [jnp.dot(tri[i], g_hi[i], preferred_element_type=F32) + jnp.dot(tri[i], g_lo[i], preferred_element_type=F32)
           for i in n]

    def chunk_row(x, idx):
        return jnp.concatenate(
            [jnp.broadcast_to(x[ci * c + idx:ci * c + idx + 1, :], (c, dk)) for ci in range(nc)], axis=0)

    cum_mid = [chunk_row(cum[i], mid[i]) for i in n]
    cum_last = [chunk_row(cum[i], last[i]) for i in n]
    qs = [(qs_[i] * jnp.exp(cum[i] - cum_mid[i])).astype(BF16) for i in n]
    ks = [(ks_[i] * jnp.exp(cum_mid[i] - cum[i])).astype(BF16) for i in n]
    a = [lax.dot_general(qs[i], ks[i], NT_DIMS, preferred_element_type=F32) for i in n]
    a = [jnp.where(tri_f[i] > 0.5, a[i], 0.0).astype(BF16) for i in n]
    o_intra = [jnp.dot(a[i], vs_[i], preferred_element_type=F32) for i in n]
    q_inter = [(qs_[i] * jnp.exp(cum[i])).astype(BF16) for i in n]
    k_carry = [(ks_[i] * jnp.exp(cum_last[i] - cum[i])).astype(BF16) for i in n]
    st = [st_refs[i][...] for i in n]
    o_inter = [[None] * nc for _ in n]
    for step in range(nc):
        for i in n:
            ci = nc - 1 - step if revs[i] else step
            rows = slice(ci * c, (ci + 1) * c)
            o_inter[i][ci] = lax.dot_general(q_inter[i][rows], st[i].astype(BF16), NT_DIMS,
                                             preferred_element_type=F32)
            decay = jnp.exp(cum[i][ci * c + last[i]:ci * c + last[i] + 1, :])
            st[i] = st[i] * decay + lax.dot_general(vs_[i][rows], k_carry[i][rows], TN_DIMS,
                                                    preferred_element_type=F32)
    for i in n:
        st_refs[i][...] = st[i]
    return [o_intra[i] + jnp.concatenate(o_inter[i], axis=0) for i in n]


def _gla_kernel(qc_ref, kc_ref, vc_ref, rc_ref, zc_ref, ql_ref, kl_ref, vl_ref, rl_ref, zl_ref,
                wgf_ref, bgf_ref, wgb_ref, bgb_ref, gain_ref, oc_ref, ol_ref, sf_ref, sb_ref, of_ref, ob_ref):
    blk = GLA_BLOCK
    n_ctx = qc_ref.shape[0] // blk
    n_lat = ql_ref.shape[0] // blk
    q_scale = qc_ref.shape[1] ** -0.5
    ctx_refs = (qc_ref, kc_ref, vc_ref, zc_ref)
    lat_refs = (ql_ref, kl_ref, vl_ref, zl_ref)

    def rows_of(bi):
        return pl.ds(pl.multiple_of(bi * blk, blk), blk)

    def block(refs, bi, reverse):
        q_ref, k_ref, v_ref, z_ref = refs
        rows = rows_of(bi)
        wg_ref, bg_ref, st_ref = (wgb_ref, bgb_ref, sb_ref) if reverse else (wgf_ref, bgf_ref, sf_ref)
        return (q_ref[rows, :].astype(F32) * q_scale, k_ref[rows, :].astype(F32), v_ref[rows, :], z_ref[rows, :],
                wg_ref[0], bg_ref[0], st_ref, reverse)

    def both(refs, base, n):
        def body(t, carry):
            o_f, o_b = _gla_blocks([block(refs, t, False), block(refs, n - 1 - t, True)])
            of_ref[rows_of(base + t), :] = o_f
            ob_ref[rows_of(base + n - 1 - t), :] = o_b
            return carry
        return body

    def finish(r_ref, out_ref, base):
        def body(bi, carry):
            rows = rows_of(bi)
            o = of_ref[rows_of(base + bi), :] + ob_ref[rows_of(base + bi), :]
            y = (_rms(o) * gain_ref[...]) * jax.nn.silu(r_ref[rows, :].astype(F32))
            out_ref[rows, :] = y.astype(out_ref.dtype)
            return carry
        return body

    sf_ref[...] = jnp.zeros_like(sf_ref)
    sb_ref[...] = jnp.zeros_like(sb_ref)
    lax.fori_loop(0, n_ctx, both(ctx_refs, 0, n_ctx), 0)
    lax.fori_loop(0, n_lat, both(lat_refs, n_ctx, n_lat), 0)
    lax.fori_loop(0, n_ctx, finish(rc_ref, oc_ref, 0), 0)
    lax.fori_loop(0, n_lat, finish(rl_ref, ol_ref, n_ctx), 0)


def _gla(p_ctx, p_lat, wgf, bgf, wgb, bgb, out_gain, *, batch, dk, dv):
    h = GLA_HEADS
    rows_c = p_ctx.shape[0] // batch
    rows_l = p_lat.shape[0] // batch
    assert rows_c % GLA_BLOCK == 0 and rows_l % GLA_BLOCK == 0
    zblk = (2 * h * dk + 2 * h * dv) // LANES
    k0 = h
    v0 = (2 * h * dk) // dv
    r0 = v0 + h

    def stream(rows):
        return [pl.BlockSpec((rows, dk), lambda b, hh: (b, hh)),
                pl.BlockSpec((rows, dk), lambda b, hh: (b, k0 + hh)),
                pl.BlockSpec((rows, dv), lambda b, hh: (b, v0 + hh)),
                pl.BlockSpec((rows, dv), lambda b, hh: (b, r0 + hh)),
                pl.BlockSpec((rows, LANES), lambda b, hh: (b, zblk))]

    wspec = pl.BlockSpec((1, LANES, dk), lambda b, hh: (hh, 0, 0))
    bspec = pl.BlockSpec((1, 1, dk), lambda b, hh: (hh, 0, 0))
    return pl.pallas_call(
        _gla_kernel,
        grid=(batch, h),
        in_specs=stream(rows_c) + stream(rows_l) + [wspec, bspec, wspec, bspec,
                                                    pl.BlockSpec((1, dv), lambda b, hh: (0, 0))],
        out_specs=[pl.BlockSpec((rows_c, dv), lambda b, hh: (b, hh)),
                   pl.BlockSpec((rows_l, dv), lambda b, hh: (b, hh))],
        out_shape=[jax.ShapeDtypeStruct((batch * rows_c, h * dv), BF16),
                   jax.ShapeDtypeStruct((batch * rows_l, h * dv), BF16)],
        scratch_shapes=[pltpu.VMEM((dv, dk), F32), pltpu.VMEM((dv, dk), F32),
                        pltpu.VMEM((rows_c + rows_l, dv), F32), pltpu.VMEM((rows_c + rows_l, dv), F32)],
        compiler_params=_params("parallel", "parallel"),
        name="gla",
    )(*([p_ctx] * 5 + [p_lat] * 5 + [wgf, bgf, wgb, bgb, out_gain.reshape(1, dv)]))


def _dft_tables(n):
    idx = jnp.arange(n, dtype=jnp.int32)
    ang = ((idx[:, None] * idx[None, :]) % n).astype(F32) * (2.0 * math.pi / n)
    return jnp.cos(ang).astype(BF16), jnp.sin(ang).astype(BF16)


def _fnet_chan_kernel(x_ref, gain_ref, shift_ref, scale_ref, cc_ref, sc_ref, p_ref, q_ref, xn_ref):
    j = pl.program_id(1)
    gd = cc_ref.shape[0]

    @pl.when(j == 0)
    def _():
        _norm_mod_rows(x_ref, xn_ref, gain_ref[...], shift_ref[0], scale_ref[0])

    hg = xn_ref[:, pl.ds(pl.multiple_of(j * gd, gd), gd)]
    p_ref[...] = jnp.dot(hg, cc_ref[...], preferred_element_type=F32).astype(p_ref.dtype)
    q_ref[...] = jnp.dot(hg, sc_ref[...], preferred_element_type=F32).astype(q_ref.dtype)


def _fnet_seq_kernel(cs_ref, ss_ref, p_ref, q_ref, o_ref, *, inv_norm):
    acc = (jnp.dot(cs_ref[...], p_ref[...], preferred_element_type=F32)
           - jnp.dot(ss_ref[...], q_ref[...], preferred_element_type=F32))
    o_ref[...] = (acc * inv_norm).astype(o_ref.dtype)


def _fnet(x, gain, shift, scale, *, batch, tm):
    t, d = x.shape
    s = t // batch
    gd = d // FNET_GROUPS
    tiles_per_batch = s // tm
    cc, sc = _dft_tables(gd)
    cs, ss = _dft_tables(s)
    vec = pl.BlockSpec((1, 1, d), lambda i, j: (i // tiles_per_batch, 0, 0))
    tab = pl.BlockSpec((gd, gd), lambda i, j: (0, 0))
    blk = pl.BlockSpec((tm, gd), lambda i, j: (i, j))
    p, q = pl.pallas_call(
        _fnet_chan_kernel,
        grid=(t // tm, FNET_GROUPS),
        in_specs=[pl.BlockSpec((tm, d), lambda i, j: (i, 0)),
                  pl.BlockSpec((1, d), lambda i, j: (0, 0)), vec, vec, tab, tab],
        out_specs=[blk, blk],
        out_shape=[jax.ShapeDtypeStruct((t, d), BF16)] * 2,
        scratch_shapes=[pltpu.VMEM((tm, d), BF16)],
        compiler_params=_params("parallel", "arbitrary"),
        name="fnet_chan",
    )(x, gain.reshape(1, d), shift, scale, cc, sc)
    rows = pl.BlockSpec((tm, s), lambda b, j, i: (i, 0))
    cols = pl.BlockSpec((s, gd), lambda b, j, i: (b, j))
    return pl.pallas_call(
        functools.partial(_fnet_seq_kernel, inv_norm=float((s * gd) ** -0.5)),
        grid=(batch, d // gd, tiles_per_batch),
        in_specs=[rows, rows, cols, cols],
        out_specs=pl.BlockSpec((tm, gd), lambda b, j, i: (b * tiles_per_batch + i, j)),
        out_shape=jax.ShapeDtypeStruct((t, d), BF16),
        compiler_params=_params("parallel", "parallel", "arbitrary"),
        name="fnet_seq",
    )(cs, ss, p, q)


def _rope_tables(n_tokens, width):
    t = jnp.arange(n_tokens)
    row = (t // GRID_W).astype(F32)
    col = (t % GRID_W).astype(F32)
    half = HEAD_DIM // 2
    inv_freq = ROPE_THETA ** (-jnp.arange(0, half, 2, dtype=F32) / half)
    ang_r = row[:, None] * inv_freq[None, :]
    ang_c = col[:, None] * inv_freq[None, :]
    ang = jnp.concatenate([ang_r, ang_r, ang_c, ang_c], axis=-1)
    sign = jnp.concatenate([-jnp.ones((half // 2,), F32), jnp.ones((half // 2,), F32)] * 2)
    reps = width // HEAD_DIM
    return jnp.tile(jnp.cos(ang), (1, reps)), jnp.tile(jnp.sin(ang) * sign, (1, reps))


def _head_gains(q_gain, k_gain, n_q_tiles, n_k_tiles, n_v_tiles, tn):
    reps = tn // HEAD_DIM
    qg = jnp.tile(q_gain.astype(F32) * (HEAD_DIM ** -0.5 * math.log2(math.e)), reps)
    kg = jnp.tile(k_gain.astype(F32), reps)
    rows = [qg] * n_q_tiles + [kg] * n_k_tiles + [jnp.ones((tn,), F32)] * n_v_tiles
    return jnp.stack(rows)[:, None, :]


def _gate_weights(wg, bg, lane0, dk):
    r = wg.shape[0]
    w = wg.reshape(r, GLA_HEADS, dk).transpose(1, 0, 2)
    w = jnp.pad(w, ((0, 0), (lane0, LANES - lane0 - r), (0, 0))).astype(BF16)
    return w, bg.reshape(GLA_HEADS, 1, dk).astype(F32)


def kernel(x, c, ctx, c_ctx, l0_mod_w, l0_mod_b, l0_norm1, l0_gla_w_in, l0_gla_wg_f, l0_gla_bg_f, l0_gla_wg_b, l0_gla_bg_b, l0_gla_out_norm, l0_gla_w_out, l0_norm2, l0_ffn_w_in, l0_ffn_w_out, l1_mod_w, l1_mod_b, l1_norm1, l1_gqa_w_in, l1_gqa_q_norm, l1_gqa_k_norm, l1_gqa_w_out, l1_norm2, l1_ffn_w_in, l1_ffn_w_out, l2_mod_w, l2_mod_b, l2_norm1, l2_diff_w_in, l2_diff_q_norm, l2_diff_k_norm, l2_diff_lq1, l2_diff_lk1, l2_diff_lq2, l2_diff_lk2, l2_diff_out_norm, l2_diff_w_out, l2_norm2, l2_ffn_w_in, l2_ffn_w_out, l3_mod_w, l3_mod_b, l3_norm1, l3_fnet_w_out, l3_norm2, l3_ffn_w_in, l3_ffn_w_out):
    b, s, d = x.shape
    n_ctx = ctx.shape[1]
    xl = x.reshape(b * s, d)
    xc = ctx.reshape(b * n_ctx, d)
    tm = math.gcd(s, 1024)
    tm_c = math.gcd(b * n_ctx, 1024)
    tm_r = math.gcd(s, 512)
    tm_rc = math.gcd(b * n_ctx, 512)
    th = math.gcd(l0_ffn_w_out.shape[0], 512)
    tq_gqa = math.gcd(s, 2 * ATTN_SUBTILE)
    tq_diff = math.gcd(s, 4 * ATTN_SUBTILE)

    n_cond = -(-(b + 1) // BF16_ROWS) * BF16_ROWS
    cond = jnp.concatenate([c, c_ctx[None, :], jnp.zeros((n_cond - b - 1, d), F32)], axis=0)

    def modulation(mod_w, mod_b):
        m = _mm(cond, mod_w, tm=n_cond, tn=512, out_dtype=F32, prologue="silu", epilogue="bias", bias=mod_b,
                name="modulation")
        lat = [m[:b, k * d:(k + 1) * d].reshape(b, 1, d) for k in range(6)]
        cx = [m[b:b + 1, k * d:(k + 1) * d].reshape(1, 1, d) for k in range(6)]
        return lat, cx

    def tiling(rows, resident=False):
        if rows == s:
            return dict(tm=tm_r if resident else tm, rows_per_batch=s)
        return dict(tm=tm_rc if resident else tm_c, rows_per_batch=None)

    def proj(xs, w, n1, sh, sc, rows, tn, name, **kw):
        return _mm(xs, w, tn=tn, out_dtype=BF16, prologue="norm_mod", gain=n1, shift=sh, scale=sc, name=name,
                   **tiling(rows), **kw)

    def out_resid(y, w, xs, gate, rows):
        return _mm(y, w.astype(BF16), tn=w.shape[1], out_dtype=F32, epilogue="resid", res=xs, gate=gate,
                   name="out_resid", **tiling(rows, resident=True))

    def ffn(xs, n2, sh, sc, gate, w_in, w_out, rows):
        return _ffn(xs, n2, sh, sc, gate, w_in.astype(BF16), w_out.astype(BF16), th=th, name="ffn", **tiling(rows))

    (sh1, sc1, g1, sh2, sc2, g2), (csh1, csc1, cg1, csh2, csc2, cg2) = modulation(l0_mod_w, l0_mod_b)
    dk = l0_gla_wg_f.shape[1] // GLA_HEADS
    dv = d // GLA_HEADS
    n_in = l0_gla_w_in.shape[1]
    tn0 = 1280
    n_pad = -(-(n_in - 2 * GLA_RANK + LANES) // tn0) * tn0
    w0 = jnp.pad(l0_gla_w_in, ((0, 0), (0, n_pad - n_in))).astype(BF16)
    pl0 = proj(xl, w0, l0_norm1, sh1, sc1, s, tn0, "gla_proj")
    pc0 = proj(xc, w0, l0_norm1, csh1, csc1, n_ctx, tn0, "gla_proj_ctx")
    wgf, bgf = _gate_weights(l0_gla_wg_f, l0_gla_bg_f, 0, dk)
    wgb, bgb = _gate_weights(l0_gla_wg_b, l0_gla_bg_b, GLA_RANK, dk)
    yc, yl = _gla(pc0, pl0, wgf, bgf, wgb, bgb, l0_gla_out_norm, batch=b, dk=dk, dv=dv)
    xl = out_resid(yl, l0_gla_w_out, xl, g1, s)
    xc = out_resid(yc, l0_gla_w_out, xc, cg1, n_ctx)
    xl = ffn(xl, l0_norm2, sh2, sc2, g2, l0_ffn_w_in, l0_ffn_w_out, s)
    xc = ffn(xc, l0_norm2, csh2, csc2, cg2, l0_ffn_w_in, l0_ffn_w_out, n_ctx)

    (sh1, sc1, g1, sh2, sc2, g2), (csh1, csc1, cg1, csh2, csc2, cg2) = modulation(l1_mod_w, l1_mod_b)
    tn1 = 512
    n_heads = d // HEAD_DIM
    cos, sin = _rope_tables(s, tn1)
    nq_t = d // tn1
    nk_t = GQA_KV_HEADS * HEAD_DIM // tn1
    hg1 = _head_gains(l1_gqa_q_norm, l1_gqa_k_norm, nq_t, nk_t, nk_t, tn1)
    w1 = l1_gqa_w_in.astype(BF16)
    pl1 = proj(xl, w1, l1_norm1, sh1, sc1, s, tn1, "gqa_proj", epilogue="normrope", hgain=hg1, cos=cos, sin=sin,
               n_rope_tiles=nq_t + nk_t)
    pc1 = proj(xc, w1, l1_norm1, csh1, csc1, n_ctx, tn1, "gqa_proj_ctx", epilogue="normrope", hgain=hg1,
               n_rope_tiles=nq_t + nk_t)
    yl = _gqa_attention(pl1, [pl1, pc1], batch=b, n_q_heads=n_heads, n_kv_heads=GQA_KV_HEADS, tq=tq_gqa, q_rows=s,
                        name="gqa_attn")
    yc = _gqa_attention(pc1, [pc1], batch=b, n_q_heads=n_heads, n_kv_heads=GQA_KV_HEADS, tq=n_ctx, q_rows=n_ctx,
                        name="gqa_attn_ctx")
    xl = out_resid(yl, l1_gqa_w_out, xl, g1, s)
    xc = out_resid(yc, l1_gqa_w_out, xc, cg1, n_ctx)
    xl = ffn(xl, l1_norm2, sh2, sc2, g2, l1_ffn_w_in, l1_ffn_w_out, s)
    xc = ffn(xc, l1_norm2, csh2, csc2, cg2, l1_ffn_w_in, l1_ffn_w_out, n_ctx)

    (sh1, sc1, g1, sh2, sc2, g2), (csh1, csc1, _, _, _, _) = modulation(l2_mod_w, l2_mod_b)
    lam_init = 0.8 - 0.6 * math.exp(-0.3 * 2)
    hg2 = _head_gains(l2_diff_q_norm, l2_diff_k_norm, nq_t, nq_t, nq_t, tn1)
    w2 = l2_diff_w_in.astype(BF16)
    pl2 = proj(xl, w2, l2_norm1, sh1, sc1, s, tn1, "diff_proj", epilogue="normrope", hgain=hg2, cos=cos, sin=sin,
               n_rope_tiles=2 * nq_t)
    pc2 = proj(xc, w2, l2_norm1, csh1, csc1, n_ctx, tn1, "diff_proj_ctx", epilogue="normrope", hgain=hg2,
               n_rope_tiles=2 * nq_t)
    yl = _diff_attention(pl2, [pl2, pc2], (l2_diff_lq1, l2_diff_lk1, l2_diff_lq2, l2_diff_lk2),
                         l2_diff_out_norm, batch=b, n_heads=n_heads // 2, tq=tq_diff, q_rows=s, lam_init=lam_init,
                         name="diff_attn")
    xl = out_resid(yl, l2_diff_w_out, xl, g1, s)
    xl = ffn(xl, l2_norm2, sh2, sc2, g2, l2_ffn_w_in, l2_ffn_w_out, s)

    (sh1, sc1, g1, sh2, sc2, g2), _ = modulation(l3_mod_w, l3_mod_b)
    yl = _fnet(xl, l3_norm1, sh1, sc1, batch=b, tm=tm_r)
    xl = out_resid(yl, l3_fnet_w_out, xl, g1, s)
    xl = ffn(xl, l3_norm2, sh2, sc2, g2, l3_ffn_w_in, l3_ffn_w_out, s)
    return xl.reshape(b, s, d)
```

```python
import functools
import math

import jax
import jax.numpy as jnp
from jax import lax
from jax.experimental import pallas as pl
from jax.experimental.pallas import tpu as pltpu

F32 = jnp.float32
BF16 = jnp.bfloat16

NORM_EPS = 1e-6
ROPE_THETA = 10000.0
GRID_W = 64
HEAD_DIM = 128
GQA_KV_HEADS = 4
GLA_HEADS = 4
GLA_RANK = 16
GLA_TAU = 16.0
GLA_CHUNK = 64
GLA_BLOCK = 4 * GLA_CHUNK
FNET_GROUPS = 4

LANES = 128
BF16_ROWS = 16
STRIP_UNROLL = 8
VMEM_LIMIT = 56 * 1024 * 1024

NT_DIMS = (((1,), (1,)), ((), ()))
TN_DIMS = (((0,), (0,)), ((), ()))


def _params(*sem):
    return pltpu.CompilerParams(dimension_semantics=sem, vmem_limit_bytes=VMEM_LIMIT)


def _rms(x, eps=NORM_EPS):
    return x * lax.rsqrt(jnp.mean(x * x, axis=-1, keepdims=True) + eps)


def _norm_mod_rows(x_ref, xn_ref, gain, shift, scale):
    mult = gain * (1.0 + scale)

    def strip(r, carry):
        rows = pl.ds(pl.multiple_of(r * BF16_ROWS, BF16_ROWS), BF16_ROWS)
        xn_ref[rows, :] = (_rms(x_ref[rows, :]) * mult + shift).astype(BF16)
        return carry

    lax.fori_loop(0, x_ref.shape[0] // BF16_ROWS, strip, 0, unroll=STRIP_UNROLL)


ROPE_STRIP = 256
MXU_WIDTH = 256


def _head_group_matrices():
    src = lax.broadcasted_iota(jnp.int32, (MXU_WIDTH, MXU_WIDTH), 0)
    dst = lax.broadcasted_iota(jnp.int32, (MXU_WIDTH, MXU_WIDTH), 1)
    quarter = HEAD_DIM // 4
    partner = jnp.where((dst // quarter) % 2 == 0, dst + quarter, dst - quarter)
    mean = jnp.where(src // HEAD_DIM == dst // HEAD_DIM, 1.0 / HEAD_DIM, 0.0).astype(BF16)
    return mean, (src == partner).astype(BF16)


def _mm_kernel(*refs, prologue, epilogue):
    it = iter(refs)
    x_ref = next(it)
    if prologue == "norm_mod":
        gain_ref, shift_ref, scale_ref = next(it), next(it), next(it)
    w_ref = next(it)
    if epilogue == "bias":
        b_ref = next(it)
    elif epilogue == "resid":
        res_ref, gate_ref = next(it), next(it)
    o_ref = next(it)
    if prologue != "cast":
        xn_ref = next(it)
    j = pl.program_id(1)

    if prologue == "cast":
        a = x_ref[...].astype(BF16)
    else:
        @pl.when(j == 0)
        def _():
            if prologue == "norm_mod":
                _norm_mod_rows(x_ref, xn_ref, gain_ref[...], shift_ref[0], scale_ref[0])
            else:
                xn_ref[...] = jax.nn.silu(x_ref[...]).astype(BF16)

        a = xn_ref[...]
    acc = jnp.dot(a, w_ref[...].astype(BF16), preferred_element_type=F32)
    if epilogue == "store":
        o_ref[...] = acc.astype(o_ref.dtype)
    elif epilogue == "bias":
        o_ref[...] = (acc + b_ref[...]).astype(o_ref.dtype)
    elif epilogue == "resid":
        o_ref[...] = (res_ref[...] + gate_ref[0] * acc).astype(o_ref.dtype)


def _mm(x, w, *, tm, tn, out_dtype, name, prologue="cast", epilogue="store", rows_per_batch=None,
        gain=None, shift=None, scale=None, bias=None, res=None, gate=None):
    t, k = x.shape
    n = w.shape[1]
    assert t % tm == 0 and n % tn == 0, (t, tm, n, tn)
    rows_per_batch = rows_per_batch or t
    assert rows_per_batch % tm == 0
    tiles_per_batch = rows_per_batch // tm
    once = dict(pipeline_mode=pl.Buffered(1))

    def batch_of(i):
        return i // tiles_per_batch

    args = [x]
    specs = [pl.BlockSpec((tm, k), lambda i, j: (i, 0))]
    if prologue == "norm_mod":
        nb = shift.shape[0]
        bsel = (lambda i: batch_of(i)) if nb > 1 else (lambda i: 0)
        args += [gain.reshape(1, k), shift, scale]
        specs += [pl.BlockSpec((1, k), lambda i, j: (0, 0)),
                  pl.BlockSpec((1, 1, k), lambda i, j: (bsel(i), 0, 0)),
                  pl.BlockSpec((1, 1, k), lambda i, j: (bsel(i), 0, 0))]
    args.append(w)
    specs.append(pl.BlockSpec((k, tn), lambda i, j: (0, j), **(once if tn == n else {})))
    if epilogue == "bias":
        args.append(bias.reshape(1, n))
        specs.append(pl.BlockSpec((1, tn), lambda i, j: (0, j)))
    elif epilogue == "resid":
        nb = gate.shape[0]
        gsel = (lambda i: batch_of(i)) if nb > 1 else (lambda i: 0)
        args += [res, gate]
        specs += [pl.BlockSpec((tm, tn), lambda i, j: (i, j)),
                  pl.BlockSpec((1, 1, tn), lambda i, j: (gsel(i), 0, j))]
    scratch = []
    if prologue != "cast":
        scratch.append(pltpu.VMEM((tm, k), BF16))
    kern = functools.partial(_mm_kernel, prologue=prologue, epilogue=epilogue)
    return pl.pallas_call(
        kern,
        grid=(t // tm, n // tn),
        in_specs=specs,
        out_specs=pl.BlockSpec((tm, tn), lambda i, j: (i, j)),
        out_shape=jax.ShapeDtypeStruct((t, n), out_dtype),
        scratch_shapes=scratch,
        compiler_params=_params("parallel", "arbitrary"),
        name=name,
    )(*args)


def _qkv_kernel(*refs, normed_cols, rope):
    it = iter(refs)
    x_ref, gain_ref, shift_ref, scale_ref, w_ref, hg_ref, ones_ref = [next(it) for _ in range(7)]
    if rope:
        perm_ref, cos_ref, sin_ref = next(it), next(it), next(it)
    o_ref, xn_ref = next(it), next(it)
    j = pl.program_id(1)
    tm, tn = o_ref.shape
    gw = ones_ref.shape[0]
    n_sub = tn // gw
    full_tiles, rem = divmod(normed_cols, tn)

    @pl.when(j == 0)
    def _():
        _norm_mod_rows(x_ref, xn_ref, gain_ref[...], shift_ref[0], scale_ref[0])

    def norm_rope_store(acc, cols):
        for r0 in range(0, tm, ROPE_STRIP):
            rows = slice(r0, r0 + ROPE_STRIP)
            blk = acc[rows, :]
            ms = jnp.dot((blk * blk).astype(BF16), ones_ref[...], preferred_element_type=F32)
            y = blk * lax.rsqrt(ms + NORM_EPS) * hg_ref[:, cols]
            if rope:
                partner = jnp.dot(y.astype(BF16), perm_ref[...], preferred_element_type=F32)
                reps = gw // HEAD_DIM
                y = (y * jnp.concatenate([cos_ref[rows, :]] * reps, axis=-1)
                     + partner * jnp.concatenate([sin_ref[rows, :]] * reps, axis=-1))
            o_ref[rows, cols] = y.astype(o_ref.dtype)

    def tile(n_normed_subs):
        a = xn_ref[...]
        if n_normed_subs == 0:
            o_ref[...] = jnp.dot(a, w_ref[...], preferred_element_type=F32).astype(o_ref.dtype)
            return
        sub_cols = [slice(s * gw, (s + 1) * gw) for s in range(n_sub)]
        acc = jnp.dot(a, w_ref[:, sub_cols[0]], preferred_element_type=F32)
        for s in range(n_sub):
            nxt = jnp.dot(a, w_ref[:, sub_cols[s + 1]], preferred_element_type=F32) if s + 1 < n_sub else None
            if s < n_normed_subs:
                norm_rope_store(acc, sub_cols[s])
            else:
                o_ref[:, sub_cols[s]] = acc.astype(o_ref.dtype)
            acc = nxt

    if full_tiles:
        pl.when(j < full_tiles)(functools.partial(tile, n_sub))
    if rem:
        pl.when(j == full_tiles)(functools.partial(tile, rem // gw))
    pl.when(j >= full_tiles + (1 if rem else 0))(functools.partial(tile, 0))


def _qkv_proj(x, w, gain, shift, scale, col_gain, *, normed_cols, tm, tn, name, rows_per_batch=None,
              cos=None, sin=None):
    t, k = x.shape
    n = w.shape[1]
    assert t % tm == 0 and n % tn == 0 and normed_cols % MXU_WIDTH == 0
    rows_per_batch = rows_per_batch or t
    assert rows_per_batch % tm == 0
    tiles_per_batch = rows_per_batch // tm
    rope = cos is not None
    nb = shift.shape[0]
    bsel = (lambda i: i // tiles_per_batch) if nb > 1 else (lambda i: 0)
    ones, perm = _head_group_matrices()
    const = pl.BlockSpec(ones.shape, lambda i, j: (0, 0))
    vec = pl.BlockSpec((1, 1, k), lambda i, j: (bsel(i), 0, 0))
    args = [x, gain.reshape(1, k), shift, scale, w, col_gain, ones]
    specs = [pl.BlockSpec((tm, k), lambda i, j: (i, 0)),
             pl.BlockSpec((1, k), lambda i, j: (0, 0)), vec, vec,
             pl.BlockSpec((k, tn), lambda i, j: (0, j)),
             pl.BlockSpec((1, tn), lambda i, j: (0, j)), const]
    if rope:
        table = pl.BlockSpec((tm, HEAD_DIM), lambda i, j: (i % tiles_per_batch, 0))
        args += [perm, cos, sin]
        specs += [const, table, table]
    return pl.pallas_call(
        functools.partial(_qkv_kernel, normed_cols=normed_cols, rope=rope),
        grid=(t // tm, n // tn),
        in_specs=specs,
        out_specs=pl.BlockSpec((tm, tn), lambda i, j: (i, j)),
        out_shape=jax.ShapeDtypeStruct((t, n), BF16),
        scratch_shapes=[pltpu.VMEM((tm, k), BF16)],
        compiler_params=_params("parallel", "arbitrary"),
        name=name,
    )(*args)


FFN_ROW_SPLIT = 1
FFN_HID_SPLIT = 2


def _ffn_kernel(x_ref, gain_ref, shift_ref, scale_ref, gate_ref, wg_ref, wu_ref, wo_ref, o_ref, xn_ref):
    j = pl.program_id(1)

    @pl.when(j == 0)
    def _():
        _norm_mod_rows(x_ref, xn_ref, gain_ref[...], shift_ref[0], scale_ref[0])
        o_ref[...] = jnp.zeros_like(o_ref)

    hr = o_ref.shape[0] // FFN_ROW_SPLIT
    hc = wg_ref.shape[1] // FFN_HID_SPLIT
    for h in range(FFN_ROW_SPLIT):
        rows = slice(h * hr, (h + 1) * hr)
        xn = xn_ref[rows, :]
        cols = [slice(c * hc, (c + 1) * hc) for c in range(FFN_HID_SPLIT)]
        gu = [(jnp.dot(xn, wg_ref[:, cs], preferred_element_type=F32),
               jnp.dot(xn, wu_ref[:, cs], preferred_element_type=F32)) for cs in cols]
        acc = None
        for cs, (g, u) in zip(cols, gu):
            a = (jax.nn.silu(g) * u).astype(BF16)
            part = jnp.dot(a, wo_ref[cs, :], preferred_element_type=F32)
            acc = part if acc is None else acc + part
        o_ref[rows, :] += acc

    @pl.when(j == pl.num_programs(1) - 1)
    def _():
        o_ref[...] = x_ref[...] + gate_ref[0] * o_ref[...]


def _ffn(x, gain, shift, scale, gate, w_in, w_out, *, tm, th, name, rows_per_batch=None):
    t, d = x.shape
    hidden = w_out.shape[0]
    assert t % tm == 0 and hidden % th == 0
    nh = hidden // th
    rows_per_batch = rows_per_batch or t
    tiles_per_batch = rows_per_batch // tm
    nb = shift.shape[0]
    bsel = (lambda i: i // tiles_per_batch) if nb > 1 else (lambda i: 0)
    vec = pl.BlockSpec((1, 1, d), lambda i, j: (bsel(i), 0, 0))
    return pl.pallas_call(
        _ffn_kernel,
        grid=(t // tm, nh),
        in_specs=[pl.BlockSpec((tm, d), lambda i, j: (i, 0), pipeline_mode=pl.Buffered(1)),
                  pl.BlockSpec((1, d), lambda i, j: (0, 0)),
                  vec, vec, vec,
                  pl.BlockSpec((d, th), lambda i, j: (0, j)),
                  pl.BlockSpec((d, th), lambda i, j: (0, nh + j)),
                  pl.BlockSpec((th, d), lambda i, j: (j, 0))],
        out_specs=pl.BlockSpec((tm, d), lambda i, j: (i, 0)),
        out_shape=jax.ShapeDtypeStruct((t, d), F32),
        scratch_shapes=[pltpu.VMEM((tm, d), BF16)],
        compiler_params=_params("parallel", "arbitrary"),
        name=name,
    )(x, gain.reshape(1, d), shift, scale, gate, w_in, w_in, w_out)


def _fill_kv(k_s, v_s, kv_refs):
    off = 0
    for k_ref, v_ref in kv_refs:
        n = k_ref.shape[0]
        k_s[off:off + n, :] = k_ref[...]
        v_s[off:off + n, :] = v_ref[...]
        off += n


ATTN_SUBTILE = 256


def _gqa_kernel(*refs, n_kv_src, n_group):
    q_ref = refs[0]
    kv_refs = [(refs[1 + 2 * s], refs[2 + 2 * s]) for s in range(n_kv_src)]
    o_ref, k_s, v_s = refs[1 + 2 * n_kv_src:]

    @pl.when(pl.program_id(2) == 0)
    def _():
        _fill_kv(k_s, v_s, kv_refs)

    k = k_s[...]
    v = v_s[...]
    sub = min(ATTN_SUBTILE, q_ref.shape[0])
    chains = [(slice(r0, r0 + sub), slice(g * HEAD_DIM, (g + 1) * HEAD_DIM))
              for r0 in range(0, q_ref.shape[0], sub) for g in range(n_group)]

    def scores(c):
        return lax.dot_general(q_ref[chains[c]], k, NT_DIMS, preferred_element_type=F32)

    s = scores(0)
    for c in range(len(chains)):
        s_next = scores(c + 1) if c + 1 < len(chains) else None
        p = jnp.exp2(s - jnp.max(s, axis=-1, keepdims=True))
        l = jnp.sum(p, axis=-1, keepdims=True)
        o = jnp.dot(p.astype(BF16), v, preferred_element_type=F32)
        o_ref[chains[c]] = (o / l).astype(o_ref.dtype)
        s = s_next


def _gqa_attention(q_src, kv_srcs, *, batch, n_q_heads, n_kv_heads, tq, q_rows, name):
    group = n_q_heads // n_kv_heads
    gw = group * HEAD_DIM
    nq = q_rows // tq
    k_blk0 = n_q_heads
    v_blk0 = n_q_heads + n_kv_heads
    specs = [pl.BlockSpec((tq, gw), lambda b, h, i: (b * nq + i, h))]
    args = [q_src]
    total = 0
    for src in kv_srcs:
        rows = src.shape[0] // batch
        total += rows
        specs += [pl.BlockSpec((rows, HEAD_DIM), lambda b, h, i: (b, k_blk0 + h)),
                  pl.BlockSpec((rows, HEAD_DIM), lambda b, h, i: (b, v_blk0 + h))]
        args += [src, src]
    kern = functools.partial(_gqa_kernel, n_kv_src=len(kv_srcs), n_group=group)
    return pl.pallas_call(
        kern,
        grid=(batch, n_kv_heads, nq),
        in_specs=specs,
        out_specs=pl.BlockSpec((tq, gw), lambda b, h, i: (b * nq + i, h)),
        out_shape=jax.ShapeDtypeStruct((batch * q_rows, n_q_heads * HEAD_DIM), BF16),
        scratch_shapes=[pltpu.VMEM((total, HEAD_DIM), BF16), pltpu.VMEM((total, HEAD_DIM), BF16)],
        compiler_params=_params("parallel", "parallel", "arbitrary"),
        name=name,
    )(*args)


def _diff_kernel(*refs, n_kv_src, lam_init):
    q_ref = refs[0]
    kv_refs = [(refs[1 + 2 * s], refs[2 + 2 * s]) for s in range(n_kv_src)]
    lq1, lk1, lq2, lk2, gain_ref, o_ref, k_s, v_s = refs[1 + 2 * n_kv_src:]

    @pl.when(pl.program_id(2) == 0)
    def _():
        _fill_kv(k_s, v_s, kv_refs)

    lam = (jnp.exp(jnp.sum(lq1[...] * lk1[...], axis=-1, keepdims=True))
           - jnp.exp(jnp.sum(lq2[...] * lk2[...], axis=-1, keepdims=True)) + lam_init)
    dh = HEAD_DIM
    sub = min(ATTN_SUBTILE, q_ref.shape[0])
    chains = [slice(r0, r0 + sub) for r0 in range(0, q_ref.shape[0], sub)]

    def scores(c):
        return [lax.dot_general(q_ref[chains[c], m * dh:(m + 1) * dh], k_s[:, m * dh:(m + 1) * dh], NT_DIMS,
                                preferred_element_type=F32) for m in range(2)]

    s = scores(0)
    for c in range(len(chains)):
        s_next = scores(c + 1) if c + 1 < len(chains) else None
        p = [jnp.exp2(sm - jnp.max(sm, axis=-1, keepdims=True)) for sm in s]
        l0, l1 = [jnp.sum(pm, axis=-1, keepdims=True) for pm in p]
        w = p[0] * (1.0 / l0) - p[1] * (lam / l1)
        o = jnp.dot(w.astype(BF16), v_s[...], preferred_element_type=F32)
        o_ref[chains[c], :] = ((_rms(o) * gain_ref[...]) * (1.0 - lam_init)).astype(o_ref.dtype)
        s = s_next


def _diff_attention(q_src, kv_srcs, lams, out_gain, *, batch, n_heads, tq, q_rows, lam_init, name):
    hw = 2 * HEAD_DIM
    nq = q_rows // tq
    specs = [pl.BlockSpec((tq, hw), lambda b, h, i: (b * nq + i, h))]
    args = [q_src]
    total = 0
    for src in kv_srcs:
        rows = src.shape[0] // batch
        total += rows
        specs += [pl.BlockSpec((rows, hw), lambda b, h, i: (b, n_heads + h)),
                  pl.BlockSpec((rows, hw), lambda b, h, i: (b, 2 * n_heads + h))]
        args += [src, src]
    small = pl.BlockSpec((1, HEAD_DIM), lambda b, h, i: (0, 0))
    specs += [small] * 4 + [pl.BlockSpec((1, hw), lambda b, h, i: (0, 0))]
    args += [v.reshape(1, HEAD_DIM) for v in lams] + [out_gain.reshape(1, hw)]
    kern = functools.partial(_diff_kernel, n_kv_src=len(kv_srcs), lam_init=lam_init)
    return pl.pallas_call(
        kern,
        grid=(batch, n_heads, nq),
        in_specs=specs,
        out_specs=pl.BlockSpec((tq, hw), lambda b, h, i: (b * nq + i, h)),
        out_shape=jax.ShapeDtypeStruct((batch * q_rows, n_heads * hw), BF16),
        scratch_shapes=[pltpu.VMEM((total, hw), BF16), pltpu.VMEM((total, hw), BF16)],
        compiler_params=_params("parallel", "parallel", "arbitrary"),
        name=name,
    )(*args)


def _gla_blocks(blocks):
    r, dk = blocks[0][0].shape
    c = GLA_CHUNK
    nc = r // c
    n = range(len(blocks))
    qs_, ks_, vs_, zs_, wgs, bgs, st_refs, revs = zip(*blocks)
    row = lax.broadcasted_iota(jnp.int32, (r, r), 0)
    col = lax.broadcasted_iota(jnp.int32, (r, r), 1)
    same = row // c == col // c
    tri_f = [jnp.where(same, jnp.where((col >= row) if revs[i] else (col <= row), 1.0, 0.0), 0.0) for i in n]
    tri = [t.astype(BF16) for t in tri_f]
    mid = [c // 2 if revs[i] else c // 2 - 1 for i in n]
    last = [0 if revs[i] else c - 1 for i in n]

    g = [jax.nn.log_sigmoid(jnp.dot(zs_[i], wgs[i], preferred_element_type=F32) + bgs[i]) / GLA_TAU for i in n]
    g_hi = [g[i].astype(BF16) for i in n]
    g_lo = [(g[i] - g_hi[i].astype(F32)).astype(BF16) for i in n]
    cum = [jnp.dot(tri[i], g_hi[i], preferred_element_type=F32) + jnp.dot(tri[i], g_lo[i], preferred_element_type=F32)
           for i in n]

    def chunk_row(x, idx):
        return jnp.concatenate(
            [jnp.broadcast_to(x[ci * c + idx:ci * c + idx + 1, :], (c, dk)) for ci in range(nc)], axis=0)

    cum_mid = [chunk_row(cum[i], mid[i]) for i in n]
    cum_last = [chunk_row(cum[i], last[i]) for i in n]
    qs = [(qs_[i] * jnp.exp(cum[i] - cum_mid[i])).astype(BF16) for i in n]
    ks = [(ks_[i] * jnp.exp(cum_mid[i] - cum[i])).astype(BF16) for i in n]
    a = [lax.dot_general(qs[i], ks[i], NT_DIMS, preferred_element_type=F32) for i in n]
    a = [jnp.where(tri_f[i] > 0.5, a[i], 0.0).astype(BF16) for i in n]
    o_intra = [jnp.dot(a[i], vs_[i], preferred_element_type=F32) for i in n]
    q_inter = [(qs_[i] * jnp.exp(cum[i])).astype(BF16) for i in n]
    k_carry = [(ks_[i] * jnp.exp(cum_last[i] - cum[i])).astype(BF16) for i in n]
    st = [st_refs[i][...] for i in n]
    o_inter = [[None] * nc for _ in n]
    for step in range(nc):
        for i in n:
            ci = nc - 1 - step if revs[i] else step
            rows = slice(ci * c, (ci + 1) * c)
            o_inter[i][ci] = lax.dot_general(q_inter[i][rows], st[i].astype(BF16), NT_DIMS,
                                             preferred_element_type=F32)
            decay = jnp.exp(cum[i][ci * c + last[i]:ci * c + last[i] + 1, :])
            st[i] = st[i] * decay + lax.dot_general(vs_[i][rows], k_carry[i][rows], TN_DIMS,
                                                    preferred_element_type=F32)
    for i in n:
        st_refs[i][...] = st[i]
    return [o_intra[i] + jnp.concatenate(o_inter[i], axis=0) for i in n]


def _gla_kernel(qc_ref, kc_ref, vc_ref, rc_ref, zc_ref, ql_ref, kl_ref, vl_ref, rl_ref, zl_ref,
                wgf_ref, bgf_ref, wgb_ref, bgb_ref, gain_ref, oc_ref, ol_ref, sf_ref, sb_ref, of_ref, ob_ref):
    blk = GLA_BLOCK
    n_ctx = qc_ref.shape[0] // blk
    n_lat = ql_ref.shape[0] // blk
    q_scale = qc_ref.shape[1] ** -0.5
    ctx_refs = (qc_ref, kc_ref, vc_ref, zc_ref)
    lat_refs = (ql_ref, kl_ref, vl_ref, zl_ref)

    def rows_of(bi):
        return pl.ds(pl.multiple_of(bi * blk, blk), blk)

    def block(refs, bi, reverse):
        q_ref, k_ref, v_ref, z_ref = refs
        rows = rows_of(bi)
        wg_ref, bg_ref, st_ref = (wgb_ref, bgb_ref, sb_ref) if reverse else (wgf_ref, bgf_ref, sf_ref)
        return (q_ref[rows, :].astype(F32) * q_scale, k_ref[rows, :].astype(F32), v_ref[rows, :], z_ref[rows, :],
                wg_ref[0], bg_ref[0], st_ref, reverse)

    def both(refs, base, n):
        def body(t, carry):
            o_f, o_b = _gla_blocks([block(refs, t, False), block(refs, n - 1 - t, True)])
            of_ref[rows_of(base + t), :] = o_f
            ob_ref[rows_of(base + n - 1 - t), :] = o_b
            return carry
        return body

    def finish(r_ref, out_ref, base):
        def body(bi, carry):
            rows = rows_of(bi)
            o = of_ref[rows_of(base + bi), :] + ob_ref[rows_of(base + bi), :]
            y = (_rms(o) * gain_ref[...]) * jax.nn.silu(r_ref[rows, :].astype(F32))
            out_ref[rows, :] = y.astype(out_ref.dtype)
            return carry
        return body

    sf_ref[...] = jnp.zeros_like(sf_ref)
    sb_ref[...] = jnp.zeros_like(sb_ref)
    lax.fori_loop(0, n_ctx, both(ctx_refs, 0, n_ctx), 0)
    lax.fori_loop(0, n_lat, both(lat_refs, n_ctx, n_lat), 0)
    lax.fori_loop(0, n_ctx, finish(rc_ref, oc_ref, 0), 0)
    lax.fori_loop(0, n_lat, finish(rl_ref, ol_ref, n_ctx), 0)


def _gla(p_ctx, p_lat, wgf, bgf, wgb, bgb, out_gain, *, batch, dk, dv):
    h = GLA_HEADS
    rows_c = p_ctx.shape[0] // batch
    rows_l = p_lat.shape[0] // batch
    assert rows_c % GLA_BLOCK == 0 and rows_l % GLA_BLOCK == 0
    zblk = (2 * h * dk + 2 * h * dv) // LANES
    k0 = h
    v0 = (2 * h * dk) // dv
    r0 = v0 + h

    def stream(rows):
        return [pl.BlockSpec((rows, dk), lambda b, hh: (b, hh)),
                pl.BlockSpec((rows, dk), lambda b, hh: (b, k0 + hh)),
                pl.BlockSpec((rows, dv), lambda b, hh: (b, v0 + hh)),
                pl.BlockSpec((rows, dv), lambda b, hh: (b, r0 + hh)),
                pl.BlockSpec((rows, LANES), lambda b, hh: (b, zblk))]

    wspec = pl.BlockSpec((1, LANES, dk), lambda b, hh: (hh, 0, 0))
    bspec = pl.BlockSpec((1, 1, dk), lambda b, hh: (hh, 0, 0))
    return pl.pallas_call(
        _gla_kernel,
        grid=(batch, h),
        in_specs=stream(rows_c) + stream(rows_l) + [wspec, bspec, wspec, bspec,
                                                    pl.BlockSpec((1, dv), lambda b, hh: (0, 0))],
        out_specs=[pl.BlockSpec((rows_c, dv), lambda b, hh: (b, hh)),
                   pl.BlockSpec((rows_l, dv), lambda b, hh: (b, hh))],
        out_shape=[jax.ShapeDtypeStruct((batch * rows_c, h * dv), BF16),
                   jax.ShapeDtypeStruct((batch * rows_l, h * dv), BF16)],
        scratch_shapes=[pltpu.VMEM((dv, dk), F32), pltpu.VMEM((dv, dk), F32),
                        pltpu.VMEM((rows_c + rows_l, dv), F32), pltpu.VMEM((rows_c + rows_l, dv), F32)],
        compiler_params=_params("parallel", "parallel"),
        name="gla",
    )(*([p_ctx] * 5 + [p_lat] * 5 + [wgf, bgf, wgb, bgb, out_gain.reshape(1, dv)]))


def _dft_tables(n):
    idx = jnp.arange(n, dtype=jnp.int32)
    ang = ((idx[:, None] * idx[None, :]) % n).astype(F32) * (2.0 * math.pi / n)
    return jnp.cos(ang).astype(BF16), jnp.sin(ang).astype(BF16)


def _fnet_chan_kernel(x_ref, gain_ref, shift_ref, scale_ref, csc_ref, p_ref, q_ref, xn_ref):
    gd = csc_ref.shape[0]
    _norm_mod_rows(x_ref, xn_ref, gain_ref[...], shift_ref[0], scale_ref[0])
    for g in range(x_ref.shape[1] // gd):
        cols = slice(g * gd, (g + 1) * gd)
        pq = jnp.dot(xn_ref[:, cols], csc_ref[...], preferred_element_type=F32)
        p_ref[:, cols] = pq[:, :gd].astype(p_ref.dtype)
        q_ref[:, cols] = pq[:, gd:].astype(q_ref.dtype)


def _fnet_seq_kernel(cs_ref, ss_ref, p_ref, q_ref, o_ref, *, inv_norm):
    acc = (jnp.dot(cs_ref[...], p_ref[...], preferred_element_type=F32)
           - jnp.dot(ss_ref[...], q_ref[...], preferred_element_type=F32))
    o_ref[...] = (acc * inv_norm).astype(o_ref.dtype)


def _fnet(x, gain, shift, scale, *, batch, tm):
    t, d = x.shape
    s = t // batch
    gd = d // FNET_GROUPS
    tiles_per_batch = s // tm
    cc, sc = _dft_tables(gd)
    cs, ss = _dft_tables(s)
    vec = pl.BlockSpec((1, 1, d), lambda i: (i // tiles_per_batch, 0, 0))
    blk = pl.BlockSpec((tm, d), lambda i: (i, 0))
    p, q = pl.pallas_call(
        _fnet_chan_kernel,
        grid=(t // tm,),
        in_specs=[blk, pl.BlockSpec((1, d), lambda i: (0, 0)), vec, vec,
                  pl.BlockSpec((gd, 2 * gd), lambda i: (0, 0))],
        out_specs=[blk, blk],
        out_shape=[jax.ShapeDtypeStruct((t, d), BF16)] * 2,
        scratch_shapes=[pltpu.VMEM((tm, d), BF16)],
        compiler_params=_params("parallel"),
        name="fnet_chan",
    )(x, gain.reshape(1, d), shift, scale, jnp.concatenate([cc, sc], axis=1))
    rows = pl.BlockSpec((tm, s), lambda b, j, i: (i, 0))
    cols = pl.BlockSpec((s, gd), lambda b, j, i: (b, j))
    return pl.pallas_call(
        functools.partial(_fnet_seq_kernel, inv_norm=float((s * gd) ** -0.5)),
        grid=(batch, d // gd, tiles_per_batch),
        in_specs=[rows, rows, cols, cols],
        out_specs=pl.BlockSpec((tm, gd), lambda b, j, i: (b * tiles_per_batch + i, j)),
        out_shape=jax.ShapeDtypeStruct((t, d), BF16),
        compiler_params=_params("parallel", "parallel", "arbitrary"),
        name="fnet_seq",
    )(cs, ss, p, q)


def _rope_tables(n_tokens):
    t = jnp.arange(n_tokens)
    row = (t // GRID_W).astype(F32)
    col = (t % GRID_W).astype(F32)
    half = HEAD_DIM // 2
    inv_freq = ROPE_THETA ** (-jnp.arange(0, half, 2, dtype=F32) / half)
    ang_r = row[:, None] * inv_freq[None, :]
    ang_c = col[:, None] * inv_freq[None, :]
    ang = jnp.concatenate([ang_r, ang_r, ang_c, ang_c], axis=-1)
    sign = jnp.concatenate([-jnp.ones((half // 2,), F32), jnp.ones((half // 2,), F32)] * 2)
    return jnp.cos(ang), jnp.sin(ang) * sign


def _column_gains(q_gain, k_gain, q_cols, k_cols, v_cols):
    qg = jnp.tile(q_gain.astype(F32) * (HEAD_DIM ** -0.5 * math.log2(math.e)), q_cols // HEAD_DIM)
    kg = jnp.tile(k_gain.astype(F32), k_cols // HEAD_DIM)
    return jnp.concatenate([qg, kg, jnp.ones((v_cols,), F32)])[None, :]


def _gate_weights(wg, bg, lane0, dk):
    r = wg.shape[0]
    w = wg.reshape(r, GLA_HEADS, dk).transpose(1, 0, 2)
    w = jnp.pad(w, ((0, 0), (lane0, LANES - lane0 - r), (0, 0))).astype(BF16)
    return w, bg.reshape(GLA_HEADS, 1, dk).astype(F32)


def kernel(x, c, ctx, c_ctx, l0_mod_w, l0_mod_b, l0_norm1, l0_gla_w_in, l0_gla_wg_f, l0_gla_bg_f, l0_gla_wg_b, l0_gla_bg_b, l0_gla_out_norm, l0_gla_w_out, l0_norm2, l0_ffn_w_in, l0_ffn_w_out, l1_mod_w, l1_mod_b, l1_norm1, l1_gqa_w_in, l1_gqa_q_norm, l1_gqa_k_norm, l1_gqa_w_out, l1_norm2, l1_ffn_w_in, l1_ffn_w_out, l2_mod_w, l2_mod_b, l2_norm1, l2_diff_w_in, l2_diff_q_norm, l2_diff_k_norm, l2_diff_lq1, l2_diff_lk1, l2_diff_lq2, l2_diff_lk2, l2_diff_out_norm, l2_diff_w_out, l2_norm2, l2_ffn_w_in, l2_ffn_w_out, l3_mod_w, l3_mod_b, l3_norm1, l3_fnet_w_out, l3_norm2, l3_ffn_w_in, l3_ffn_w_out):
    b, s, d = x.shape
    n_ctx = ctx.shape[1]
    xl = x.reshape(b * s, d)
    xc = ctx.reshape(b * n_ctx, d)
    tm = math.gcd(s, 1024)
    tm_c = math.gcd(b * n_ctx, 1024)
    tm_r = math.gcd(s, 512)
    tm_rc = math.gcd(b * n_ctx, 512)
    th = math.gcd(l0_ffn_w_out.shape[0], 512)
    tq_gqa = math.gcd(s, 2 * ATTN_SUBTILE)
    tq_diff = math.gcd(s, 4 * ATTN_SUBTILE)

    n_cond = -(-(b + 1) // BF16_ROWS) * BF16_ROWS
    cond = jnp.concatenate([c, c_ctx[None, :], jnp.zeros((n_cond - b - 1, d), F32)], axis=0)

    def modulation(mod_w, mod_b):
        m = _mm(cond, mod_w, tm=n_cond, tn=512, out_dtype=F32, prologue="silu", epilogue="bias", bias=mod_b,
                name="modulation")
        lat = [m[:b, k * d:(k + 1) * d].reshape(b, 1, d) for k in range(6)]
        cx = [m[b:b + 1, k * d:(k + 1) * d].reshape(1, 1, d) for k in range(6)]
        return lat, cx

    def tiling(rows, resident=False):
        if rows == s:
            return dict(tm=tm_r if resident else tm, rows_per_batch=s)
        return dict(tm=tm_rc if resident else tm_c, rows_per_batch=None)

    def proj(xs, w, n1, sh, sc, rows, tn, name, **kw):
        return _mm(xs, w, tn=tn, out_dtype=BF16, prologue="norm_mod", gain=n1, shift=sh, scale=sc, name=name,
                   **tiling(rows), **kw)

    def qkv_proj(xs, w, n1, sh, sc, col_gain, rows, name, **kw):
        return _qkv_proj(xs, w, n1, sh, sc, col_gain, tn=math.gcd(w.shape[1], 1024), name=name, **tiling(rows), **kw)

    def out_resid(y, w, xs, gate, rows):
        return _mm(y, w.astype(BF16), tn=w.shape[1], out_dtype=F32, epilogue="resid", res=xs, gate=gate,
                   name="out_resid", **tiling(rows, resident=True))

    def ffn(xs, n2, sh, sc, gate, w_in, w_out, rows):
        return _ffn(xs, n2, sh, sc, gate, w_in.astype(BF16), w_out.astype(BF16), th=th, name="ffn", **tiling(rows))

    (sh1, sc1, g1, sh2, sc2, g2), (csh1, csc1, cg1, csh2, csc2, cg2) = modulation(l0_mod_w, l0_mod_b)
    dk = l0_gla_wg_f.shape[1] // GLA_HEADS
    dv = d // GLA_HEADS
    n_in = l0_gla_w_in.shape[1]
    tn0 = 1280
    n_pad = -(-(n_in - 2 * GLA_RANK + LANES) // tn0) * tn0
    w0 = jnp.pad(l0_gla_w_in, ((0, 0), (0, n_pad - n_in))).astype(BF16)
    pl0 = proj(xl, w0, l0_norm1, sh1, sc1, s, tn0, "gla_proj")
    pc0 = proj(xc, w0, l0_norm1, csh1, csc1, n_ctx, tn0, "gla_proj_ctx")
    wgf, bgf = _gate_weights(l0_gla_wg_f, l0_gla_bg_f, 0, dk)
    wgb, bgb = _gate_weights(l0_gla_wg_b, l0_gla_bg_b, GLA_RANK, dk)
    yc, yl = _gla(pc0, pl0, wgf, bgf, wgb, bgb, l0_gla_out_norm, batch=b, dk=dk, dv=dv)
    xl = out_resid(yl, l0_gla_w_out, xl, g1, s)
    xc = out_resid(yc, l0_gla_w_out, xc, cg1, n_ctx)
    xl = ffn(xl, l0_norm2, sh2, sc2, g2, l0_ffn_w_in, l0_ffn_w_out, s)
    xc = ffn(xc, l0_norm2, csh2, csc2, cg2, l0_ffn_w_in, l0_ffn_w_out, n_ctx)

    (sh1, sc1, g1, sh2, sc2, g2), (csh1, csc1, cg1, csh2, csc2, cg2) = modulation(l1_mod_w, l1_mod_b)
    n_heads = d // HEAD_DIM
    cos, sin = _rope_tables(s)
    kv_cols = GQA_KV_HEADS * HEAD_DIM
    cg_gqa = _column_gains(l1_gqa_q_norm, l1_gqa_k_norm, d, kv_cols, kv_cols)
    w1 = l1_gqa_w_in.astype(BF16)
    pl1 = qkv_proj(xl, w1, l1_norm1, sh1, sc1, cg_gqa, s, "gqa_proj", normed_cols=d + kv_cols, cos=cos, sin=sin)
    pc1 = qkv_proj(xc, w1, l1_norm1, csh1, csc1, cg_gqa, n_ctx, "gqa_proj_ctx", normed_cols=d + kv_cols)
    yl = _gqa_attention(pl1, [pl1, pc1], batch=b, n_q_heads=n_heads, n_kv_heads=GQA_KV_HEADS, tq=tq_gqa, q_rows=s,
                        name="gqa_attn")
    yc = _gqa_attention(pc1, [pc1], batch=b, n_q_heads=n_heads, n_kv_heads=GQA_KV_HEADS, tq=n_ctx, q_rows=n_ctx,
                        name="gqa_attn_ctx")
    xl = out_resid(yl, l1_gqa_w_out, xl, g1, s)
    xc = out_resid(yc, l1_gqa_w_out, xc, cg1, n_ctx)
    xl = ffn(xl, l1_norm2, sh2, sc2, g2, l1_ffn_w_in, l1_ffn_w_out, s)
    xc = ffn(xc, l1_norm2, csh2, csc2, cg2, l1_ffn_w_in, l1_ffn_w_out, n_ctx)

    (sh1, sc1, g1, sh2, sc2, g2), (csh1, csc1, _, _, _, _) = modulation(l2_mod_w, l2_mod_b)
    lam_init = 0.8 - 0.6 * math.exp(-0.3 * 2)
    cg_diff = _column_gains(l2_diff_q_norm, l2_diff_k_norm, d, d, d)
    w2 = l2_diff_w_in.astype(BF16)
    pl2 = qkv_proj(xl, w2, l2_norm1, sh1, sc1, cg_diff, s, "diff_proj", normed_cols=2 * d, cos=cos, sin=sin)
    pc2 = qkv_proj(xc, w2, l2_norm1, csh1, csc1, cg_diff, n_ctx, "diff_proj_ctx", normed_cols=2 * d)
    yl = _diff_attention(pl2, [pl2, pc2], (l2_diff_lq1, l2_diff_lk1, l2_diff_lq2, l2_diff_lk2),
                         l2_diff_out_norm, batch=b, n_heads=n_heads // 2, tq=tq_diff, q_rows=s, lam_init=lam_init,
                         name="diff_attn")
    xl = out_resid(yl, l2_diff_w_out, xl, g1, s)
    xl = ffn(xl, l2_norm2, sh2, sc2, g2, l2_ffn_w_in, l2_ffn_w_out, s)

    (sh1, sc1, g1, sh2, sc2, g2), _ = modulation(l3_mod_w, l3_mod_b)
    yl = _fnet(xl, l3_norm1, sh1, sc1, batch=b, tm=tm)
    xl = out_resid(yl, l3_fnet_w_out, xl, g1, s)
    xl = ffn(xl, l3_norm2, sh2, sc2, g2, l3_ffn_w_in, l3_ffn_w_out, s)
    return xl.reshape(b, s, d)
```

```python
import functools
import math

import jax
import jax.numpy as jnp
from jax import lax
from jax.experimental import pallas as pl
from jax.experimental.pallas import tpu as pltpu

F32 = jnp.float32
BF16 = jnp.bfloat16

NORM_EPS = 1e-6
ROPE_THETA = 10000.0
GRID_W = 64
HEAD_DIM = 128
GQA_KV_HEADS = 4
GLA_HEADS = 4
GLA_RANK = 16
GLA_TAU = 16.0
GLA_CHUNK = 64
GLA_BLOCK = 4 * GLA_CHUNK
FNET_GROUPS = 4

LANES = 128
BF16_ROWS = 16
STRIP_UNROLL = 8
VMEM_LIMIT = 56 * 1024 * 1024

NT_DIMS = (((1,), (1,)), ((), ()))
TN_DIMS = (((0,), (0,)), ((), ()))


def _params(*sem):
    return pltpu.CompilerParams(dimension_semantics=sem, vmem_limit_bytes=VMEM_LIMIT)


def _rms(x, eps=NORM_EPS):
    return x * lax.rsqrt(jnp.mean(x * x, axis=-1, keepdims=True) + eps)


def _norm_mod_rows(x_ref, xn_ref, gain, shift, scale):
    mult = gain * (1.0 + scale)

    def strip(r, carry):
        rows = pl.ds(pl.multiple_of(r * BF16_ROWS, BF16_ROWS), BF16_ROWS)
        xn_ref[rows, :] = (_rms(x_ref[rows, :]) * mult + shift).astype(BF16)
        return carry

    lax.fori_loop(0, x_ref.shape[0] // BF16_ROWS, strip, 0, unroll=STRIP_UNROLL)


ROPE_STRIP = 256
MXU_WIDTH = 256


def _head_group_matrices():
    src = lax.broadcasted_iota(jnp.int32, (MXU_WIDTH, MXU_WIDTH), 0)
    dst = lax.broadcasted_iota(jnp.int32, (MXU_WIDTH, MXU_WIDTH), 1)
    quarter = HEAD_DIM // 4
    partner = jnp.where((dst // quarter) % 2 == 0, dst + quarter, dst - quarter)
    mean = jnp.where(src // HEAD_DIM == dst // HEAD_DIM, 1.0 / HEAD_DIM, 0.0).astype(BF16)
    return mean, (src == partner).astype(BF16)


def _mm_kernel(*refs, prologue, epilogue):
    it = iter(refs)
    x_ref = next(it)
    if prologue == "norm_mod":
        gain_ref, shift_ref, scale_ref = next(it), next(it), next(it)
    w_ref = next(it)
    if epilogue == "bias":
        b_ref = next(it)
    elif epilogue == "resid":
        res_ref, gate_ref = next(it), next(it)
    o_ref = next(it)
    if prologue != "cast":
        xn_ref = next(it)
    j = pl.program_id(1)

    if prologue == "cast":
        a = x_ref[...].astype(BF16)
    else:
        @pl.when(j == 0)
        def _():
            if prologue == "norm_mod":
                _norm_mod_rows(x_ref, xn_ref, gain_ref[...], shift_ref[0], scale_ref[0])
            else:
                xn_ref[...] = jax.nn.silu(x_ref[...]).astype(BF16)

        a = xn_ref[...]
    acc = jnp.dot(a, w_ref[...].astype(BF16), preferred_element_type=F32)
    if epilogue == "store":
        o_ref[...] = acc.astype(o_ref.dtype)
    elif epilogue == "bias":
        o_ref[...] = (acc + b_ref[...]).astype(o_ref.dtype)
    elif epilogue == "resid":
        o_ref[...] = (res_ref[...] + gate_ref[0] * acc).astype(o_ref.dtype)


def _mm(x, w, *, tm, tn, out_dtype, name, prologue="cast", epilogue="store", rows_per_batch=None,
        gain=None, shift=None, scale=None, bias=None, res=None, gate=None):
    t, k = x.shape
    n = w.shape[1]
    assert t % tm == 0 and n % tn == 0, (t, tm, n, tn)
    rows_per_batch = rows_per_batch or t
    assert rows_per_batch % tm == 0
    tiles_per_batch = rows_per_batch // tm
    once = dict(pipeline_mode=pl.Buffered(1))

    def batch_of(i):
        return i // tiles_per_batch

    args = [x]
    specs = [pl.BlockSpec((tm, k), lambda i, j: (i, 0))]
    if prologue == "norm_mod":
        nb = shift.shape[0]
        bsel = (lambda i: batch_of(i)) if nb > 1 else (lambda i: 0)
        args += [gain.reshape(1, k), shift, scale]
        specs += [pl.BlockSpec((1, k), lambda i, j: (0, 0)),
                  pl.BlockSpec((1, 1, k), lambda i, j: (bsel(i), 0, 0)),
                  pl.BlockSpec((1, 1, k), lambda i, j: (bsel(i), 0, 0))]
    args.append(w)
    specs.append(pl.BlockSpec((k, tn), lambda i, j: (0, j), **(once if tn == n else {})))
    if epilogue == "bias":
        args.append(bias.reshape(1, n))
        specs.append(pl.BlockSpec((1, tn), lambda i, j: (0, j)))
    elif epilogue == "resid":
        nb = gate.shape[0]
        gsel = (lambda i: batch_of(i)) if nb > 1 else (lambda i: 0)
        args += [res, gate]
        specs += [pl.BlockSpec((tm, tn), lambda i, j: (i, j)),
                  pl.BlockSpec((1, 1, tn), lambda i, j: (gsel(i), 0, j))]
    scratch = []
    if prologue != "cast":
        scratch.append(pltpu.VMEM((tm, k), BF16))
    kern = functools.partial(_mm_kernel, prologue=prologue, epilogue=epilogue)
    return pl.pallas_call(
        kern,
        grid=(t // tm, n // tn),
        in_specs=specs,
        out_specs=pl.BlockSpec((tm, tn), lambda i, j: (i, j)),
        out_shape=jax.ShapeDtypeStruct((t, n), out_dtype),
        scratch_shapes=scratch,
        compiler_params=_params("parallel", "arbitrary"),
        name=name,
    )(*args)


def _qkv_kernel(*refs, normed_cols, rope):
    it = iter(refs)
    x_ref, gain_ref, shift_ref, scale_ref, w_ref, hg_ref, ones_ref = [next(it) for _ in range(7)]
    if rope:
        perm_ref, cos_ref, sin_ref = next(it), next(it), next(it)
    o_ref, xn_ref = next(it), next(it)
    j = pl.program_id(1)
    tm, tn = o_ref.shape
    gw = ones_ref.shape[0]
    n_sub = tn // gw
    full_tiles, rem = divmod(normed_cols, tn)

    @pl.when(j == 0)
    def _():
        _norm_mod_rows(x_ref, xn_ref, gain_ref[...], shift_ref[0], scale_ref[0])

    def norm_rope_store(acc, cols):
        for r0 in range(0, tm, ROPE_STRIP):
            rows = slice(r0, r0 + ROPE_STRIP)
            blk = acc[rows, :]
            ms = jnp.dot((blk * blk).astype(BF16), ones_ref[...], preferred_element_type=F32)
            y = blk * lax.rsqrt(ms + NORM_EPS) * hg_ref[:, cols]
            if rope:
                partner = jnp.dot(y.astype(BF16), perm_ref[...], preferred_element_type=F32)
                reps = gw // HEAD_DIM
                y = (y * jnp.concatenate([cos_ref[rows, :]] * reps, axis=-1)
                     + partner * jnp.concatenate([sin_ref[rows, :]] * reps, axis=-1))
            o_ref[rows, cols] = y.astype(o_ref.dtype)

    def tile(n_normed_subs):
        a = xn_ref[...]
        if n_normed_subs == 0:
            o_ref[...] = jnp.dot(a, w_ref[...], preferred_element_type=F32).astype(o_ref.dtype)
            return
        sub_cols = [slice(s * gw, (s + 1) * gw) for s in range(n_sub)]
        acc = jnp.dot(a, w_ref[:, sub_cols[0]], preferred_element_type=F32)
        for s in range(n_sub):
            nxt = jnp.dot(a, w_ref[:, sub_cols[s + 1]], preferred_element_type=F32) if s + 1 < n_sub else None
            if s < n_normed_subs:
                norm_rope_store(acc, sub_cols[s])
            else:
                o_ref[:, sub_cols[s]] = acc.astype(o_ref.dtype)
            acc = nxt

    if full_tiles:
        pl.when(j < full_tiles)(functools.partial(tile, n_sub))
    if rem:
        pl.when(j == full_tiles)(functools.partial(tile, rem // gw))
    pl.when(j >= full_tiles + (1 if rem else 0))(functools.partial(tile, 0))


def _qkv_proj(x, w, gain, shift, scale, col_gain, *, normed_cols, tm, tn, name, rows_per_batch=None,
              cos=None, sin=None):
    t, k = x.shape
    n = w.shape[1]
    assert t % tm == 0 and n % tn == 0 and normed_cols % MXU_WIDTH == 0
    rows_per_batch = rows_per_batch or t
    assert rows_per_batch % tm == 0
    tiles_per_batch = rows_per_batch // tm
    rope = cos is not None
    nb = shift.shape[0]
    bsel = (lambda i: i // tiles_per_batch) if nb > 1 else (lambda i: 0)
    ones, perm = _head_group_matrices()
    const = pl.BlockSpec(ones.shape, lambda i, j: (0, 0))
    vec = pl.BlockSpec((1, 1, k), lambda i, j: (bsel(i), 0, 0))
    args = [x, gain.reshape(1, k), shift, scale, w, col_gain, ones]
    specs = [pl.BlockSpec((tm, k), lambda i, j: (i, 0)),
             pl.BlockSpec((1, k), lambda i, j: (0, 0)), vec, vec,
             pl.BlockSpec((k, tn), lambda i, j: (0, j)),
             pl.BlockSpec((1, tn), lambda i, j: (0, j)), const]
    if rope:
        table = pl.BlockSpec((tm, HEAD_DIM), lambda i, j: (i % tiles_per_batch, 0))
        args += [perm, cos, sin]
        specs += [const, table, table]
    return pl.pallas_call(
        functools.partial(_qkv_kernel, normed_cols=normed_cols, rope=rope),
        grid=(t // tm, n // tn),
        in_specs=specs,
        out_specs=pl.BlockSpec((tm, tn), lambda i, j: (i, j)),
        out_shape=jax.ShapeDtypeStruct((t, n), BF16),
        scratch_shapes=[pltpu.VMEM((tm, k), BF16)],
        compiler_params=_params("parallel", "arbitrary"),
        name=name,
    )(*args)


FFN_HID_SPLIT = 2


def _ffn_kernel(x_ref, gain_ref, shift_ref, scale_ref, gate_ref, wg_ref, wu_ref, wo_ref, o_ref, xn_ref):
    j = pl.program_id(1)
    hc = wg_ref.shape[1] // FFN_HID_SPLIT
    cols = [slice(c * hc, (c + 1) * hc) for c in range(FFN_HID_SPLIT)]

    def chunk(first):
        if first:
            _norm_mod_rows(x_ref, xn_ref, gain_ref[...], shift_ref[0], scale_ref[0])
        xn = xn_ref[...]
        gu = [(jnp.dot(xn, wg_ref[:, cs], preferred_element_type=F32),
               jnp.dot(xn, wu_ref[:, cs], preferred_element_type=F32)) for cs in cols]
        for c, (g, u) in enumerate(gu):
            a = (jax.nn.silu(g) * u).astype(BF16)
            part = jnp.dot(a, wo_ref[cols[c], :], preferred_element_type=F32)
            if first and c == 0:
                o_ref[...] = part
            else:
                o_ref[...] += part

    pl.when(j == 0)(functools.partial(chunk, True))
    pl.when(j > 0)(functools.partial(chunk, False))

    @pl.when(j == pl.num_programs(1) - 1)
    def _():
        o_ref[...] = x_ref[...] + gate_ref[0] * o_ref[...]


def _ffn(x, gain, shift, scale, gate, w_in, w_out, *, tm, th, name, rows_per_batch=None):
    t, d = x.shape
    hidden = w_out.shape[0]
    assert t % tm == 0 and hidden % th == 0
    nh = hidden // th
    rows_per_batch = rows_per_batch or t
    tiles_per_batch = rows_per_batch // tm
    nb = shift.shape[0]
    bsel = (lambda i: i // tiles_per_batch) if nb > 1 else (lambda i: 0)
    vec = pl.BlockSpec((1, 1, d), lambda i, j: (bsel(i), 0, 0))
    return pl.pallas_call(
        _ffn_kernel,
        grid=(t // tm, nh),
        in_specs=[pl.BlockSpec((tm, d), lambda i, j: (i, 0)),
                  pl.BlockSpec((1, d), lambda i, j: (0, 0)),
                  vec, vec, vec,
                  pl.BlockSpec((d, th), lambda i, j: (0, j)),
                  pl.BlockSpec((d, th), lambda i, j: (0, nh + j)),
                  pl.BlockSpec((th, d), lambda i, j: (j, 0))],
        out_specs=pl.BlockSpec((tm, d), lambda i, j: (i, 0)),
        out_shape=jax.ShapeDtypeStruct((t, d), F32),
        scratch_shapes=[pltpu.VMEM((tm, d), BF16)],
        compiler_params=_params("parallel", "arbitrary"),
        name=name,
    )(x, gain.reshape(1, d), shift, scale, gate, w_in, w_in, w_out)


def _fill_kv(k_s, v_s, kv_refs):
    off = 0
    for k_ref, v_ref in kv_refs:
        n = k_ref.shape[0]
        k_s[off:off + n, :] = k_ref[...]
        v_s[off:off + n, :] = v_ref[...]
        off += n


ATTN_SUBTILE = 256


def _gqa_kernel(*refs, n_kv_src, n_group):
    q_ref = refs[0]
    kv_refs = [(refs[1 + 2 * s], refs[2 + 2 * s]) for s in range(n_kv_src)]
    o_ref, k_s, v_s = refs[1 + 2 * n_kv_src:]

    @pl.when(pl.program_id(2) == 0)
    def _():
        _fill_kv(k_s, v_s, kv_refs)

    k = k_s[...]
    v = v_s[...]
    sub = min(ATTN_SUBTILE, q_ref.shape[0])
    chains = [(slice(r0, r0 + sub), slice(g * HEAD_DIM, (g + 1) * HEAD_DIM))
              for r0 in range(0, q_ref.shape[0], sub) for g in range(n_group)]

    def scores(c):
        return lax.dot_general(q_ref[chains[c]], k, NT_DIMS, preferred_element_type=F32)

    s = scores(0)
    for c in range(len(chains)):
        s_next = scores(c + 1) if c + 1 < len(chains) else None
        p = jnp.exp2(s - jnp.max(s, axis=-1, keepdims=True))
        l = jnp.sum(p, axis=-1, keepdims=True)
        o = jnp.dot(p.astype(BF16), v, preferred_element_type=F32)
        o_ref[chains[c]] = (o / l).astype(o_ref.dtype)
        s = s_next


def _gqa_attention(q_src, kv_srcs, *, batch, n_q_heads, n_kv_heads, tq, q_rows, name):
    group = n_q_heads // n_kv_heads
    gw = group * HEAD_DIM
    nq = q_rows // tq
    k_blk0 = n_q_heads
    v_blk0 = n_q_heads + n_kv_heads
    specs = [pl.BlockSpec((tq, gw), lambda b, h, i: (b * nq + i, h))]
    args = [q_src]
    total = 0
    for src in kv_srcs:
        rows = src.shape[0] // batch
        total += rows
        specs += [pl.BlockSpec((rows, HEAD_DIM), lambda b, h, i: (b, k_blk0 + h)),
                  pl.BlockSpec((rows, HEAD_DIM), lambda b, h, i: (b, v_blk0 + h))]
        args += [src, src]
    kern = functools.partial(_gqa_kernel, n_kv_src=len(kv_srcs), n_group=group)
    return pl.pallas_call(
        kern,
        grid=(batch, n_kv_heads, nq),
        in_specs=specs,
        out_specs=pl.BlockSpec((tq, gw), lambda b, h, i: (b * nq + i, h)),
        out_shape=jax.ShapeDtypeStruct((batch * q_rows, n_q_heads * HEAD_DIM), BF16),
        scratch_shapes=[pltpu.VMEM((total, HEAD_DIM), BF16), pltpu.VMEM((total, HEAD_DIM), BF16)],
        compiler_params=_params("parallel", "parallel", "arbitrary"),
        name=name,
    )(*args)


def _diff_kernel(*refs, n_kv_src, lam_init):
    q_ref = refs[0]
    kv_refs = [(refs[1 + 2 * s], refs[2 + 2 * s]) for s in range(n_kv_src)]
    lq1, lk1, lq2, lk2, gain_ref, o_ref, k_s, v_s = refs[1 + 2 * n_kv_src:]

    @pl.when(pl.program_id(2) == 0)
    def _():
        _fill_kv(k_s, v_s, kv_refs)

    lam = (jnp.exp(jnp.sum(lq1[...] * lk1[...], axis=-1, keepdims=True))
           - jnp.exp(jnp.sum(lq2[...] * lk2[...], axis=-1, keepdims=True)) + lam_init)
    dh = HEAD_DIM
    sub = min(ATTN_SUBTILE, q_ref.shape[0])
    chains = [slice(r0, r0 + sub) for r0 in range(0, q_ref.shape[0], sub)]

    def scores(c):
        return [lax.dot_general(q_ref[chains[c], m * dh:(m + 1) * dh], k_s[:, m * dh:(m + 1) * dh], NT_DIMS,
                                preferred_element_type=F32) for m in range(2)]

    s = scores(0)
    for c in range(len(chains)):
        s_next = scores(c + 1) if c + 1 < len(chains) else None
        p = [jnp.exp2(sm - jnp.max(sm, axis=-1, keepdims=True)) for sm in s]
        l0, l1 = [jnp.sum(pm, axis=-1, keepdims=True) for pm in p]
        w = p[0] * (1.0 / l0) - p[1] * (lam / l1)
        o = jnp.dot(w.astype(BF16), v_s[...], preferred_element_type=F32)
        o_ref[chains[c], :] = ((_rms(o) * gain_ref[...]) * (1.0 - lam_init)).astype(o_ref.dtype)
        s = s_next


def _diff_attention(q_src, kv_srcs, lams, out_gain, *, batch, n_heads, tq, q_rows, lam_init, name):
    hw = 2 * HEAD_DIM
    nq = q_rows // tq
    specs = [pl.BlockSpec((tq, hw), lambda b, h, i: (b * nq + i, h))]
    args = [q_src]
    total = 0
    for src in kv_srcs:
        rows = src.shape[0] // batch
        total += rows
        specs += [pl.BlockSpec((rows, hw), lambda b, h, i: (b, n_heads + h)),
                  pl.BlockSpec((rows, hw), lambda b, h, i: (b, 2 * n_heads + h))]
        args += [src, src]
    small = pl.BlockSpec((1, HEAD_DIM), lambda b, h, i: (0, 0))
    specs += [small] * 4 + [pl.BlockSpec((1, hw), lambda b, h, i: (0, 0))]
    args += [v.reshape(1, HEAD_DIM) for v in lams] + [out_gain.reshape(1, hw)]
    kern = functools.partial(_diff_kernel, n_kv_src=len(kv_srcs), lam_init=lam_init)
    return pl.pallas_call(
        kern,
        grid=(batch, n_heads, nq),
        in_specs=specs,
        out_specs=pl.BlockSpec((tq, hw), lambda b, h, i: (b * nq + i, h)),
        out_shape=jax.ShapeDtypeStruct((batch * q_rows, n_heads * hw), BF16),
        scratch_shapes=[pltpu.VMEM((total, hw), BF16), pltpu.VMEM((total, hw), BF16)],
        compiler_params=_params("parallel", "parallel", "arbitrary"),
        name=name,
    )(*args)


def _gla_blocks(blocks):
    r, dk = blocks[0][0].shape
    c = GLA_CHUNK
    nc = r // c
    n = range(len(blocks))
    qs_, ks_, vs_, zs_, wgs, bgs, st_refs, revs = zip(*blocks)
    row = lax.broadcasted_iota(jnp.int32, (r, r), 0)
    col = lax.broadcasted_iota(jnp.int32, (r, r), 1)
    same = row // c == col // c
    tri_f = [jnp.where(same, jnp.where((col >= row) if revs[i] else (col <= row), 1.0, 0.0), 0.0) for i in n]
    tri = [t.astype(BF16) for t in tri_f]
    mid = [c // 2 if revs[i] else c // 2 - 1 for i in n]
    last = [0 if revs[i] else c - 1 for i in n]

    g = [jax.nn.log_sigmoid(jnp.dot(zs_[i], wgs[i], preferred_element_type=F32) + bgs[i]) / GLA_TAU for i in n]
    g_hi = [g[i].astype(BF16) for i in n]
    g_lo = [(g[i] - g_hi[i].astype(F32)).astype(BF16) for i in n]
    cum = [jnp.dot(tri[i], g_hi[i], preferred_element_type=F32) + jnp.dot(tri[i], g_lo[i], preferred_element_type=F32)
           for i in n]

    def chunk_row(x, idx):
        return jnp.concatenate(
            [jnp.broadcast_to(x[ci * c + idx:ci * c + idx + 1, :], (c, dk)) for ci in range(nc)], axis=0)

    cum_mid = [chunk_row(cum[i], mid[i]) for i in n]
    cum_last = [chunk_row(cum[i], last[i]) for i in n]
    qs = [(qs_[i] * jnp.exp(cum[i] - cum_mid[i])).astype(BF16) for i in n]
    ks = [(ks_[i] * jnp.exp(cum_mid[i] - cum[i])).astype(BF16) for i in n]
    a = [lax.dot_general(qs[i], ks[i], NT_DIMS, preferred_element_type=F32) for i in n]
    a = [jnp.where(tri_f[i] > 0.5, a[i], 0.0).astype(BF16) for i in n]
    o_intra = [jnp.dot(a[i], vs_[i], preferred_element_type=F32) for i in n]
    q_inter = [(qs_[i] * jnp.exp(cum[i])).astype(BF16) for i in n]
    k_carry = [(ks_[i] * jnp.exp(cum_last[i] - cum[i])).astype(BF16) for i in n]
    st = [st_refs[i][...] for i in n]
    o_inter = [[None] * nc for _ in n]
    for step in range(nc):
        for i in n:
            ci = nc - 1 - step if revs[i] else step
            rows = slice(ci * c, (ci + 1) * c)
            o_inter[i][ci] = lax.dot_general(q_inter[i][rows], st[i].astype(BF16), NT_DIMS,
                                             preferred_element_type=F32)
            decay = jnp.exp(cum[i][ci * c + last[i]:ci * c + last[i] + 1, :])
            st[i] = st[i] * decay + lax.dot_general(vs_[i][rows], k_carry[i][rows], TN_DIMS,
                                                    preferred_element_type=F32)
    for i in n:
        st_refs[i][...] = st[i]
    return [o_intra[i] + jnp.concatenate(o_inter[i], axis=0) for i in n]


def _gla_kernel(qc_ref, kc_ref, vc_ref, rc_ref, zc_ref, ql_ref, kl_ref, vl_ref, rl_ref, zl_ref,
                wgf_ref, bgf_ref, wgb_ref, bgb_ref, gain_ref, oc_ref, ol_ref, sf_ref, sb_ref, of_ref, ob_ref):
    blk = GLA_BLOCK
    n_ctx = qc_ref.shape[0] // blk
    n_lat = ql_ref.shape[0] // blk
    q_scale = qc_ref.shape[1] ** -0.5
    ctx_refs = (qc_ref, kc_ref, vc_ref, zc_ref)
    lat_refs = (ql_ref, kl_ref, vl_ref, zl_ref)

    def rows_of(bi):
        return pl.ds(pl.multiple_of(bi * blk, blk), blk)

    def block(refs, bi, reverse):
        q_ref, k_ref, v_ref, z_ref = refs
        rows = rows_of(bi)
        wg_ref, bg_ref, st_ref = (wgb_ref, bgb_ref, sb_ref) if reverse else (wgf_ref, bgf_ref, sf_ref)
        return (q_ref[rows, :].astype(F32) * q_scale, k_ref[rows, :].astype(F32), v_ref[rows, :], z_ref[rows, :],
                wg_ref[0], bg_ref[0], st_ref, reverse)

    def both(refs, base, n):
        def body(t, carry):
            o_f, o_b = _gla_blocks([block(refs, t, False), block(refs, n - 1 - t, True)])
            of_ref[rows_of(base + t), :] = o_f
            ob_ref[rows_of(base + n - 1 - t), :] = o_b
            return carry
        return body

    def finish(r_ref, out_ref, base):
        def body(bi, carry):
            rows = rows_of(bi)
            o = of_ref[rows_of(base + bi), :] + ob_ref[rows_of(base + bi), :]
            y = (_rms(o) * gain_ref[...]) * jax.nn.silu(r_ref[rows, :].astype(F32))
            out_ref[rows, :] = y.astype(out_ref.dtype)
            return carry
        return body

    sf_ref[...] = jnp.zeros_like(sf_ref)
    sb_ref[...] = jnp.zeros_like(sb_ref)
    lax.fori_loop(0, n_ctx, both(ctx_refs, 0, n_ctx), 0)
    lax.fori_loop(0, n_lat, both(lat_refs, n_ctx, n_lat), 0)
    lax.fori_loop(0, n_ctx, finish(rc_ref, oc_ref, 0), 0)
    lax.fori_loop(0, n_lat, finish(rl_ref, ol_ref, n_ctx), 0)


def _gla(p_ctx, p_lat, wgf, bgf, wgb, bgb, out_gain, *, batch, dk, dv):
    h = GLA_HEADS
    rows_c = p_ctx.shape[0] // batch
    rows_l = p_lat.shape[0] // batch
    assert rows_c % GLA_BLOCK == 0 and rows_l % GLA_BLOCK == 0
    zblk = (2 * h * dk + 2 * h * dv) // LANES
    k0 = h
    v0 = (2 * h * dk) // dv
    r0 = v0 + h

    def stream(rows):
        return [pl.BlockSpec((rows, dk), lambda b, hh: (b, hh)),
                pl.BlockSpec((rows, dk), lambda b, hh: (b, k0 + hh)),
                pl.BlockSpec((rows, dv), lambda b, hh: (b, v0 + hh)),
                pl.BlockSpec((rows, dv), lambda b, hh: (b, r0 + hh)),
                pl.BlockSpec((rows, LANES), lambda b, hh: (b, zblk))]

    wspec = pl.BlockSpec((1, LANES, dk), lambda b, hh: (hh, 0, 0))
    bspec = pl.BlockSpec((1, 1, dk), lambda b, hh: (hh, 0, 0))
    return pl.pallas_call(
        _gla_kernel,
        grid=(batch, h),
        in_specs=stream(rows_c) + stream(rows_l) + [wspec, bspec, wspec, bspec,
                                                    pl.BlockSpec((1, dv), lambda b, hh: (0, 0))],
        out_specs=[pl.BlockSpec((rows_c, dv), lambda b, hh: (b, hh)),
                   pl.BlockSpec((rows_l, dv), lambda b, hh: (b, hh))],
        out_shape=[jax.ShapeDtypeStruct((batch * rows_c, h * dv), BF16),
                   jax.ShapeDtypeStruct((batch * rows_l, h * dv), BF16)],
        scratch_shapes=[pltpu.VMEM((dv, dk), F32), pltpu.VMEM((dv, dk), F32),
                        pltpu.VMEM((rows_c + rows_l, dv), F32), pltpu.VMEM((rows_c + rows_l, dv), F32)],
        compiler_params=_params("parallel", "parallel"),
        name="gla",
    )(*([p_ctx] * 5 + [p_lat] * 5 + [wgf, bgf, wgb, bgb, out_gain.reshape(1, dv)]))


def _dft_tables(n):
    idx = jnp.arange(n, dtype=jnp.int32)
    ang = ((idx[:, None] * idx[None, :]) % n).astype(F32) * (2.0 * math.pi / n)
    return jnp.cos(ang).astype(BF16), jnp.sin(ang).astype(BF16)


def _fnet_chan_kernel(x_ref, gain_ref, shift_ref, scale_ref, csc_ref, p_ref, q_ref, xn_ref):
    gd = csc_ref.shape[0]
    _norm_mod_rows(x_ref, xn_ref, gain_ref[...], shift_ref[0], scale_ref[0])
    for g in range(x_ref.shape[1] // gd):
        cols = slice(g * gd, (g + 1) * gd)
        pq = jnp.dot(xn_ref[:, cols], csc_ref[...], preferred_element_type=F32)
        p_ref[:, cols] = pq[:, :gd].astype(p_ref.dtype)
        q_ref[:, cols] = pq[:, gd:].astype(q_ref.dtype)


def _fnet_seq_kernel(cs_ref, ss_ref, p_ref, q_ref, o_ref, *, inv_norm):
    acc = (jnp.dot(cs_ref[...], p_ref[...], preferred_element_type=F32)
           - jnp.dot(ss_ref[...], q_ref[...], preferred_element_type=F32))
    o_ref[...] = (acc * inv_norm).astype(o_ref.dtype)


def _fnet(x, gain, shift, scale, *, batch, tm):
    t, d = x.shape
    s = t // batch
    gd = d // FNET_GROUPS
    tiles_per_batch = s // tm
    cc, sc = _dft_tables(gd)
    cs, ss = _dft_tables(s)
    vec = pl.BlockSpec((1, 1, d), lambda i: (i // tiles_per_batch, 0, 0))
    blk = pl.BlockSpec((tm, d), lambda i: (i, 0))
    p, q = pl.pallas_call(
        _fnet_chan_kernel,
        grid=(t // tm,),
        in_specs=[blk, pl.BlockSpec((1, d), lambda i: (0, 0)), vec, vec,
                  pl.BlockSpec((gd, 2 * gd), lambda i: (0, 0))],
        out_specs=[blk, blk],
        out_shape=[jax.ShapeDtypeStruct((t, d), BF16)] * 2,
        scratch_shapes=[pltpu.VMEM((tm, d), BF16)],
        compiler_params=_params("parallel"),
        name="fnet_chan",
    )(x, gain.reshape(1, d), shift, scale, jnp.concatenate([cc, sc], axis=1))
    rows = pl.BlockSpec((tm, s), lambda b, j, i: (i, 0))
    cols = pl.BlockSpec((s, gd), lambda b, j, i: (b, j))
    return pl.pallas_call(
        functools.partial(_fnet_seq_kernel, inv_norm=float((s * gd) ** -0.5)),
        grid=(batch, d // gd, tiles_per_batch),
        in_specs=[rows, rows, cols, cols],
        out_specs=pl.BlockSpec((tm, gd), lambda b, j, i: (b * tiles_per_batch + i, j)),
        out_shape=jax.ShapeDtypeStruct((t, d), BF16),
        compiler_params=_params("parallel", "parallel", "arbitrary"),
        name="fnet_seq",
    )(cs, ss, p, q)


def _rope_tables(n_tokens):
    t = jnp.arange(n_tokens)
    row = (t // GRID_W).astype(F32)
    col = (t % GRID_W).astype(F32)
    half = HEAD_DIM // 2
    inv_freq = ROPE_THETA ** (-jnp.arange(0, half, 2, dtype=F32) / half)
    ang_r = row[:, None] * inv_freq[None, :]
    ang_c = col[:, None] * inv_freq[None, :]
    ang = jnp.concatenate([ang_r, ang_r, ang_c, ang_c], axis=-1)
    sign = jnp.concatenate([-jnp.ones((half // 2,), F32), jnp.ones((half // 2,), F32)] * 2)
    return jnp.cos(ang), jnp.sin(ang) * sign


def _column_gains(q_gain, k_gain, q_cols, k_cols, v_cols):
    qg = jnp.tile(q_gain.astype(F32) * (HEAD_DIM ** -0.5 * math.log2(math.e)), q_cols // HEAD_DIM)
    kg = jnp.tile(k_gain.astype(F32), k_cols // HEAD_DIM)
    return jnp.concatenate([qg, kg, jnp.ones((v_cols,), F32)])[None, :]


def _gate_weights(wg, bg, lane0, dk):
    r = wg.shape[0]
    w = wg.reshape(r, GLA_HEADS, dk).transpose(1, 0, 2)
    w = jnp.pad(w, ((0, 0), (lane0, LANES - lane0 - r), (0, 0))).astype(BF16)
    return w, bg.reshape(GLA_HEADS, 1, dk).astype(F32)


def kernel(x, c, ctx, c_ctx, l0_mod_w, l0_mod_b, l0_norm1, l0_gla_w_in, l0_gla_wg_f, l0_gla_bg_f, l0_gla_wg_b, l0_gla_bg_b, l0_gla_out_norm, l0_gla_w_out, l0_norm2, l0_ffn_w_in, l0_ffn_w_out, l1_mod_w, l1_mod_b, l1_norm1, l1_gqa_w_in, l1_gqa_q_norm, l1_gqa_k_norm, l1_gqa_w_out, l1_norm2, l1_ffn_w_in, l1_ffn_w_out, l2_mod_w, l2_mod_b, l2_norm1, l2_diff_w_in, l2_diff_q_norm, l2_diff_k_norm, l2_diff_lq1, l2_diff_lk1, l2_diff_lq2, l2_diff_lk2, l2_diff_out_norm, l2_diff_w_out, l2_norm2, l2_ffn_w_in, l2_ffn_w_out, l3_mod_w, l3_mod_b, l3_norm1, l3_fnet_w_out, l3_norm2, l3_ffn_w_in, l3_ffn_w_out):
    b, s, d = x.shape
    n_ctx = ctx.shape[1]
    xl = x.reshape(b * s, d)
    xc = ctx.reshape(b * n_ctx, d)
    tm = math.gcd(s, 1024)
    tm_c = math.gcd(b * n_ctx, 1024)
    tm_r = math.gcd(s, 512)
    tm_rc = math.gcd(b * n_ctx, 512)
    th = math.gcd(l0_ffn_w_out.shape[0], 512)
    tq_gqa = math.gcd(s, 4 * ATTN_SUBTILE)
    tq_diff = math.gcd(s, 4 * ATTN_SUBTILE)

    n_cond = -(-(b + 1) // BF16_ROWS) * BF16_ROWS
    cond = jnp.concatenate([c, c_ctx[None, :], jnp.zeros((n_cond - b - 1, d), F32)], axis=0)

    def modulation(mod_w, mod_b):
        m = _mm(cond, mod_w, tm=n_cond, tn=512, out_dtype=F32, prologue="silu", epilogue="bias", bias=mod_b,
                name="modulation")
        lat = [m[:b, k * d:(k + 1) * d].reshape(b, 1, d) for k in range(6)]
        cx = [m[b:b + 1, k * d:(k + 1) * d].reshape(1, 1, d) for k in range(6)]
        return lat, cx

    def tiling(rows, resident=False):
        if rows == s:
            return dict(tm=tm_r if resident else tm, rows_per_batch=s)
        return dict(tm=tm_rc if resident else tm_c, rows_per_batch=None)

    def proj(xs, w, n1, sh, sc, rows, tn, name, **kw):
        return _mm(xs, w, tn=tn, out_dtype=BF16, prologue="norm_mod", gain=n1, shift=sh, scale=sc, name=name,
                   **tiling(rows), **kw)

    def qkv_proj(xs, w, n1, sh, sc, col_gain, rows, name, **kw):
        return _qkv_proj(xs, w, n1, sh, sc, col_gain, tn=math.gcd(w.shape[1], 1024), name=name, **tiling(rows), **kw)

    def out_resid(y, w, xs, gate, rows):
        return _mm(y, w.astype(BF16), tn=w.shape[1], out_dtype=F32, epilogue="resid", res=xs, gate=gate,
                   name="out_resid", **tiling(rows, resident=True))

    def ffn(xs, n2, sh, sc, gate, w_in, w_out, rows):
        return _ffn(xs, n2, sh, sc, gate, w_in.astype(BF16), w_out.astype(BF16), th=th, name="ffn", **tiling(rows))

    (sh1, sc1, g1, sh2, sc2, g2), (csh1, csc1, cg1, csh2, csc2, cg2) = modulation(l0_mod_w, l0_mod_b)
    dk = l0_gla_wg_f.shape[1] // GLA_HEADS
    dv = d // GLA_HEADS
    n_in = l0_gla_w_in.shape[1]
    tn0 = 1280
    n_pad = -(-(n_in - 2 * GLA_RANK + LANES) // tn0) * tn0
    w0 = jnp.pad(l0_gla_w_in, ((0, 0), (0, n_pad - n_in))).astype(BF16)
    pl0 = proj(xl, w0, l0_norm1, sh1, sc1, s, tn0, "gla_proj")
    pc0 = proj(xc, w0, l0_norm1, csh1, csc1, n_ctx, tn0, "gla_proj_ctx")
    wgf, bgf = _gate_weights(l0_gla_wg_f, l0_gla_bg_f, 0, dk)
    wgb, bgb = _gate_weights(l0_gla_wg_b, l0_gla_bg_b, GLA_RANK, dk)
    yc, yl = _gla(pc0, pl0, wgf, bgf, wgb, bgb, l0_gla_out_norm, batch=b, dk=dk, dv=dv)
    xl = out_resid(yl, l0_gla_w_out, xl, g1, s)
    xc = out_resid(yc, l0_gla_w_out, xc, cg1, n_ctx)
    xl = ffn(xl, l0_norm2, sh2, sc2, g2, l0_ffn_w_in, l0_ffn_w_out, s)
    xc = ffn(xc, l0_norm2, csh2, csc2, cg2, l0_ffn_w_in, l0_ffn_w_out, n_ctx)

    (sh1, sc1, g1, sh2, sc2, g2), (csh1, csc1, cg1, csh2, csc2, cg2) = modulation(l1_mod_w, l1_mod_b)
    n_heads = d // HEAD_DIM
    cos, sin = _rope_tables(s)
    kv_cols = GQA_KV_HEADS * HEAD_DIM
    cg_gqa = _column_gains(l1_gqa_q_norm, l1_gqa_k_norm, d, kv_cols, kv_cols)
    w1 = l1_gqa_w_in.astype(BF16)
    pl1 = qkv_proj(xl, w1, l1_norm1, sh1, sc1, cg_gqa, s, "gqa_proj", normed_cols=d + kv_cols, cos=cos, sin=sin)
    pc1 = qkv_proj(xc, w1, l1_norm1, csh1, csc1, cg_gqa, n_ctx, "gqa_proj_ctx", normed_cols=d + kv_cols)
    yl = _gqa_attention(pl1, [pl1, pc1], batch=b, n_q_heads=n_heads, n_kv_heads=GQA_KV_HEADS, tq=tq_gqa, q_rows=s,
                        name="gqa_attn")
    yc = _gqa_attention(pc1, [pc1], batch=b, n_q_heads=n_heads, n_kv_heads=GQA_KV_HEADS, tq=n_ctx, q_rows=n_ctx,
                        name="gqa_attn_ctx")
    xl = out_resid(yl, l1_gqa_w_out, xl, g1, s)
    xc = out_resid(yc, l1_gqa_w_out, xc, cg1, n_ctx)
    xl = ffn(xl, l1_norm2, sh2, sc2, g2, l1_ffn_w_in, l1_ffn_w_out, s)
    xc = ffn(xc, l1_norm2, csh2, csc2, cg2, l1_ffn_w_in, l1_ffn_w_out, n_ctx)

    (sh1, sc1, g1, sh2, sc2, g2), (csh1, csc1, _, _, _, _) = modulation(l2_mod_w, l2_mod_b)
    lam_init = 0.8 - 0.6 * math.exp(-0.3 * 2)
    cg_diff = _column_gains(l2_diff_q_norm, l2_diff_k_norm, d, d, d)
    w2 = l2_diff_w_in.astype(BF16)
    pl2 = qkv_proj(xl, w2, l2_norm1, sh1, sc1, cg_diff, s, "diff_proj", normed_cols=2 * d, cos=cos, sin=sin)
    pc2 = qkv_proj(xc, w2, l2_norm1, csh1, csc1, cg_diff, n_ctx, "diff_proj_ctx", normed_cols=2 * d)
    yl = _diff_attention(pl2, [pl2, pc2], (l2_diff_lq1, l2_diff_lk1, l2_diff_lq2, l2_diff_lk2),
                         l2_diff_out_norm, batch=b, n_heads=n_heads // 2, tq=tq_diff, q_rows=s, lam_init=lam_init,
                         name="diff_attn")
    xl = out_resid(yl, l2_diff_w_out, xl, g1, s)
    xl = ffn(xl, l2_norm2, sh2, sc2, g2, l2_ffn_w_in, l2_ffn_w_out, s)

    (sh1, sc1, g1, sh2, sc2, g2), _ = modulation(l3_mod_w, l3_mod_b)
    yl = _fnet(xl, l3_norm1, sh1, sc1, batch=b, tm=tm)
    xl = out_resid(yl, l3_fnet_w_out, xl, g1, s)
    xl = ffn(xl, l3_norm2, sh2, sc2, g2, l3_ffn_w_in, l3_ffn_w_out, s)
    return xl.reshape(b, s, d)
```

```python
import functools
import math

import jax
import jax.numpy as jnp
from jax import lax
from jax.experimental import pallas as pl
from jax.experimental.pallas import tpu as pltpu

F32 = jnp.float32
BF16 = jnp.bfloat16

NORM_EPS = 1e-6
ROPE_THETA = 10000.0
GRID_W = 64
HEAD_DIM = 128
GQA_KV_HEADS = 4
GLA_HEADS = 4
GLA_RANK = 16
GLA_TAU = 16.0
GLA_CHUNK = 64
GLA_BLOCK = 4 * GLA_CHUNK
FNET_GROUPS = 4

LANES = 128
BF16_ROWS = 16
STRIP_UNROLL = 8
VMEM_LIMIT = 56 * 1024 * 1024

NT_DIMS = (((1,), (1,)), ((), ()))
TN_DIMS = (((0,), (0,)), ((), ()))


def _params(*sem):
    return pltpu.CompilerParams(dimension_semantics=sem, vmem_limit_bytes=VMEM_LIMIT)


def _rms(x, eps=NORM_EPS):
    return x * lax.rsqrt(jnp.mean(x * x, axis=-1, keepdims=True) + eps)


def _norm_mod_rows(x_ref, xn_ref, gain, shift, scale):
    mult = gain * (1.0 + scale)

    def strip(r, carry):
        rows = pl.ds(pl.multiple_of(r * BF16_ROWS, BF16_ROWS), BF16_ROWS)
        xn_ref[rows, :] = (_rms(x_ref[rows, :]) * mult + shift).astype(BF16)
        return carry

    lax.fori_loop(0, x_ref.shape[0] // BF16_ROWS, strip, 0, unroll=STRIP_UNROLL)


ROPE_STRIP = 256
MXU_WIDTH = 256


def _head_group_matrices():
    src = lax.broadcasted_iota(jnp.int32, (MXU_WIDTH, MXU_WIDTH), 0)
    dst = lax.broadcasted_iota(jnp.int32, (MXU_WIDTH, MXU_WIDTH), 1)
    quarter = HEAD_DIM // 4
    partner = jnp.where((dst // quarter) % 2 == 0, dst + quarter, dst - quarter)
    mean = jnp.where(src // HEAD_DIM == dst // HEAD_DIM, 1.0 / HEAD_DIM, 0.0).astype(BF16)
    return mean, (src == partner).astype(BF16)


def _mm_kernel(*refs, prologue, epilogue):
    it = iter(refs)
    x_ref = next(it)
    if prologue == "norm_mod":
        gain_ref, shift_ref, scale_ref = next(it), next(it), next(it)
    w_ref = next(it)
    if epilogue == "bias":
        b_ref = next(it)
    elif epilogue == "resid":
        res_ref, gate_ref = next(it), next(it)
    o_ref = next(it)
    if prologue != "cast":
        xn_ref = next(it)
    j = pl.program_id(1)

    if prologue == "cast":
        a = x_ref[...].astype(BF16)
    else:
        @pl.when(j == 0)
        def _():
            if prologue == "norm_mod":
                _norm_mod_rows(x_ref, xn_ref, gain_ref[...], shift_ref[0], scale_ref[0])
            else:
                xn_ref[...] = jax.nn.silu(x_ref[...]).astype(BF16)

        a = xn_ref[...]
    acc = jnp.dot(a, w_ref[...].astype(BF16), preferred_element_type=F32)
    if epilogue == "store":
        o_ref[...] = acc.astype(o_ref.dtype)
    elif epilogue == "bias":
        o_ref[...] = (acc + b_ref[...]).astype(o_ref.dtype)
    elif epilogue == "resid":
        o_ref[...] = (res_ref[...] + gate_ref[0] * acc).astype(o_ref.dtype)


def _mm(x, w, *, tm, tn, out_dtype, name, prologue="cast", epilogue="store", rows_per_batch=None,
        gain=None, shift=None, scale=None, bias=None, res=None, gate=None):
    t, k = x.shape
    n = w.shape[1]
    assert t % tm == 0 and n % tn == 0, (t, tm, n, tn)
    rows_per_batch = rows_per_batch or t
    assert rows_per_batch % tm == 0
    tiles_per_batch = rows_per_batch // tm
    once = dict(pipeline_mode=pl.Buffered(1))

    def batch_of(i):
        return i // tiles_per_batch

    args = [x]
    specs = [pl.BlockSpec((tm, k), lambda i, j: (i, 0))]
    if prologue == "norm_mod":
        nb = shift.shape[0]
        bsel = (lambda i: batch_of(i)) if nb > 1 else (lambda i: 0)
        args += [gain.reshape(1, k), shift, scale]
        specs += [pl.BlockSpec((1, k), lambda i, j: (0, 0)),
                  pl.BlockSpec((1, 1, k), lambda i, j: (bsel(i), 0, 0)),
                  pl.BlockSpec((1, 1, k), lambda i, j: (bsel(i), 0, 0))]
    args.append(w)
    specs.append(pl.BlockSpec((k, tn), lambda i, j: (0, j), **(once if tn == n else {})))
    if epilogue == "bias":
        args.append(bias.reshape(1, n))
        specs.append(pl.BlockSpec((1, tn), lambda i, j: (0, j)))
    elif epilogue == "resid":
        nb = gate.shape[0]
        gsel = (lambda i: batch_of(i)) if nb > 1 else (lambda i: 0)
        args += [res, gate]
        specs += [pl.BlockSpec((tm, tn), lambda i, j: (i, j)),
                  pl.BlockSpec((1, 1, tn), lambda i, j: (gsel(i), 0, j))]
    scratch = []
    if prologue != "cast":
        scratch.append(pltpu.VMEM((tm, k), BF16))
    kern = functools.partial(_mm_kernel, prologue=prologue, epilogue=epilogue)
    return pl.pallas_call(
        kern,
        grid=(t // tm, n // tn),
        in_specs=specs,
        out_specs=pl.BlockSpec((tm, tn), lambda i, j: (i, j)),
        out_shape=jax.ShapeDtypeStruct((t, n), out_dtype),
        scratch_shapes=scratch,
        compiler_params=_params("parallel", "arbitrary"),
        name=name,
    )(*args)


def _qkv_kernel(*refs, normed_cols, rope):
    it = iter(refs)
    x_ref, gain_ref, shift_ref, scale_ref, w_ref, hg_ref, ones_ref = [next(it) for _ in range(7)]
    if rope:
        perm_ref, cos_ref, sin_ref = next(it), next(it), next(it)
    o_ref, xn_ref = next(it), next(it)
    j = pl.program_id(1)
    tm, tn = o_ref.shape
    gw = ones_ref.shape[0]
    n_sub = tn // gw
    full_tiles, rem = divmod(normed_cols, tn)

    @pl.when(j == 0)
    def _():
        _norm_mod_rows(x_ref, xn_ref, gain_ref[...], shift_ref[0], scale_ref[0])

    def norm_rope_store(acc, cols):
        for r0 in range(0, tm, ROPE_STRIP):
            rows = slice(r0, r0 + ROPE_STRIP)
            blk = acc[rows, :]
            ms = jnp.dot((blk * blk).astype(BF16), ones_ref[...], preferred_element_type=F32)
            y = blk * lax.rsqrt(ms + NORM_EPS) * hg_ref[:, cols]
            if rope:
                partner = jnp.dot(y.astype(BF16), perm_ref[...], preferred_element_type=F32)
                reps = gw // HEAD_DIM
                y = (y * jnp.concatenate([cos_ref[rows, :]] * reps, axis=-1)
                     + partner * jnp.concatenate([sin_ref[rows, :]] * reps, axis=-1))
            o_ref[rows, cols] = y.astype(o_ref.dtype)

    def tile(n_normed_subs):
        a = xn_ref[...]
        if n_normed_subs == 0:
            o_ref[...] = jnp.dot(a, w_ref[...], preferred_element_type=F32).astype(o_ref.dtype)
            return
        sub_cols = [slice(s * gw, (s + 1) * gw) for s in range(n_sub)]
        acc = jnp.dot(a, w_ref[:, sub_cols[0]], preferred_element_type=F32)
        for s in range(n_sub):
            nxt = jnp.dot(a, w_ref[:, sub_cols[s + 1]], preferred_element_type=F32) if s + 1 < n_sub else None
            if s < n_normed_subs:
                norm_rope_store(acc, sub_cols[s])
            else:
                o_ref[:, sub_cols[s]] = acc.astype(o_ref.dtype)
            acc = nxt

    if full_tiles:
        pl.when(j < full_tiles)(functools.partial(tile, n_sub))
    if rem:
        pl.when(j == full_tiles)(functools.partial(tile, rem // gw))
    pl.when(j >= full_tiles + (1 if rem else 0))(functools.partial(tile, 0))


def _qkv_proj(x, w, gain, shift, scale, col_gain, *, normed_cols, tm, tn, name, rows_per_batch=None,
              cos=None, sin=None):
    t, k = x.shape
    n = w.shape[1]
    assert t % tm == 0 and n % tn == 0 and normed_cols % MXU_WIDTH == 0
    rows_per_batch = rows_per_batch or t
    assert rows_per_batch % tm == 0
    tiles_per_batch = rows_per_batch // tm
    rope = cos is not None
    nb = shift.shape[0]
    bsel = (lambda i: i // tiles_per_batch) if nb > 1 else (lambda i: 0)
    ones, perm = _head_group_matrices()
    const = pl.BlockSpec(ones.shape, lambda i, j: (0, 0))
    vec = pl.BlockSpec((1, 1, k), lambda i, j: (bsel(i), 0, 0))
    args = [x, gain.reshape(1, k), shift, scale, w, col_gain, ones]
    specs = [pl.BlockSpec((tm, k), lambda i, j: (i, 0)),
             pl.BlockSpec((1, k), lambda i, j: (0, 0)), vec, vec,
             pl.BlockSpec((k, tn), lambda i, j: (0, j)),
             pl.BlockSpec((1, tn), lambda i, j: (0, j)), const]
    if rope:
        table = pl.BlockSpec((tm, HEAD_DIM), lambda i, j: (i % tiles_per_batch, 0))
        args += [perm, cos, sin]
        specs += [const, table, table]
    return pl.pallas_call(
        functools.partial(_qkv_kernel, normed_cols=normed_cols, rope=rope),
        grid=(t // tm, n // tn),
        in_specs=specs,
        out_specs=pl.BlockSpec((tm, tn), lambda i, j: (i, j)),
        out_shape=jax.ShapeDtypeStruct((t, n), BF16),
        scratch_shapes=[pltpu.VMEM((tm, k), BF16)],
        compiler_params=_params("parallel", "arbitrary"),
        name=name,
    )(*args)


FFN_HID_SPLIT = 2


def _ffn_kernel(x_ref, gain_ref, shift_ref, scale_ref, gate_ref, wg_ref, wu_ref, wo_ref, o_ref, xn_ref):
    j = pl.program_id(1)
    hc = wg_ref.shape[1] // FFN_HID_SPLIT
    cols = [slice(c * hc, (c + 1) * hc) for c in range(FFN_HID_SPLIT)]

    def chunk(first, last):
        if first:
            _norm_mod_rows(x_ref, xn_ref, gain_ref[...], shift_ref[0], scale_ref[0])
        xn = xn_ref[...]
        gu = [(jnp.dot(xn, wg_ref[:, cs], preferred_element_type=F32),
               jnp.dot(xn, wu_ref[:, cs], preferred_element_type=F32)) for cs in cols]
        for c, (g, u) in enumerate(gu):
            a = (jax.nn.silu(g) * u).astype(BF16)
            part = jnp.dot(a, wo_ref[cols[c], :], preferred_element_type=F32)
            if first and c == 0:
                o_ref[...] = part
            elif last and c == len(gu) - 1:
                o_ref[...] = x_ref[...] + gate_ref[0] * (o_ref[...] + part)
            else:
                o_ref[...] += part

    n_chunks = pl.num_programs(1)
    pl.when(j == 0)(functools.partial(chunk, True, False))
    pl.when((j > 0) & (j < n_chunks - 1))(functools.partial(chunk, False, False))
    pl.when(j == n_chunks - 1)(functools.partial(chunk, False, True))


def _ffn(x, gain, shift, scale, gate, w_in, w_out, *, tm, th, name, rows_per_batch=None):
    t, d = x.shape
    hidden = w_out.shape[0]
    assert t % tm == 0 and hidden % th == 0 and hidden // th >= 2
    nh = hidden // th
    rows_per_batch = rows_per_batch or t
    tiles_per_batch = rows_per_batch // tm
    nb = shift.shape[0]
    bsel = (lambda i: i // tiles_per_batch) if nb > 1 else (lambda i: 0)
    vec = pl.BlockSpec((1, 1, d), lambda i, j: (bsel(i), 0, 0))
    return pl.pallas_call(
        _ffn_kernel,
        grid=(t // tm, nh),
        in_specs=[pl.BlockSpec((tm, d), lambda i, j: (i, 0)),
                  pl.BlockSpec((1, d), lambda i, j: (0, 0)),
                  vec, vec, vec,
                  pl.BlockSpec((d, th), lambda i, j: (0, j)),
                  pl.BlockSpec((d, th), lambda i, j: (0, nh + j)),
                  pl.BlockSpec((th, d), lambda i, j: (j, 0))],
        out_specs=pl.BlockSpec((tm, d), lambda i, j: (i, 0)),
        out_shape=jax.ShapeDtypeStruct((t, d), F32),
        scratch_shapes=[pltpu.VMEM((tm, d), BF16)],
        compiler_params=_params("parallel", "arbitrary"),
        name=name,
    )(x, gain.reshape(1, d), shift, scale, gate, w_in, w_in, w_out)


def _fill_kv(k_s, v_s, kv_refs):
    off = 0
    for k_ref, v_ref in kv_refs:
        n = k_ref.shape[0]
        k_s[off:off + n, :] = k_ref[...]
        v_s[off:off + n, :] = v_ref[...]
        off += n


ATTN_SUBTILE = 256


def _gqa_kernel(*refs, n_kv_src, n_group):
    q_ref = refs[0]
    kv_refs = [(refs[1 + 2 * s], refs[2 + 2 * s]) for s in range(n_kv_src)]
    o_ref, k_s, v_s = refs[1 + 2 * n_kv_src:]

    @pl.when(pl.program_id(2) == 0)
    def _():
        _fill_kv(k_s, v_s, kv_refs)

    k = k_s[...]
    v = v_s[...]
    sub = min(ATTN_SUBTILE, q_ref.shape[0])
    chains = [(slice(r0, r0 + sub), slice(g * HEAD_DIM, (g + 1) * HEAD_DIM))
              for r0 in range(0, q_ref.shape[0], sub) for g in range(n_group)]

    def scores(c):
        return lax.dot_general(q_ref[chains[c]], k, NT_DIMS, preferred_element_type=F32)

    s = scores(0)
    for c in range(len(chains)):
        s_next = scores(c + 1) if c + 1 < len(chains) else None
        p = jnp.exp2(s - jnp.max(s, axis=-1, keepdims=True))
        l = jnp.sum(p, axis=-1, keepdims=True)
        o = jnp.dot(p.astype(BF16), v, preferred_element_type=F32)
        o_ref[chains[c]] = (o / l).astype(o_ref.dtype)
        s = s_next


def _gqa_attention(q_src, kv_srcs, *, batch, n_q_heads, n_kv_heads, tq, q_rows, name):
    group = n_q_heads // n_kv_heads
    gw = group * HEAD_DIM
    nq = q_rows // tq
    k_blk0 = n_q_heads
    v_blk0 = n_q_heads + n_kv_heads
    specs = [pl.BlockSpec((tq, gw), lambda b, h, i: (b * nq + i, h))]
    args = [q_src]
    total = 0
    for src in kv_srcs:
        rows = src.shape[0] // batch
        total += rows
        specs += [pl.BlockSpec((rows, HEAD_DIM), lambda b, h, i: (b, k_blk0 + h)),
                  pl.BlockSpec((rows, HEAD_DIM), lambda b, h, i: (b, v_blk0 + h))]
        args += [src, src]
    kern = functools.partial(_gqa_kernel, n_kv_src=len(kv_srcs), n_group=group)
    return pl.pallas_call(
        kern,
        grid=(batch, n_kv_heads, nq),
        in_specs=specs,
        out_specs=pl.BlockSpec((tq, gw), lambda b, h, i: (b * nq + i, h)),
        out_shape=jax.ShapeDtypeStruct((batch * q_rows, n_q_heads * HEAD_DIM), BF16),
        scratch_shapes=[pltpu.VMEM((total, HEAD_DIM), BF16), pltpu.VMEM((total, HEAD_DIM), BF16)],
        compiler_params=_params("parallel", "parallel", "arbitrary"),
        name=name,
    )(*args)


def _diff_kernel(*refs, n_kv_src, lam_init):
    q_ref = refs[0]
    kv_refs = [(refs[1 + 2 * s], refs[2 + 2 * s]) for s in range(n_kv_src)]
    lq1, lk1, lq2, lk2, gain_ref, o_ref, k_s, v_s = refs[1 + 2 * n_kv_src:]

    @pl.when(pl.program_id(2) == 0)
    def _():
        _fill_kv(k_s, v_s, kv_refs)

    lam = (jnp.exp(jnp.sum(lq1[...] * lk1[...], axis=-1, keepdims=True))
           - jnp.exp(jnp.sum(lq2[...] * lk2[...], axis=-1, keepdims=True)) + lam_init)
    dh = HEAD_DIM
    sub = min(ATTN_SUBTILE, q_ref.shape[0])
    chains = [slice(r0, r0 + sub) for r0 in range(0, q_ref.shape[0], sub)]

    def scores(c):
        return [lax.dot_general(q_ref[chains[c], m * dh:(m + 1) * dh], k_s[:, m * dh:(m + 1) * dh], NT_DIMS,
                                preferred_element_type=F32) for m in range(2)]

    s = scores(0)
    for c in range(len(chains)):
        s_next = scores(c + 1) if c + 1 < len(chains) else None
        p = [jnp.exp2(sm - jnp.max(sm, axis=-1, keepdims=True)) for sm in s]
        l0, l1 = [jnp.sum(pm, axis=-1, keepdims=True) for pm in p]
        w = p[0] - p[1] * (lam * l0 / l1)
        o = jnp.dot(w.astype(BF16), v_s[...], preferred_element_type=F32) * (1.0 / l0)
        o_ref[chains[c], :] = ((_rms(o) * gain_ref[...]) * (1.0 - lam_init)).astype(o_ref.dtype)
        s = s_next


def _diff_attention(q_src, kv_srcs, lams, out_gain, *, batch, n_heads, tq, q_rows, lam_init, name):
    hw = 2 * HEAD_DIM
    nq = q_rows // tq
    specs = [pl.BlockSpec((tq, hw), lambda b, h, i: (b * nq + i, h))]
    args = [q_src]
    total = 0
    for src in kv_srcs:
        rows = src.shape[0] // batch
        total += rows
        specs += [pl.BlockSpec((rows, hw), lambda b, h, i: (b, n_heads + h)),
                  pl.BlockSpec((rows, hw), lambda b, h, i: (b, 2 * n_heads + h))]
        args += [src, src]
    small = pl.BlockSpec((1, HEAD_DIM), lambda b, h, i: (0, 0))
    specs += [small] * 4 + [pl.BlockSpec((1, hw), lambda b, h, i: (0, 0))]
    args += [v.reshape(1, HEAD_DIM) for v in lams] + [out_gain.reshape(1, hw)]
    kern = functools.partial(_diff_kernel, n_kv_src=len(kv_srcs), lam_init=lam_init)
    return pl.pallas_call(
        kern,
        grid=(batch, n_heads, nq),
        in_specs=specs,
        out_specs=pl.BlockSpec((tq, hw), lambda b, h, i: (b * nq + i, h)),
        out_shape=jax.ShapeDtypeStruct((batch * q_rows, n_heads * hw), BF16),
        scratch_shapes=[pltpu.VMEM((total, hw), BF16), pltpu.VMEM((total, hw), BF16)],
        compiler_params=_params("parallel", "parallel", "arbitrary"),
        name=name,
    )(*args)


def _gla_blocks(blocks):
    r, dk = blocks[0][0].shape
    c = GLA_CHUNK
    nc = r // c
    n = range(len(blocks))
    qs_, ks_, vs_, zs_, wgs, bgs, st_refs, revs = zip(*blocks)
    row = lax.broadcasted_iota(jnp.int32, (r, r), 0)
    col = lax.broadcasted_iota(jnp.int32, (r, r), 1)
    same = row // c == col // c
    tri_f = [jnp.where(same, jnp.where((col >= row) if revs[i] else (col <= row), 1.0, 0.0), 0.0) for i in n]
    tri = [t.astype(BF16) for t in tri_f]
    mid = [c // 2 if revs[i] else c // 2 - 1 for i in n]
    last = [0 if revs[i] else c - 1 for i in n]

    g = [jax.nn.log_sigmoid(jnp.dot(zs_[i], wgs[i], preferred_element_type=F32) + bgs[i]) / GLA_TAU for i in n]
    g_hi = [g[i].astype(BF16) for i in n]
    g_lo = [(g[i] - g_hi[i].astype(F32)).astype(BF16) for i in n]
    cum = [jnp.dot(tri[i], g_hi[i], preferred_element_type=F32) + jnp.dot(tri[i], g_lo[i], preferred_element_type=F32)
           for i in n]

    def chunk_row(x, idx):
        return jnp.concatenate(
            [jnp.broadcast_to(x[ci * c + idx:ci * c + idx + 1, :], (c, dk)) for ci in range(nc)], axis=0)

    cum_mid = [chunk_row(cum[i], mid[i]) for i in n]
    cum_last = [chunk_row(cum[i], last[i]) for i in n]
    qs = [(qs_[i] * jnp.exp(cum[i] - cum_mid[i])).astype(BF16) for i in n]
    ks = [(ks_[i] * jnp.exp(cum_mid[i] - cum[i])).astype(BF16) for i in n]
    a = [lax.dot_general(qs[i], ks[i], NT_DIMS, preferred_element_type=F32) for i in n]
    a = [jnp.where(tri_f[i] > 0.5, a[i], 0.0).astype(BF16) for i in n]
    o_intra = [jnp.dot(a[i], vs_[i], preferred_element_type=F32) for i in n]
    q_inter = [(qs_[i] * jnp.exp(cum[i])).astype(BF16) for i in n]
    k_carry = [(ks_[i] * jnp.exp(cum_last[i] - cum[i])).astype(BF16) for i in n]
    st = [st_refs[i][...] for i in n]
    o_inter = [[None] * nc for _ in n]
    for step in range(nc):
        for i in n:
            ci = nc - 1 - step if revs[i] else step
            rows = slice(ci * c, (ci + 1) * c)
            o_inter[i][ci] = lax.dot_general(q_inter[i][rows], st[i].astype(BF16), NT_DIMS,
                                             preferred_element_type=F32)
            decay = jnp.exp(cum[i][ci * c + last[i]:ci * c + last[i] + 1, :])
            st[i] = st[i] * decay + lax.dot_general(vs_[i][rows], k_carry[i][rows], TN_DIMS,
                                                    preferred_element_type=F32)
    for i in n:
        st_refs[i][...] = st[i]
    return [o_intra[i] + jnp.concatenate(o_inter[i], axis=0) for i in n]


def _gla_kernel(qc_ref, kc_ref, vc_ref, rc_ref, zc_ref, ql_ref, kl_ref, vl_ref, rl_ref, zl_ref,
                wgf_ref, bgf_ref, wgb_ref, bgb_ref, gain_ref, oc_ref, ol_ref, sf_ref, sb_ref, of_ref, ob_ref):
    blk = GLA_BLOCK
    n_ctx = qc_ref.shape[0] // blk
    n_lat = ql_ref.shape[0] // blk
    q_scale = qc_ref.shape[1] ** -0.5
    ctx_refs = (qc_ref, kc_ref, vc_ref, zc_ref)
    lat_refs = (ql_ref, kl_ref, vl_ref, zl_ref)

    def rows_of(bi):
        return pl.ds(pl.multiple_of(bi * blk, blk), blk)

    def block(refs, bi, reverse):
        q_ref, k_ref, v_ref, z_ref = refs
        rows = rows_of(bi)
        wg_ref, bg_ref, st_ref = (wgb_ref, bgb_ref, sb_ref) if reverse else (wgf_ref, bgf_ref, sf_ref)
        return (q_ref[rows, :].astype(F32) * q_scale, k_ref[rows, :].astype(F32), v_ref[rows, :], z_ref[rows, :],
                wg_ref[0], bg_ref[0], st_ref, reverse)

    def both(refs, base, n):
        def body(t, carry):
            o_f, o_b = _gla_blocks([block(refs, t, False), block(refs, n - 1 - t, True)])
            of_ref[rows_of(base + t), :] = o_f
            ob_ref[rows_of(base + n - 1 - t), :] = o_b
            return carry
        return body

    def finish(r_ref, out_ref, base):
        def body(bi, carry):
            rows = rows_of(bi)
            o = of_ref[rows_of(base + bi), :] + ob_ref[rows_of(base + bi), :]
            y = (_rms(o) * gain_ref[...]) * jax.nn.silu(r_ref[rows, :].astype(F32))
            out_ref[rows, :] = y.astype(out_ref.dtype)
            return carry
        return body

    sf_ref[...] = jnp.zeros_like(sf_ref)
    sb_ref[...] = jnp.zeros_like(sb_ref)
    lax.fori_loop(0, n_ctx, both(ctx_refs, 0, n_ctx), 0)
    lax.fori_loop(0, n_lat, both(lat_refs, n_ctx, n_lat), 0)
    lax.fori_loop(0, n_ctx, finish(rc_ref, oc_ref, 0), 0)
    lax.fori_loop(0, n_lat, finish(rl_ref, ol_ref, n_ctx), 0)


def _gla(p_ctx, p_lat, wgf, bgf, wgb, bgb, out_gain, *, batch, dk, dv):
    h = GLA_HEADS
    rows_c = p_ctx.shape[0] // batch
    rows_l = p_lat.shape[0] // batch
    assert rows_c % GLA_BLOCK == 0 and rows_l % GLA_BLOCK == 0
    zblk = (2 * h * dk + 2 * h * dv) // LANES
    k0 = h
    v0 = (2 * h * dk) // dv
    r0 = v0 + h

    def stream(rows):
        return [pl.BlockSpec((rows, dk), lambda b, hh: (b, hh)),
                pl.BlockSpec((rows, dk), lambda b, hh: (b, k0 + hh)),
                pl.BlockSpec((rows, dv), lambda b, hh: (b, v0 + hh)),
                pl.BlockSpec((rows, dv), lambda b, hh: (b, r0 + hh)),
                pl.BlockSpec((rows, LANES), lambda b, hh: (b, zblk))]

    wspec = pl.BlockSpec((1, LANES, dk), lambda b, hh: (hh, 0, 0))
    bspec = pl.BlockSpec((1, 1, dk), lambda b, hh: (hh, 0, 0))
    return pl.pallas_call(
        _gla_kernel,
        grid=(batch, h),
        in_specs=stream(rows_c) + stream(rows_l) + [wspec, bspec, wspec, bspec,
                                                    pl.BlockSpec((1, dv), lambda b, hh: (0, 0))],
        out_specs=[pl.BlockSpec((rows_c, dv), lambda b, hh: (b, hh)),
                   pl.BlockSpec((rows_l, dv), lambda b, hh: (b, hh))],
        out_shape=[jax.ShapeDtypeStruct((batch * rows_c, h * dv), BF16),
                   jax.ShapeDtypeStruct((batch * rows_l, h * dv), BF16)],
        scratch_shapes=[pltpu.VMEM((dv, dk), F32), pltpu.VMEM((dv, dk), F32),
                        pltpu.VMEM((rows_c + rows_l, dv), F32), pltpu.VMEM((rows_c + rows_l, dv), F32)],
        compiler_params=_params("parallel", "parallel"),
        name="gla",
    )(*([p_ctx] * 5 + [p_lat] * 5 + [wgf, bgf, wgb, bgb, out_gain.reshape(1, dv)]))


def _dft_tables(n):
    idx = jnp.arange(n, dtype=jnp.int32)
    ang = ((idx[:, None] * idx[None, :]) % n).astype(F32) * (2.0 * math.pi / n)
    return jnp.cos(ang).astype(BF16), jnp.sin(ang).astype(BF16)


def _fnet_chan_kernel(x_ref, gain_ref, shift_ref, scale_ref, csc_ref, p_ref, q_ref, xn_ref):
    gd = csc_ref.shape[0]
    _norm_mod_rows(x_ref, xn_ref, gain_ref[...], shift_ref[0], scale_ref[0])
    for g in range(x_ref.shape[1] // gd):
        cols = slice(g * gd, (g + 1) * gd)
        pq = jnp.dot(xn_ref[:, cols], csc_ref[...], preferred_element_type=F32)
        p_ref[:, cols] = pq[:, :gd].astype(p_ref.dtype)
        q_ref[:, cols] = pq[:, gd:].astype(q_ref.dtype)


def _fnet_seq_kernel(cs_ref, ss_ref, p_ref, q_ref, o_ref, *, inv_norm):
    acc = (jnp.dot(cs_ref[...], p_ref[...], preferred_element_type=F32)
           - jnp.dot(ss_ref[...], q_ref[...], preferred_element_type=F32))
    o_ref[...] = (acc * inv_norm).astype(o_ref.dtype)


def _fnet(x, gain, shift, scale, *, batch, tm):
    t, d = x.shape
    s = t // batch
    gd = d // FNET_GROUPS
    tiles_per_batch = s // tm
    cc, sc = _dft_tables(gd)
    cs, ss = _dft_tables(s)
    vec = pl.BlockSpec((1, 1, d), lambda i: (i // tiles_per_batch, 0, 0))
    blk = pl.BlockSpec((tm, d), lambda i: (i, 0))
    p, q = pl.pallas_call(
        _fnet_chan_kernel,
        grid=(t // tm,),
        in_specs=[blk, pl.BlockSpec((1, d), lambda i: (0, 0)), vec, vec,
                  pl.BlockSpec((gd, 2 * gd), lambda i: (0, 0))],
        out_specs=[blk, blk],
        out_shape=[jax.ShapeDtypeStruct((t, d), BF16)] * 2,
        scratch_shapes=[pltpu.VMEM((tm, d), BF16)],
        compiler_params=_params("parallel"),
        name="fnet_chan",
    )(x, gain.reshape(1, d), shift, scale, jnp.concatenate([cc, sc], axis=1))
    rows = pl.BlockSpec((tm, s), lambda b, j, i: (i, 0))
    cols = pl.BlockSpec((s, gd), lambda b, j, i: (b, j))
    return pl.pallas_call(
        functools.partial(_fnet_seq_kernel, inv_norm=float((s * gd) ** -0.5)),
        grid=(batch, d // gd, tiles_per_batch),
        in_specs=[rows, rows, cols, cols],
        out_specs=pl.BlockSpec((tm, gd), lambda b, j, i: (b * tiles_per_batch + i, j)),
        out_shape=jax.ShapeDtypeStruct((t, d), BF16),
        compiler_params=_params("parallel", "parallel", "arbitrary"),
        name="fnet_seq",
    )(cs, ss, p, q)


def _rope_tables(n_tokens):
    t = jnp.arange(n_tokens)
    row = (t // GRID_W).astype(F32)
    col = (t % GRID_W).astype(F32)
    half = HEAD_DIM // 2
    inv_freq = ROPE_THETA ** (-jnp.arange(0, half, 2, dtype=F32) / half)
    ang_r = row[:, None] * inv_freq[None, :]
    ang_c = col[:, None] * inv_freq[None, :]
    ang = jnp.concatenate([ang_r, ang_r, ang_c, ang_c], axis=-1)
    sign = jnp.concatenate([-jnp.ones((half // 2,), F32), jnp.ones((half // 2,), F32)] * 2)
    return jnp.cos(ang), jnp.sin(ang) * sign


def _column_gains(q_gain, k_gain, q_cols, k_cols, v_cols):
    qg = jnp.tile(q_gain.astype(F32) * (HEAD_DIM ** -0.5 * math.log2(math.e)), q_cols // HEAD_DIM)
    kg = jnp.tile(k_gain.astype(F32), k_cols // HEAD_DIM)
    return jnp.concatenate([qg, kg, jnp.ones((v_cols,), F32)])[None, :]


def _gate_weights(wg, bg, lane0, dk):
    r = wg.shape[0]
    w = wg.reshape(r, GLA_HEADS, dk).transpose(1, 0, 2)
    w = jnp.pad(w, ((0, 0), (lane0, LANES - lane0 - r), (0, 0))).astype(BF16)
    return w, bg.reshape(GLA_HEADS, 1, dk).astype(F32)


def kernel(x, c, ctx, c_ctx, l0_mod_w, l0_mod_b, l0_norm1, l0_gla_w_in, l0_gla_wg_f, l0_gla_bg_f, l0_gla_wg_b, l0_gla_bg_b, l0_gla_out_norm, l0_gla_w_out, l0_norm2, l0_ffn_w_in, l0_ffn_w_out, l1_mod_w, l1_mod_b, l1_norm1, l1_gqa_w_in, l1_gqa_q_norm, l1_gqa_k_norm, l1_gqa_w_out, l1_norm2, l1_ffn_w_in, l1_ffn_w_out, l2_mod_w, l2_mod_b, l2_norm1, l2_diff_w_in, l2_diff_q_norm, l2_diff_k_norm, l2_diff_lq1, l2_diff_lk1, l2_diff_lq2, l2_diff_lk2, l2_diff_out_norm, l2_diff_w_out, l2_norm2, l2_ffn_w_in, l2_ffn_w_out, l3_mod_w, l3_mod_b, l3_norm1, l3_fnet_w_out, l3_norm2, l3_ffn_w_in, l3_ffn_w_out):
    b, s, d = x.shape
    n_ctx = ctx.shape[1]
    xl = x.reshape(b * s, d)
    xc = ctx.reshape(b * n_ctx, d)
    tm = math.gcd(s, 1024)
    tm_c = math.gcd(b * n_ctx, 1024)
    tm_r = math.gcd(s, 512)
    tm_rc = math.gcd(b * n_ctx, 512)
    th = math.gcd(l0_ffn_w_out.shape[0], 512)
    tq_gqa = math.gcd(s, 4 * ATTN_SUBTILE)
    tq_diff = math.gcd(s, 4 * ATTN_SUBTILE)

    n_cond = -(-(b + 1) // BF16_ROWS) * BF16_ROWS
    cond = jnp.concatenate([c, c_ctx[None, :], jnp.zeros((n_cond - b - 1, d), F32)], axis=0)

    def modulation(mod_w, mod_b):
        m = _mm(cond, mod_w, tm=n_cond, tn=math.gcd(mod_w.shape[1], 1024), out_dtype=F32, prologue="silu", epilogue="bias", bias=mod_b,
                name="modulation")
        lat = [m[:b, k * d:(k + 1) * d].reshape(b, 1, d) for k in range(6)]
        cx = [m[b:b + 1, k * d:(k + 1) * d].reshape(1, 1, d) for k in range(6)]
        return lat, cx

    def tiling(rows, resident=False):
        if rows == s:
            return dict(tm=tm_r if resident else tm, rows_per_batch=s)
        return dict(tm=tm_rc if resident else tm_c, rows_per_batch=None)

    def proj(xs, w, n1, sh, sc, rows, tn, name, **kw):
        return _mm(xs, w, tn=tn, out_dtype=BF16, prologue="norm_mod", gain=n1, shift=sh, scale=sc, name=name,
                   **tiling(rows), **kw)

    def qkv_proj(xs, w, n1, sh, sc, col_gain, rows, name, **kw):
        return _qkv_proj(xs, w, n1, sh, sc, col_gain, tn=math.gcd(w.shape[1], 1024), name=name, **tiling(rows), **kw)

    def out_resid(y, w, xs, gate, rows):
        return _mm(y, w.astype(BF16), tn=w.shape[1], out_dtype=F32, epilogue="resid", res=xs, gate=gate,
                   name="out_resid", **tiling(rows, resident=True))

    def ffn(xs, n2, sh, sc, gate, w_in, w_out, rows):
        return _ffn(xs, n2, sh, sc, gate, w_in.astype(BF16), w_out.astype(BF16), th=th, name="ffn", **tiling(rows))

    (sh1, sc1, g1, sh2, sc2, g2), (csh1, csc1, cg1, csh2, csc2, cg2) = modulation(l0_mod_w, l0_mod_b)
    dk = l0_gla_wg_f.shape[1] // GLA_HEADS
    dv = d // GLA_HEADS
    n_in = l0_gla_w_in.shape[1]
    tn0 = 1280
    n_pad = -(-(n_in - 2 * GLA_RANK + LANES) // tn0) * tn0
    w0 = jnp.pad(l0_gla_w_in, ((0, 0), (0, n_pad - n_in))).astype(BF16)
    pl0 = proj(xl, w0, l0_norm1, sh1, sc1, s, tn0, "gla_proj")
    pc0 = proj(xc, w0, l0_norm1, csh1, csc1, n_ctx, tn0, "gla_proj_ctx")
    wgf, bgf = _gate_weights(l0_gla_wg_f, l0_gla_bg_f, 0, dk)
    wgb, bgb = _gate_weights(l0_gla_wg_b, l0_gla_bg_b, GLA_RANK, dk)
    yc, yl = _gla(pc0, pl0, wgf, bgf, wgb, bgb, l0_gla_out_norm, batch=b, dk=dk, dv=dv)
    xl = out_resid(yl, l0_gla_w_out, xl, g1, s)
    xc = out_resid(yc, l0_gla_w_out, xc, cg1, n_ctx)
    xl = ffn(xl, l0_norm2, sh2, sc2, g2, l0_ffn_w_in, l0_ffn_w_out, s)
    xc = ffn(xc, l0_norm2, csh2, csc2, cg2, l0_ffn_w_in, l0_ffn_w_out, n_ctx)

    (sh1, sc1, g1, sh2, sc2, g2), (csh1, csc1, cg1, csh2, csc2, cg2) = modulation(l1_mod_w, l1_mod_b)
    n_heads = d // HEAD_DIM
    cos, sin = _rope_tables(s)
    kv_cols = GQA_KV_HEADS * HEAD_DIM
    cg_gqa = _column_gains(l1_gqa_q_norm, l1_gqa_k_norm, d, kv_cols, kv_cols)
    w1 = l1_gqa_w_in.astype(BF16)
    pl1 = qkv_proj(xl, w1, l1_norm1, sh1, sc1, cg_gqa, s, "gqa_proj", normed_cols=d + kv_cols, cos=cos, sin=sin)
    pc1 = qkv_proj(xc, w1, l1_norm1, csh1, csc1, cg_gqa, n_ctx, "gqa_proj_ctx", normed_cols=d + kv_cols)
    yl = _gqa_attention(pl1, [pl1, pc1], batch=b, n_q_heads=n_heads, n_kv_heads=GQA_KV_HEADS, tq=tq_gqa, q_rows=s,
                        name="gqa_attn")
    yc = _gqa_attention(pc1, [pc1], batch=b, n_q_heads=n_heads, n_kv_heads=GQA_KV_HEADS, tq=n_ctx, q_rows=n_ctx,
                        name="gqa_attn_ctx")
    xl = out_resid(yl, l1_gqa_w_out, xl, g1, s)
    xc = out_resid(yc, l1_gqa_w_out, xc, cg1, n_ctx)
    xl = ffn(xl, l1_norm2, sh2, sc2, g2, l1_ffn_w_in, l1_ffn_w_out, s)
    xc = ffn(xc, l1_norm2, csh2, csc2, cg2, l1_ffn_w_in, l1_ffn_w_out, n_ctx)

    (sh1, sc1, g1, sh2, sc2, g2), (csh1, csc1, _, _, _, _) = modulation(l2_mod_w, l2_mod_b)
    lam_init = 0.8 - 0.6 * math.exp(-0.3 * 2)
    cg_diff = _column_gains(l2_diff_q_norm, l2_diff_k_norm, d, d, d)
    w2 = l2_diff_w_in.astype(BF16)
    pl2 = qkv_proj(xl, w2, l2_norm1, sh1, sc1, cg_diff, s, "diff_proj", normed_cols=2 * d, cos=cos, sin=sin)
    pc2 = qkv_proj(xc, w2, l2_norm1, csh1, csc1, cg_diff, n_ctx, "diff_proj_ctx", normed_cols=2 * d)
    yl = _diff_attention(pl2, [pl2, pc2], (l2_diff_lq1, l2_diff_lk1, l2_diff_lq2, l2_diff_lk2),
                         l2_diff_out_norm, batch=b, n_heads=n_heads // 2, tq=tq_diff, q_rows=s, lam_init=lam_init,
                         name="diff_attn")
    xl = out_resid(yl, l2_diff_w_out, xl, g1, s)
    xl = ffn(xl, l2_norm2, sh2, sc2, g2, l2_ffn_w_in, l2_ffn_w_out, s)

    (sh1, sc1, g1, sh2, sc2, g2), _ = modulation(l3_mod_w, l3_mod_b)
    yl = _fnet(xl, l3_norm1, sh1, sc1, batch=b, tm=tm)
    xl = out_resid(yl, l3_fnet_w_out, xl, g1, s)
    xl = ffn(xl, l3_norm2, sh2, sc2, g2, l3_ffn_w_in, l3_ffn_w_out, s)
    return xl.reshape(b, s, d)
```

```python
import functools
import math

import jax
import jax.numpy as jnp
from jax import lax
from jax.experimental import pallas as pl
from jax.experimental.pallas import tpu as pltpu

F32 = jnp.float32
BF16 = jnp.bfloat16

NORM_EPS = 1e-6
ROPE_THETA = 10000.0
GRID_W = 64
HEAD_DIM = 128
GQA_KV_HEADS = 4
GLA_HEADS = 4
GLA_RANK = 16
GLA_TAU = 16.0
GLA_CHUNK = 64
GLA_BLOCK = 4 * GLA_CHUNK
FNET_GROUPS = 4

LANES = 128
BF16_ROWS = 16
STRIP_UNROLL = 8
VMEM_LIMIT = 56 * 1024 * 1024

NT_DIMS = (((1,), (1,)), ((), ()))
TN_DIMS = (((0,), (0,)), ((), ()))


def _params(*sem):
    return pltpu.CompilerParams(dimension_semantics=sem, vmem_limit_bytes=VMEM_LIMIT)


def _rms(x, eps=NORM_EPS):
    return x * lax.rsqrt(jnp.mean(x * x, axis=-1, keepdims=True) + eps)


def _norm_mod_rows(x_ref, xn_ref, gain, shift, scale):
    mult = gain * (1.0 + scale)

    def strip(r, carry):
        rows = pl.ds(pl.multiple_of(r * BF16_ROWS, BF16_ROWS), BF16_ROWS)
        xn_ref[rows, :] = (_rms(x_ref[rows, :]) * mult + shift).astype(BF16)
        return carry

    lax.fori_loop(0, x_ref.shape[0] // BF16_ROWS, strip, 0, unroll=STRIP_UNROLL)


ROPE_STRIP = 256
MXU_WIDTH = 256


def _head_group_matrices():
    src = lax.broadcasted_iota(jnp.int32, (MXU_WIDTH, MXU_WIDTH), 0)
    dst = lax.broadcasted_iota(jnp.int32, (MXU_WIDTH, MXU_WIDTH), 1)
    quarter = HEAD_DIM // 4
    partner = jnp.where((dst // quarter) % 2 == 0, dst + quarter, dst - quarter)
    mean = jnp.where(src // HEAD_DIM == dst // HEAD_DIM, 1.0 / HEAD_DIM, 0.0).astype(BF16)
    return mean, (src == partner).astype(BF16)


def _mm_kernel(*refs, prologue, epilogue):
    it = iter(refs)
    x_ref = next(it)
    if prologue == "norm_mod":
        gain_ref, shift_ref, scale_ref = next(it), next(it), next(it)
    w_ref = next(it)
    if epilogue == "bias":
        b_ref = next(it)
    elif epilogue == "resid":
        res_ref, gate_ref = next(it), next(it)
    o_ref = next(it)
    if prologue != "cast":
        xn_ref = next(it)
    j = pl.program_id(1)

    if prologue == "cast":
        a = x_ref[...].astype(BF16)
    else:
        @pl.when(j == 0)
        def _():
            if prologue == "norm_mod":
                _norm_mod_rows(x_ref, xn_ref, gain_ref[...], shift_ref[0], scale_ref[0])
            else:
                xn_ref[...] = jax.nn.silu(x_ref[...]).astype(BF16)

        a = xn_ref[...]
    acc = jnp.dot(a, w_ref[...].astype(BF16), preferred_element_type=F32)
    if epilogue == "store":
        o_ref[...] = acc.astype(o_ref.dtype)
    elif epilogue == "bias":
        o_ref[...] = (acc + b_ref[...]).astype(o_ref.dtype)
    elif epilogue == "resid":
        o_ref[...] = (res_ref[...] + gate_ref[0] * acc).astype(o_ref.dtype)


def _mm(x, w, *, tm, tn, out_dtype, name, prologue="cast", epilogue="store", rows_per_batch=None,
        gain=None, shift=None, scale=None, bias=None, res=None, gate=None):
    t, k = x.shape
    n = w.shape[1]
    assert t % tm == 0 and n % tn == 0, (t, tm, n, tn)
    rows_per_batch = rows_per_batch or t
    assert rows_per_batch % tm == 0
    tiles_per_batch = rows_per_batch // tm
    once = dict(pipeline_mode=pl.Buffered(1))

    def batch_of(i):
        return i // tiles_per_batch

    args = [x]
    specs = [pl.BlockSpec((tm, k), lambda i, j: (i, 0))]
    if prologue == "norm_mod":
        nb = shift.shape[0]
        bsel = (lambda i: batch_of(i)) if nb > 1 else (lambda i: 0)
        args += [gain.reshape(1, k), shift, scale]
        specs += [pl.BlockSpec((1, k), lambda i, j: (0, 0)),
                  pl.BlockSpec((1, 1, k), lambda i, j: (bsel(i), 0, 0)),
                  pl.BlockSpec((1, 1, k), lambda i, j: (bsel(i), 0, 0))]
    args.append(w)
    specs.append(pl.BlockSpec((k, tn), lambda i, j: (0, j), **(once if tn == n else {})))
    if epilogue == "bias":
        args.append(bias.reshape(1, n))
        specs.append(pl.BlockSpec((1, tn), lambda i, j: (0, j)))
    elif epilogue == "resid":
        nb = gate.shape[0]
        gsel = (lambda i: batch_of(i)) if nb > 1 else (lambda i: 0)
        args += [res, gate]
        specs += [pl.BlockSpec((tm, tn), lambda i, j: (i, j)),
                  pl.BlockSpec((1, 1, tn), lambda i, j: (gsel(i), 0, j))]
    scratch = []
    if prologue != "cast":
        scratch.append(pltpu.VMEM((tm, k), BF16))
    kern = functools.partial(_mm_kernel, prologue=prologue, epilogue=epilogue)
    return pl.pallas_call(
        kern,
        grid=(t // tm, n // tn),
        in_specs=specs,
        out_specs=pl.BlockSpec((tm, tn), lambda i, j: (i, j)),
        out_shape=jax.ShapeDtypeStruct((t, n), out_dtype),
        scratch_shapes=scratch,
        compiler_params=_params("parallel", "arbitrary"),
        name=name,
    )(*args)


def _qkv_kernel(*refs, normed_cols, rope):
    it = iter(refs)
    x_ref, gain_ref, shift_ref, scale_ref, w_ref, hg_ref, ones_ref = [next(it) for _ in range(7)]
    if rope:
        perm_ref, cos_ref, sin_ref = next(it), next(it), next(it)
    o_ref, xn_ref = next(it), next(it)
    j = pl.program_id(1)
    tm, tn = o_ref.shape
    gw = ones_ref.shape[0]
    n_sub = tn // gw
    full_tiles, rem = divmod(normed_cols, tn)

    @pl.when(j == 0)
    def _():
        _norm_mod_rows(x_ref, xn_ref, gain_ref[...], shift_ref[0], scale_ref[0])

    def norm_rope_store(acc, cols):
        for r0 in range(0, tm, ROPE_STRIP):
            rows = slice(r0, r0 + ROPE_STRIP)
            blk = acc[rows, :]
            ms = jnp.dot((blk * blk).astype(BF16), ones_ref[...], preferred_element_type=F32)
            y = blk * lax.rsqrt(ms + NORM_EPS) * hg_ref[:, cols]
            if rope:
                partner = jnp.dot(y.astype(BF16), perm_ref[...], preferred_element_type=F32)
                reps = gw // HEAD_DIM
                y = (y * jnp.concatenate([cos_ref[rows, :]] * reps, axis=-1)
                     + partner * jnp.concatenate([sin_ref[rows, :]] * reps, axis=-1))
            o_ref[rows, cols] = y.astype(o_ref.dtype)

    def tile(n_normed_subs):
        a = xn_ref[...]
        if n_normed_subs == 0:
            o_ref[...] = jnp.dot(a, w_ref[...], preferred_element_type=F32).astype(o_ref.dtype)
            return
        sub_cols = [slice(s * gw, (s + 1) * gw) for s in range(n_sub)]
        acc = jnp.dot(a, w_ref[:, sub_cols[0]], preferred_element_type=F32)
        for s in range(n_sub):
            nxt = jnp.dot(a, w_ref[:, sub_cols[s + 1]], preferred_element_type=F32) if s + 1 < n_sub else None
            if s < n_normed_subs:
                norm_rope_store(acc, sub_cols[s])
            else:
                o_ref[:, sub_cols[s]] = acc.astype(o_ref.dtype)
            acc = nxt

    if full_tiles:
        pl.when(j < full_tiles)(functools.partial(tile, n_sub))
    if rem:
        pl.when(j == full_tiles)(functools.partial(tile, rem // gw))
    pl.when(j >= full_tiles + (1 if rem else 0))(functools.partial(tile, 0))


def _qkv_proj(x, w, gain, shift, scale, col_gain, *, normed_cols, tm, tn, name, rows_per_batch=None,
              cos=None, sin=None):
    t, k = x.shape
    n = w.shape[1]
    assert t % tm == 0 and n % tn == 0 and normed_cols % MXU_WIDTH == 0
    rows_per_batch = rows_per_batch or t
    assert rows_per_batch % tm == 0
    tiles_per_batch = rows_per_batch // tm
    rope = cos is not None
    nb = shift.shape[0]
    bsel = (lambda i: i // tiles_per_batch) if nb > 1 else (lambda i: 0)
    ones, perm = _head_group_matrices()
    const = pl.BlockSpec(ones.shape, lambda i, j: (0, 0))
    vec = pl.BlockSpec((1, 1, k), lambda i, j: (bsel(i), 0, 0))
    args = [x, gain.reshape(1, k), shift, scale, w, col_gain, ones]
    specs = [pl.BlockSpec((tm, k), lambda i, j: (i, 0)),
             pl.BlockSpec((1, k), lambda i, j: (0, 0)), vec, vec,
             pl.BlockSpec((k, tn), lambda i, j: (0, j)),
             pl.BlockSpec((1, tn), lambda i, j: (0, j)), const]
    if rope:
        table = pl.BlockSpec((tm, HEAD_DIM), lambda i, j: (i % tiles_per_batch, 0))
        args += [perm, cos, sin]
        specs += [const, table, table]
    return pl.pallas_call(
        functools.partial(_qkv_kernel, normed_cols=normed_cols, rope=rope),
        grid=(t // tm, n // tn),
        in_specs=specs,
        out_specs=pl.BlockSpec((tm, tn), lambda i, j: (i, j)),
        out_shape=jax.ShapeDtypeStruct((t, n), BF16),
        scratch_shapes=[pltpu.VMEM((tm, k), BF16)],
        compiler_params=_params("parallel", "arbitrary"),
        name=name,
    )(*args)


FFN_HID_SPLIT = 2


def _ffn_kernel(*refs, n_cast):
    x_ref, gain_ref, shift_ref, scale_ref, gate_ref, wg_ref, wu_ref, wo_ref = refs[:8]
    cast_src = refs[8:8 + n_cast]
    o_ref = refs[8 + n_cast]
    cast_dst = refs[9 + n_cast:9 + 2 * n_cast]
    xn_ref = refs[9 + 2 * n_cast]
    j = pl.program_id(1)
    hc = wg_ref.shape[1] // FFN_HID_SPLIT
    cols = [slice(c * hc, (c + 1) * hc) for c in range(FFN_HID_SPLIT)]
    for src, dst in zip(cast_src, cast_dst):
        dst[...] = src[...].astype(dst.dtype)

    def chunk(first, last):
        if first:
            _norm_mod_rows(x_ref, xn_ref, gain_ref[...], shift_ref[0], scale_ref[0])
        xn = xn_ref[...]
        gu = [(jnp.dot(xn, wg_ref[:, cs], preferred_element_type=F32),
               jnp.dot(xn, wu_ref[:, cs], preferred_element_type=F32)) for cs in cols]
        for c, (g, u) in enumerate(gu):
            a = (jax.nn.silu(g) * u).astype(BF16)
            part = jnp.dot(a, wo_ref[cols[c], :], preferred_element_type=F32)
            if first and c == 0:
                o_ref[...] = part
            elif last and c == len(gu) - 1:
                o_ref[...] = x_ref[...] + gate_ref[0] * (o_ref[...] + part)
            else:
                o_ref[...] += part

    n_chunks = pl.num_programs(1)
    pl.when(j == 0)(functools.partial(chunk, True, False))
    pl.when((j > 0) & (j < n_chunks - 1))(functools.partial(chunk, False, False))
    pl.when(j == n_chunks - 1)(functools.partial(chunk, False, True))


CAST_SLAB_LANES = 8 * LANES


def _ffn(x, gain, shift, scale, gate, w_in, w_out, *, tm, th, name, rows_per_batch=None, cast=()):
    t, d = x.shape
    hidden = w_out.shape[0]
    assert t % tm == 0 and hidden % th == 0 and hidden // th >= 2
    nh = hidden // th
    n_steps = (t // tm) * nh
    rows_per_batch = rows_per_batch or t
    tiles_per_batch = rows_per_batch // tm
    nb = shift.shape[0]
    bsel = (lambda i: i // tiles_per_batch) if nb > 1 else (lambda i: 0)
    vec = pl.BlockSpec((1, 1, d), lambda i, j: (bsel(i), 0, 0))
    slabs = []
    for a in cast:
        slab_rows, rem = divmod(a.size, n_steps * CAST_SLAB_LANES)
        assert rem == 0 and slab_rows % BF16_ROWS == 0, (a.shape, n_steps)
        slabs.append(a.reshape(n_steps, slab_rows, CAST_SLAB_LANES))
    slab_specs = [pl.BlockSpec((1,) + s.shape[1:], lambda i, j: (i * nh + j, 0, 0)) for s in slabs]
    outs = pl.pallas_call(
        functools.partial(_ffn_kernel, n_cast=len(slabs)),
        grid=(t // tm, nh),
        in_specs=[pl.BlockSpec((tm, d), lambda i, j: (i, 0)),
                  pl.BlockSpec((1, d), lambda i, j: (0, 0)),
                  vec, vec, vec,
                  pl.BlockSpec((d, th), lambda i, j: (0, j)),
                  pl.BlockSpec((d, th), lambda i, j: (0, nh + j)),
                  pl.BlockSpec((th, d), lambda i, j: (j, 0))] + slab_specs,
        out_specs=[pl.BlockSpec((tm, d), lambda i, j: (i, 0))] + slab_specs,
        out_shape=[jax.ShapeDtypeStruct((t, d), F32)] + [jax.ShapeDtypeStruct(s.shape, BF16) for s in slabs],
        scratch_shapes=[pltpu.VMEM((tm, d), BF16)],
        compiler_params=_params("parallel", "arbitrary"),
        name=name,
    )(x, gain.reshape(1, d), shift, scale, gate, w_in, w_in, w_out, *slabs)
    return outs[0], [o.reshape(a.shape) for o, a in zip(outs[1:], cast)]


def _fill_kv(k_s, v_s, kv_refs):
    off = 0
    for k_ref, v_ref in kv_refs:
        n = k_ref.shape[0]
        k_s[off:off + n, :] = k_ref[...]
        v_s[off:off + n, :] = v_ref[...]
        off += n


ATTN_SUBTILE = 256


def _gqa_kernel(*refs, n_kv_src, n_group):
    q_ref = refs[0]
    kv_refs = [(refs[1 + 2 * s], refs[2 + 2 * s]) for s in range(n_kv_src)]
    o_ref, k_s, v_s = refs[1 + 2 * n_kv_src:]

    @pl.when(pl.program_id(2) == 0)
    def _():
        _fill_kv(k_s, v_s, kv_refs)

    k = k_s[...]
    v = v_s[...]
    sub = min(ATTN_SUBTILE, q_ref.shape[0])
    chains = [(slice(r0, r0 + sub), slice(g * HEAD_DIM, (g + 1) * HEAD_DIM))
              for r0 in range(0, q_ref.shape[0], sub) for g in range(n_group)]

    def scores(c):
        return lax.dot_general(q_ref[chains[c]], k, NT_DIMS, preferred_element_type=F32)

    s = scores(0)
    for c in range(len(chains)):
        s_next = scores(c + 1) if c + 1 < len(chains) else None
        p = jnp.exp2(s - jnp.max(s, axis=-1, keepdims=True))
        l = jnp.sum(p, axis=-1, keepdims=True)
        o = jnp.dot(p.astype(BF16), v, preferred_element_type=F32)
        o_ref[chains[c]] = (o / l).astype(o_ref.dtype)
        s = s_next


def _gqa_attention(q_src, kv_srcs, *, batch, n_q_heads, n_kv_heads, tq, q_rows, name):
    group = n_q_heads // n_kv_heads
    gw = group * HEAD_DIM
    nq = q_rows // tq
    k_blk0 = n_q_heads
    v_blk0 = n_q_heads + n_kv_heads
    specs = [pl.BlockSpec((tq, gw), lambda b, h, i: (b * nq + i, h))]
    args = [q_src]
    total = 0
    for src in kv_srcs:
        rows = src.shape[0] // batch
        total += rows
        specs += [pl.BlockSpec((rows, HEAD_DIM), lambda b, h, i: (b, k_blk0 + h)),
                  pl.BlockSpec((rows, HEAD_DIM), lambda b, h, i: (b, v_blk0 + h))]
        args += [src, src]
    kern = functools.partial(_gqa_kernel, n_kv_src=len(kv_srcs), n_group=group)
    return pl.pallas_call(
        kern,
        grid=(batch, n_kv_heads, nq),
        in_specs=specs,
        out_specs=pl.BlockSpec((tq, gw), lambda b, h, i: (b * nq + i, h)),
        out_shape=jax.ShapeDtypeStruct((batch * q_rows, n_q_heads * HEAD_DIM), BF16),
        scratch_shapes=[pltpu.VMEM((total, HEAD_DIM), BF16), pltpu.VMEM((total, HEAD_DIM), BF16)],
        compiler_params=_params("parallel", "parallel", "arbitrary"),
        name=name,
    )(*args)


def _diff_kernel(*refs, n_kv_src, lam_init):
    q_ref = refs[0]
    kv_refs = [(refs[1 + 2 * s], refs[2 + 2 * s]) for s in range(n_kv_src)]
    lq1, lk1, lq2, lk2, gain_ref, o_ref, k_s, v_s = refs[1 + 2 * n_kv_src:]

    @pl.when(pl.program_id(2) == 0)
    def _():
        _fill_kv(k_s, v_s, kv_refs)

    lam = (jnp.exp(jnp.sum(lq1[...] * lk1[...], axis=-1, keepdims=True))
           - jnp.exp(jnp.sum(lq2[...] * lk2[...], axis=-1, keepdims=True)) + lam_init)
    dh = HEAD_DIM
    sub = min(ATTN_SUBTILE, q_ref.shape[0])
    chains = [slice(r0, r0 + sub) for r0 in range(0, q_ref.shape[0], sub)]

    def scores(c):
        return [lax.dot_general(q_ref[chains[c], m * dh:(m + 1) * dh], k_s[:, m * dh:(m + 1) * dh], NT_DIMS,
                                preferred_element_type=F32) for m in range(2)]

    s = scores(0)
    for c in range(len(chains)):
        s_next = scores(c + 1) if c + 1 < len(chains) else None
        p = [jnp.exp2(sm - jnp.max(sm, axis=-1, keepdims=True)) for sm in s]
        l0, l1 = [jnp.sum(pm, axis=-1, keepdims=True) for pm in p]
        w = p[0] - p[1] * (lam * l0 / l1)
        o = jnp.dot(w.astype(BF16), v_s[...], preferred_element_type=F32) * (1.0 / l0)
        o_ref[chains[c], :] = ((_rms(o) * gain_ref[...]) * (1.0 - lam_init)).astype(o_ref.dtype)
        s = s_next


def _diff_attention(q_src, kv_srcs, lams, out_gain, *, batch, n_heads, tq, q_rows, lam_init, name):
    hw = 2 * HEAD_DIM
    nq = q_rows // tq
    specs = [pl.BlockSpec((tq, hw), lambda b, h, i: (b * nq + i, h))]
    args = [q_src]
    total = 0
    for src in kv_srcs:
        rows = src.shape[0] // batch
        total += rows
        specs += [pl.BlockSpec((rows, hw), lambda b, h, i: (b, n_heads + h)),
                  pl.BlockSpec((rows, hw), lambda b, h, i: (b, 2 * n_heads + h))]
        args += [src, src]
    small = pl.BlockSpec((1, HEAD_DIM), lambda b, h, i: (0, 0))
    specs += [small] * 4 + [pl.BlockSpec((1, hw), lambda b, h, i: (0, 0))]
    args += [v.reshape(1, HEAD_DIM) for v in lams] + [out_gain.reshape(1, hw)]
    kern = functools.partial(_diff_kernel, n_kv_src=len(kv_srcs), lam_init=lam_init)
    return pl.pallas_call(
        kern,
        grid=(batch, n_heads, nq),
        in_specs=specs,
        out_specs=pl.BlockSpec((tq, hw), lambda b, h, i: (b * nq + i, h)),
        out_shape=jax.ShapeDtypeStruct((batch * q_rows, n_heads * hw), BF16),
        scratch_shapes=[pltpu.VMEM((total, hw), BF16), pltpu.VMEM((total, hw), BF16)],
        compiler_params=_params("parallel", "parallel", "arbitrary"),
        name=name,
    )(*args)


def _gla_blocks(blocks):
    r, dk = blocks[0][0].shape
    c = GLA_CHUNK
    nc = r // c
    n = range(len(blocks))
    qs_, ks_, vs_, zs_, wgs, bgs, st_refs, revs = zip(*blocks)
    row = lax.broadcasted_iota(jnp.int32, (r, r), 0)
    col = lax.broadcasted_iota(jnp.int32, (r, r), 1)
    same = row // c == col // c
    tri_f = [jnp.where(same, jnp.where((col >= row) if revs[i] else (col <= row), 1.0, 0.0), 0.0) for i in n]
    tri = [t.astype(BF16) for t in tri_f]
    mid = [c // 2 if revs[i] else c // 2 - 1 for i in n]
    last = [0 if revs[i] else c - 1 for i in n]

    g = [jax.nn.log_sigmoid(jnp.dot(zs_[i], wgs[i], preferred_element_type=F32) + bgs[i]) / GLA_TAU for i in n]
    g_hi = [g[i].astype(BF16) for i in n]
    g_lo = [(g[i] - g_hi[i].astype(F32)).astype(BF16) for i in n]
    cum = [jnp.dot(tri[i], g_hi[i], preferred_element_type=F32) + jnp.dot(tri[i], g_lo[i], preferred_element_type=F32)
           for i in n]

    def chunk_row(x, idx):
        return jnp.concatenate(
            [jnp.broadcast_to(x[ci * c + idx:ci * c + idx + 1, :], (c, dk)) for ci in range(nc)], axis=0)

    cum_mid = [chunk_row(cum[i], mid[i]) for i in n]
    cum_last = [chunk_row(cum[i], last[i]) for i in n]
    qs = [(qs_[i] * jnp.exp(cum[i] - cum_mid[i])).astype(BF16) for i in n]
    ks = [(ks_[i] * jnp.exp(cum_mid[i] - cum[i])).astype(BF16) for i in n]
    a = [lax.dot_general(qs[i], ks[i], NT_DIMS, preferred_element_type=F32) for i in n]
    a = [jnp.where(tri_f[i] > 0.5, a[i], 0.0).astype(BF16) for i in n]
    o_intra = [jnp.dot(a[i], vs_[i], preferred_element_type=F32) for i in n]
    q_inter = [(qs_[i] * jnp.exp(cum[i])).astype(BF16) for i in n]
    k_carry = [(ks_[i] * jnp.exp(cum_last[i] - cum[i])).astype(BF16) for i in n]
    st = [st_refs[i][...] for i in n]
    o_inter = [[None] * nc for _ in n]
    for step in range(nc):
        for i in n:
            ci = nc - 1 - step if revs[i] else step
            rows = slice(ci * c, (ci + 1) * c)
            o_inter[i][ci] = lax.dot_general(q_inter[i][rows], st[i].astype(BF16), NT_DIMS,
                                             preferred_element_type=F32)
            decay = jnp.exp(cum[i][ci * c + last[i]:ci * c + last[i] + 1, :])
            st[i] = st[i] * decay + lax.dot_general(vs_[i][rows], k_carry[i][rows], TN_DIMS,
                                                    preferred_element_type=F32)
    for i in n:
        st_refs[i][...] = st[i]
    return [o_intra[i] + jnp.concatenate(o_inter[i], axis=0) for i in n]


def _gla_kernel(qc_ref, kc_ref, vc_ref, rc_ref, zc_ref, ql_ref, kl_ref, vl_ref, rl_ref, zl_ref,
                wgf_ref, bgf_ref, wgb_ref, bgb_ref, gain_ref, oc_ref, ol_ref, sf_ref, sb_ref, of_ref, ob_ref):
    blk = GLA_BLOCK
    n_ctx = qc_ref.shape[0] // blk
    n_lat = ql_ref.shape[0] // blk
    q_scale = qc_ref.shape[1] ** -0.5
    ctx_refs = (qc_ref, kc_ref, vc_ref, zc_ref)
    lat_refs = (ql_ref, kl_ref, vl_ref, zl_ref)

    def rows_of(bi):
        return pl.ds(pl.multiple_of(bi * blk, blk), blk)

    def block(refs, bi, reverse):
        q_ref, k_ref, v_ref, z_ref = refs
        rows = rows_of(bi)
        wg_ref, bg_ref, st_ref = (wgb_ref, bgb_ref, sb_ref) if reverse else (wgf_ref, bgf_ref, sf_ref)
        return (q_ref[rows, :].astype(F32) * q_scale, k_ref[rows, :].astype(F32), v_ref[rows, :], z_ref[rows, :],
                wg_ref[0], bg_ref[0], st_ref, reverse)

    def both(refs, base, n):
        def body(t, carry):
            o_f, o_b = _gla_blocks([block(refs, t, False), block(refs, n - 1 - t, True)])
            of_ref[rows_of(base + t), :] = o_f
            ob_ref[rows_of(base + n - 1 - t), :] = o_b
            return carry
        return body

    def finish(r_ref, out_ref, base):
        def body(bi, carry):
            rows = rows_of(bi)
            o = of_ref[rows_of(base + bi), :] + ob_ref[rows_of(base + bi), :]
            y = (_rms(o) * gain_ref[...]) * jax.nn.silu(r_ref[rows, :].astype(F32))
            out_ref[rows, :] = y.astype(out_ref.dtype)
            return carry
        return body

    sf_ref[...] = jnp.zeros_like(sf_ref)
    sb_ref[...] = jnp.zeros_like(sb_ref)
    lax.fori_loop(0, n_ctx, both(ctx_refs, 0, n_ctx), 0)
    lax.fori_loop(0, n_lat, both(lat_refs, n_ctx, n_lat), 0)
    lax.fori_loop(0, n_ctx, finish(rc_ref, oc_ref, 0), 0)
    lax.fori_loop(0, n_lat, finish(rl_ref, ol_ref, n_ctx), 0)


def _gla(p_ctx, p_lat, wgf, bgf, wgb, bgb, out_gain, *, batch, dk, dv):
    h = GLA_HEADS
    rows_c = p_ctx.shape[0] // batch
    rows_l = p_lat.shape[0] // batch
    assert rows_c % GLA_BLOCK == 0 and rows_l % GLA_BLOCK == 0
    zblk = (2 * h * dk + 2 * h * dv) // LANES
    k0 = h
    v0 = (2 * h * dk) // dv
    r0 = v0 + h

    def stream(rows):
        return [pl.BlockSpec((rows, dk), lambda b, hh: (b, hh)),
                pl.BlockSpec((rows, dk), lambda b, hh: (b, k0 + hh)),
                pl.BlockSpec((rows, dv), lambda b, hh: (b, v0 + hh)),
                pl.BlockSpec((rows, dv), lambda b, hh: (b, r0 + hh)),
                pl.BlockSpec((rows, LANES), lambda b, hh: (b, zblk))]

    wspec = pl.BlockSpec((1, LANES, dk), lambda b, hh: (hh, 0, 0))
    bspec = pl.BlockSpec((1, 1, dk), lambda b, hh: (hh, 0, 0))
    return pl.pallas_call(
        _gla_kernel,
        grid=(batch, h),
        in_specs=stream(rows_c) + stream(rows_l) + [wspec, bspec, wspec, bspec,
                                                    pl.BlockSpec((1, dv), lambda b, hh: (0, 0))],
        out_specs=[pl.BlockSpec((rows_c, dv), lambda b, hh: (b, hh)),
                   pl.BlockSpec((rows_l, dv), lambda b, hh: (b, hh))],
        out_shape=[jax.ShapeDtypeStruct((batch * rows_c, h * dv), BF16),
                   jax.ShapeDtypeStruct((batch * rows_l, h * dv), BF16)],
        scratch_shapes=[pltpu.VMEM((dv, dk), F32), pltpu.VMEM((dv, dk), F32),
                        pltpu.VMEM((rows_c + rows_l, dv), F32), pltpu.VMEM((rows_c + rows_l, dv), F32)],
        compiler_params=_params("parallel", "parallel"),
        name="gla",
    )(*([p_ctx] * 5 + [p_lat] * 5 + [wgf, bgf, wgb, bgb, out_gain.reshape(1, dv)]))


def _dft_tables(n):
    idx = jnp.arange(n, dtype=jnp.int32)
    ang = ((idx[:, None] * idx[None, :]) % n).astype(F32) * (2.0 * math.pi / n)
    return jnp.cos(ang).astype(BF16), jnp.sin(ang).astype(BF16)


def _fnet_chan_kernel(x_ref, gain_ref, shift_ref, scale_ref, csc_ref, p_ref, q_ref, xn_ref):
    gd = csc_ref.shape[0]
    _norm_mod_rows(x_ref, xn_ref, gain_ref[...], shift_ref[0], scale_ref[0])
    for g in range(x_ref.shape[1] // gd):
        cols = slice(g * gd, (g + 1) * gd)
        pq = jnp.dot(xn_ref[:, cols], csc_ref[...], preferred_element_type=F32)
        p_ref[:, cols] = pq[:, :gd].astype(p_ref.dtype)
        q_ref[:, cols] = pq[:, gd:].astype(q_ref.dtype)


def _fnet_seq_kernel(cs_ref, ss_ref, p_ref, q_ref, o_ref, *, inv_norm):
    acc = (jnp.dot(cs_ref[...], p_ref[...], preferred_element_type=F32)
           - jnp.dot(ss_ref[...], q_ref[...], preferred_element_type=F32))
    o_ref[...] = (acc * inv_norm).astype(o_ref.dtype)


def _fnet(x, gain, shift, scale, *, batch, tm):
    t, d = x.shape
    s = t // batch
    gd = d // FNET_GROUPS
    tiles_per_batch = s // tm
    cc, sc = _dft_tables(gd)
    cs, ss = _dft_tables(s)
    vec = pl.BlockSpec((1, 1, d), lambda i: (i // tiles_per_batch, 0, 0))
    blk = pl.BlockSpec((tm, d), lambda i: (i, 0))
    p, q = pl.pallas_call(
        _fnet_chan_kernel,
        grid=(t // tm,),
        in_specs=[blk, pl.BlockSpec((1, d), lambda i: (0, 0)), vec, vec,
                  pl.BlockSpec((gd, 2 * gd), lambda i: (0, 0))],
        out_specs=[blk, blk],
        out_shape=[jax.ShapeDtypeStruct((t, d), BF16)] * 2,
        scratch_shapes=[pltpu.VMEM((tm, d), BF16)],
        compiler_params=_params("parallel"),
        name="fnet_chan",
    )(x, gain.reshape(1, d), shift, scale, jnp.concatenate([cc, sc], axis=1))
    rows = pl.BlockSpec((tm, s), lambda b, j, i: (i, 0))
    cols = pl.BlockSpec((s, gd), lambda b, j, i: (b, j))
    return pl.pallas_call(
        functools.partial(_fnet_seq_kernel, inv_norm=float((s * gd) ** -0.5)),
        grid=(batch, d // gd, tiles_per_batch),
        in_specs=[rows, rows, cols, cols],
        out_specs=pl.BlockSpec((tm, gd), lambda b, j, i: (b * tiles_per_batch + i, j)),
        out_shape=jax.ShapeDtypeStruct((t, d), BF16),
        compiler_params=_params("parallel", "parallel", "arbitrary"),
        name="fnet_seq",
    )(cs, ss, p, q)


def _rope_tables(n_tokens):
    t = jnp.arange(n_tokens)
    row = (t // GRID_W).astype(F32)
    col = (t % GRID_W).astype(F32)
    half = HEAD_DIM // 2
    inv_freq = ROPE_THETA ** (-jnp.arange(0, half, 2, dtype=F32) / half)
    ang_r = row[:, None] * inv_freq[None, :]
    ang_c = col[:, None] * inv_freq[None, :]
    ang = jnp.concatenate([ang_r, ang_r, ang_c, ang_c], axis=-1)
    sign = jnp.concatenate([-jnp.ones((half // 2,), F32), jnp.ones((half // 2,), F32)] * 2)
    return jnp.cos(ang), jnp.sin(ang) * sign


def _column_gains(q_gain, k_gain, q_cols, k_cols, v_cols):
    qg = jnp.tile(q_gain.astype(F32) * (HEAD_DIM ** -0.5 * math.log2(math.e)), q_cols // HEAD_DIM)
    kg = jnp.tile(k_gain.astype(F32), k_cols // HEAD_DIM)
    return jnp.concatenate([qg, kg, jnp.ones((v_cols,), F32)])[None, :]


def _gate_weights(wg, bg, lane0, dk):
    r = wg.shape[0]
    w = wg.reshape(r, GLA_HEADS, dk).transpose(1, 0, 2)
    w = jnp.pad(w, ((0, 0), (lane0, LANES - lane0 - r), (0, 0))).astype(BF16)
    return w, bg.reshape(GLA_HEADS, 1, dk).astype(F32)


def kernel(x, c, ctx, c_ctx, l0_mod_w, l0_mod_b, l0_norm1, l0_gla_w_in, l0_gla_wg_f, l0_gla_bg_f, l0_gla_wg_b, l0_gla_bg_b, l0_gla_out_norm, l0_gla_w_out, l0_norm2, l0_ffn_w_in, l0_ffn_w_out, l1_mod_w, l1_mod_b, l1_norm1, l1_gqa_w_in, l1_gqa_q_norm, l1_gqa_k_norm, l1_gqa_w_out, l1_norm2, l1_ffn_w_in, l1_ffn_w_out, l2_mod_w, l2_mod_b, l2_norm1, l2_diff_w_in, l2_diff_q_norm, l2_diff_k_norm, l2_diff_lq1, l2_diff_lk1, l2_diff_lq2, l2_diff_lk2, l2_diff_out_norm, l2_diff_w_out, l2_norm2, l2_ffn_w_in, l2_ffn_w_out, l3_mod_w, l3_mod_b, l3_norm1, l3_fnet_w_out, l3_norm2, l3_ffn_w_in, l3_ffn_w_out):
    b, s, d = x.shape
    n_ctx = ctx.shape[1]
    xl = x.reshape(b * s, d)
    xc = ctx.reshape(b * n_ctx, d)
    tm = math.gcd(s, 1024)
    tm_c = math.gcd(b * n_ctx, 1024)
    tm_r = math.gcd(s, 512)
    tm_rc = math.gcd(b * n_ctx, 512)
    th = math.gcd(l0_ffn_w_out.shape[0], 512)
    tq_gqa = math.gcd(s, 4 * ATTN_SUBTILE)
    tq_diff = math.gcd(s, 4 * ATTN_SUBTILE)

    n_cond = -(-(b + 1) // BF16_ROWS) * BF16_ROWS
    cond = jnp.concatenate([c, c_ctx[None, :], jnp.zeros((n_cond - b - 1, d), F32)], axis=0)

    def modulation(mod_w, mod_b):
        m = _mm(cond, mod_w, tm=n_cond, tn=math.gcd(mod_w.shape[1], 1024), out_dtype=F32, prologue="silu", epilogue="bias", bias=mod_b,
                name="modulation")
        lat = [m[:b, k * d:(k + 1) * d].reshape(b, 1, d) for k in range(6)]
        cx = [m[b:b + 1, k * d:(k + 1) * d].reshape(1, 1, d) for k in range(6)]
        return lat, cx

    def tiling(rows, resident=False):
        if rows == s:
            return dict(tm=tm_r if resident else tm, rows_per_batch=s)
        return dict(tm=tm_rc if resident else tm_c, rows_per_batch=None)

    def proj(xs, w, n1, sh, sc, rows, tn, name, **kw):
        return _mm(xs, w, tn=tn, out_dtype=BF16, prologue="norm_mod", gain=n1, shift=sh, scale=sc, name=name,
                   **tiling(rows), **kw)

    def qkv_proj(xs, w, n1, sh, sc, col_gain, rows, name, **kw):
        return _qkv_proj(xs, w, n1, sh, sc, col_gain, tn=math.gcd(w.shape[1], 1024), name=name, **tiling(rows), **kw)

    def out_resid(y, w, xs, gate, rows):
        return _mm(y, w.astype(BF16), tn=w.shape[1], out_dtype=F32, epilogue="resid", res=xs, gate=gate,
                   name="out_resid", **tiling(rows, resident=True))

    def ffn(xs, n2, sh, sc, gate, w_bf16, rows, next_w=()):
        return _ffn(xs, n2, sh, sc, gate, *w_bf16, th=th, name="ffn", cast=next_w, **tiling(rows))

    ffn_w = (l0_ffn_w_in.astype(BF16), l0_ffn_w_out.astype(BF16))

    (sh1, sc1, g1, sh2, sc2, g2), (csh1, csc1, cg1, csh2, csc2, cg2) = modulation(l0_mod_w, l0_mod_b)
    dk = l0_gla_wg_f.shape[1] // GLA_HEADS
    dv = d // GLA_HEADS
    n_in = l0_gla_w_in.shape[1]
    tn0 = 1280
    n_pad = -(-(n_in - 2 * GLA_RANK + LANES) // tn0) * tn0
    w0 = jnp.pad(l0_gla_w_in, ((0, 0), (0, n_pad - n_in))).astype(BF16)
    pl0 = proj(xl, w0, l0_norm1, sh1, sc1, s, tn0, "gla_proj")
    pc0 = proj(xc, w0, l0_norm1, csh1, csc1, n_ctx, tn0, "gla_proj_ctx")
    wgf, bgf = _gate_weights(l0_gla_wg_f, l0_gla_bg_f, 0, dk)
    wgb, bgb = _gate_weights(l0_gla_wg_b, l0_gla_bg_b, GLA_RANK, dk)
    yc, yl = _gla(pc0, pl0, wgf, bgf, wgb, bgb, l0_gla_out_norm, batch=b, dk=dk, dv=dv)
    xl = out_resid(yl, l0_gla_w_out, xl, g1, s)
    xc = out_resid(yc, l0_gla_w_out, xc, cg1, n_ctx)
    xl, next_ffn_w = ffn(xl, l0_norm2, sh2, sc2, g2, ffn_w, s, next_w=(l1_ffn_w_in, l1_ffn_w_out))
    xc, _ = ffn(xc, l0_norm2, csh2, csc2, cg2, ffn_w, n_ctx)
    ffn_w = next_ffn_w

    (sh1, sc1, g1, sh2, sc2, g2), (csh1, csc1, cg1, csh2, csc2, cg2) = modulation(l1_mod_w, l1_mod_b)
    n_heads = d // HEAD_DIM
    cos, sin = _rope_tables(s)
    kv_cols = GQA_KV_HEADS * HEAD_DIM
    cg_gqa = _column_gains(l1_gqa_q_norm, l1_gqa_k_norm, d, kv_cols, kv_cols)
    w1 = l1_gqa_w_in.astype(BF16)
    pl1 = qkv_proj(xl, w1, l1_norm1, sh1, sc1, cg_gqa, s, "gqa_proj", normed_cols=d + kv_cols, cos=cos, sin=sin)
    pc1 = qkv_proj(xc, w1, l1_norm1, csh1, csc1, cg_gqa, n_ctx, "gqa_proj_ctx", normed_cols=d + kv_cols)
    yl = _gqa_attention(pl1, [pl1, pc1], batch=b, n_q_heads=n_heads, n_kv_heads=GQA_KV_HEADS, tq=tq_gqa, q_rows=s,
                        name="gqa_attn")
    yc = _gqa_attention(pc1, [pc1], batch=b, n_q_heads=n_heads, n_kv_heads=GQA_KV_HEADS, tq=n_ctx, q_rows=n_ctx,
                        name="gqa_attn_ctx")
    xl = out_resid(yl, l1_gqa_w_out, xl, g1, s)
    xc = out_resid(yc, l1_gqa_w_out, xc, cg1, n_ctx)
    xl, next_ffn_w = ffn(xl, l1_norm2, sh2, sc2, g2, ffn_w, s, next_w=(l2_ffn_w_in, l2_ffn_w_out))
    xc, _ = ffn(xc, l1_norm2, csh2, csc2, cg2, ffn_w, n_ctx)
    ffn_w = next_ffn_w

    (sh1, sc1, g1, sh2, sc2, g2), (csh1, csc1, _, _, _, _) = modulation(l2_mod_w, l2_mod_b)
    lam_init = 0.8 - 0.6 * math.exp(-0.3 * 2)
    cg_diff = _column_gains(l2_diff_q_norm, l2_diff_k_norm, d, d, d)
    w2 = l2_diff_w_in.astype(BF16)
    pl2 = qkv_proj(xl, w2, l2_norm1, sh1, sc1, cg_diff, s, "diff_proj", normed_cols=2 * d, cos=cos, sin=sin)
    pc2 = qkv_proj(xc, w2, l2_norm1, csh1, csc1, cg_diff, n_ctx, "diff_proj_ctx", normed_cols=2 * d)
    yl = _diff_attention(pl2, [pl2, pc2], (l2_diff_lq1, l2_diff_lk1, l2_diff_lq2, l2_diff_lk2),
                         l2_diff_out_norm, batch=b, n_heads=n_heads // 2, tq=tq_diff, q_rows=s, lam_init=lam_init,
                         name="diff_attn")
    xl = out_resid(yl, l2_diff_w_out, xl, g1, s)
    xl, ffn_w = ffn(xl, l2_norm2, sh2, sc2, g2, ffn_w, s, next_w=(l3_ffn_w_in, l3_ffn_w_out))

    (sh1, sc1, g1, sh2, sc2, g2), _ = modulation(l3_mod_w, l3_mod_b)
    yl = _fnet(xl, l3_norm1, sh1, sc1, batch=b, tm=tm)
    xl = out_resid(yl, l3_fnet_w_out, xl, g1, s)
    xl, _ = ffn(xl, l3_norm2, sh2, sc2, g2, ffn_w, s)
    return xl.reshape(b, s, d)
```

```python
import functools
import math

import jax
import jax.numpy as jnp
from jax import lax
from jax.experimental import pallas as pl
from jax.experimental.pallas import tpu as pltpu

F32 = jnp.float32
BF16 = jnp.bfloat16

NORM_EPS = 1e-6
ROPE_THETA = 10000.0
GRID_W = 64
HEAD_DIM = 128
GQA_KV_HEADS = 4
GLA_HEADS = 4
GLA_RANK = 16
GLA_TAU = 16.0
GLA_CHUNK = 64
GLA_BLOCK = 4 * GLA_CHUNK
FNET_GROUPS = 4

LANES = 128
BF16_ROWS = 16
STRIP_UNROLL = 8
VMEM_LIMIT = 56 * 1024 * 1024

NT_DIMS = (((1,), (1,)), ((), ()))
TN_DIMS = (((0,), (0,)), ((), ()))


def _params(*sem):
    return pltpu.CompilerParams(dimension_semantics=sem, vmem_limit_bytes=VMEM_LIMIT)


def _rms(x, eps=NORM_EPS):
    return x * lax.rsqrt(jnp.mean(x * x, axis=-1, keepdims=True) + eps)


def _norm_mod_rows(x_ref, xn_ref, gain, shift, scale):
    mult = gain * (1.0 + scale)

    def strip(r, carry):
        rows = pl.ds(pl.multiple_of(r * BF16_ROWS, BF16_ROWS), BF16_ROWS)
        xn_ref[rows, :] = (_rms(x_ref[rows, :]) * mult + shift).astype(BF16)
        return carry

    lax.fori_loop(0, x_ref.shape[0] // BF16_ROWS, strip, 0, unroll=STRIP_UNROLL)


ROPE_STRIP = 256
MXU_WIDTH = 256


def _head_group_matrices():
    src = lax.broadcasted_iota(jnp.int32, (MXU_WIDTH, MXU_WIDTH), 0)
    dst = lax.broadcasted_iota(jnp.int32, (MXU_WIDTH, MXU_WIDTH), 1)
    quarter = HEAD_DIM // 4
    partner = jnp.where((dst // quarter) % 2 == 0, dst + quarter, dst - quarter)
    mean = jnp.where(src // HEAD_DIM == dst // HEAD_DIM, 1.0 / HEAD_DIM, 0.0).astype(BF16)
    return mean, (src == partner).astype(BF16)


def _mm_kernel(*refs, prologue, epilogue):
    it = iter(refs)
    x_ref = next(it)
    if prologue == "norm_mod":
        gain_ref, shift_ref, scale_ref = next(it), next(it), next(it)
    w_ref = next(it)
    if epilogue == "bias":
        b_ref = next(it)
    elif epilogue == "resid":
        res_ref, gate_ref = next(it), next(it)
    o_ref = next(it)
    if prologue != "cast":
        xn_ref = next(it)
    j = pl.program_id(1)

    if prologue == "cast":
        a = x_ref[...].astype(BF16)
    else:
        @pl.when(j == 0)
        def _():
            if prologue == "norm_mod":
                _norm_mod_rows(x_ref, xn_ref, gain_ref[...], shift_ref[0], scale_ref[0])
            else:
                xn_ref[...] = jax.nn.silu(x_ref[...]).astype(BF16)

        a = xn_ref[...]
    acc = jnp.dot(a, w_ref[...].astype(BF16), preferred_element_type=F32)
    if epilogue == "store":
        o_ref[...] = acc.astype(o_ref.dtype)
    elif epilogue == "bias":
        o_ref[...] = (acc + b_ref[...]).astype(o_ref.dtype)
    elif epilogue == "resid":
        o_ref[...] = (res_ref[...] + gate_ref[0] * acc).astype(o_ref.dtype)


def _mm(x, w, *, tm, tn, out_dtype, name, prologue="cast", epilogue="store", rows_per_batch=None,
        gain=None, shift=None, scale=None, bias=None, res=None, gate=None):
    t, k = x.shape
    n = w.shape[1]
    assert t % tm == 0 and n % tn == 0, (t, tm, n, tn)
    rows_per_batch = rows_per_batch or t
    assert rows_per_batch % tm == 0
    tiles_per_batch = rows_per_batch // tm
    once = dict(pipeline_mode=pl.Buffered(1))

    def batch_of(i):
        return i // tiles_per_batch

    args = [x]
    specs = [pl.BlockSpec((tm, k), lambda i, j: (i, 0))]
    if prologue == "norm_mod":
        nb = shift.shape[0]
        bsel = (lambda i: batch_of(i)) if nb > 1 else (lambda i: 0)
        args += [gain.reshape(1, k), shift, scale]
        specs += [pl.BlockSpec((1, k), lambda i, j: (0, 0)),
                  pl.BlockSpec((1, 1, k), lambda i, j: (bsel(i), 0, 0)),
                  pl.BlockSpec((1, 1, k), lambda i, j: (bsel(i), 0, 0))]
    args.append(w)
    specs.append(pl.BlockSpec((k, tn), lambda i, j: (0, j), **(once if tn == n else {})))
    if epilogue == "bias":
        args.append(bias.reshape(1, n))
        specs.append(pl.BlockSpec((1, tn), lambda i, j: (0, j)))
    elif epilogue == "resid":
        nb = gate.shape[0]
        gsel = (lambda i: batch_of(i)) if nb > 1 else (lambda i: 0)
        args += [res, gate]
        specs += [pl.BlockSpec((tm, tn), lambda i, j: (i, j)),
                  pl.BlockSpec((1, 1, tn), lambda i, j: (gsel(i), 0, j))]
    scratch = []
    if prologue != "cast":
        scratch.append(pltpu.VMEM((tm, k), BF16))
    kern = functools.partial(_mm_kernel, prologue=prologue, epilogue=epilogue)
    return pl.pallas_call(
        kern,
        grid=(t // tm, n // tn),
        in_specs=specs,
        out_specs=pl.BlockSpec((tm, tn), lambda i, j: (i, j)),
        out_shape=jax.ShapeDtypeStruct((t, n), out_dtype),
        scratch_shapes=scratch,
        compiler_params=_params("parallel", "arbitrary"),
        name=name,
    )(*args)


def _qkv_kernel(*refs, normed_cols, rope):
    it = iter(refs)
    x_ref, gain_ref, shift_ref, scale_ref, w_ref, hg_ref, ones_ref = [next(it) for _ in range(7)]
    if rope:
        perm_ref, cos_ref, sin_ref = next(it), next(it), next(it)
    o_ref, xn_ref = next(it), next(it)
    j = pl.program_id(1)
    tm, tn = o_ref.shape
    gw = ones_ref.shape[0]
    n_sub = tn // gw
    full_tiles, rem = divmod(normed_cols, tn)

    @pl.when(j == 0)
    def _():
        _norm_mod_rows(x_ref, xn_ref, gain_ref[...], shift_ref[0], scale_ref[0])

    def norm_rope_store(acc, cols):
        for r0 in range(0, tm, ROPE_STRIP):
            rows = slice(r0, r0 + ROPE_STRIP)
            blk = acc[rows, :]
            ms = jnp.dot((blk * blk).astype(BF16), ones_ref[...], preferred_element_type=F32)
            y = blk * lax.rsqrt(ms + NORM_EPS) * hg_ref[:, cols]
            if rope:
                partner = jnp.dot(y.astype(BF16), perm_ref[...], preferred_element_type=F32)
                reps = gw // HEAD_DIM
                y = (y * jnp.concatenate([cos_ref[rows, :]] * reps, axis=-1)
                     + partner * jnp.concatenate([sin_ref[rows, :]] * reps, axis=-1))
            o_ref[rows, cols] = y.astype(o_ref.dtype)

    def tile(n_normed_subs):
        a = xn_ref[...]
        if n_normed_subs == 0:
            o_ref[...] = jnp.dot(a, w_ref[...], preferred_element_type=F32).astype(o_ref.dtype)
            return
        sub_cols = [slice(s * gw, (s + 1) * gw) for s in range(n_sub)]
        acc = jnp.dot(a, w_ref[:, sub_cols[0]], preferred_element_type=F32)
        for s in range(n_sub):
            nxt = jnp.dot(a, w_ref[:, sub_cols[s + 1]], preferred_element_type=F32) if s + 1 < n_sub else None
            if s < n_normed_subs:
                norm_rope_store(acc, sub_cols[s])
            else:
                o_ref[:, sub_cols[s]] = acc.astype(o_ref.dtype)
            acc = nxt

    if full_tiles:
        pl.when(j < full_tiles)(functools.partial(tile, n_sub))
    if rem:
        pl.when(j == full_tiles)(functools.partial(tile, rem // gw))
    pl.when(j >= full_tiles + (1 if rem else 0))(functools.partial(tile, 0))


def _qkv_proj(x, w, gain, shift, scale, col_gain, *, normed_cols, tm, tn, name, rows_per_batch=None,
              cos=None, sin=None):
    t, k = x.shape
    n = w.shape[1]
    assert t % tm == 0 and n % tn == 0 and normed_cols % MXU_WIDTH == 0
    rows_per_batch = rows_per_batch or t
    assert rows_per_batch % tm == 0
    tiles_per_batch = rows_per_batch // tm
    rope = cos is not None
    nb = shift.shape[0]
    bsel = (lambda i: i // tiles_per_batch) if nb > 1 else (lambda i: 0)
    ones, perm = _head_group_matrices()
    const = pl.BlockSpec(ones.shape, lambda i, j: (0, 0))
    vec = pl.BlockSpec((1, 1, k), lambda i, j: (bsel(i), 0, 0))
    args = [x, gain.reshape(1, k), shift, scale, w, col_gain, ones]
    specs = [pl.BlockSpec((tm, k), lambda i, j: (i, 0)),
             pl.BlockSpec((1, k), lambda i, j: (0, 0)), vec, vec,
             pl.BlockSpec((k, tn), lambda i, j: (0, j)),
             pl.BlockSpec((1, tn), lambda i, j: (0, j)), const]
    if rope:
        table = pl.BlockSpec((tm, HEAD_DIM), lambda i, j: (i % tiles_per_batch, 0))
        args += [perm, cos, sin]
        specs += [const, table, table]
    return pl.pallas_call(
        functools.partial(_qkv_kernel, normed_cols=normed_cols, rope=rope),
        grid=(t // tm, n // tn),
        in_specs=specs,
        out_specs=pl.BlockSpec((tm, tn), lambda i, j: (i, j)),
        out_shape=jax.ShapeDtypeStruct((t, n), BF16),
        scratch_shapes=[pltpu.VMEM((tm, k), BF16)],
        compiler_params=_params("parallel", "arbitrary"),
        name=name,
    )(*args)


FFN_HID_SPLIT = 2


def _ffn_kernel(*refs, n_cast):
    x_ref, gain_ref, shift_ref, scale_ref, gate_ref, wg_ref, wu_ref, wo_ref = refs[:8]
    cast_src = refs[8:8 + n_cast]
    o_ref = refs[8 + n_cast]
    cast_dst = refs[9 + n_cast:9 + 2 * n_cast]
    xn_ref = refs[9 + 2 * n_cast]
    j = pl.program_id(1)
    hc = wg_ref.shape[1] // FFN_HID_SPLIT
    cols = [slice(c * hc, (c + 1) * hc) for c in range(FFN_HID_SPLIT)]
    for src, dst in zip(cast_src, cast_dst):
        dst[...] = src[...].astype(dst.dtype)

    def chunk(first, last):
        if first:
            _norm_mod_rows(x_ref, xn_ref, gain_ref[...], shift_ref[0], scale_ref[0])
        xn = xn_ref[...]
        gu = [(jnp.dot(xn, wg_ref[:, cs], preferred_element_type=F32),
               jnp.dot(xn, wu_ref[:, cs], preferred_element_type=F32)) for cs in cols]
        for c, (g, u) in enumerate(gu):
            a = (jax.nn.silu(g) * u).astype(BF16)
            part = jnp.dot(a, wo_ref[cols[c], :], preferred_element_type=F32)
            if first and c == 0:
                o_ref[...] = part
            elif last and c == len(gu) - 1:
                o_ref[...] = x_ref[...] + gate_ref[0] * (o_ref[...] + part)
            else:
                o_ref[...] += part

    n_chunks = pl.num_programs(1)
    pl.when(j == 0)(functools.partial(chunk, True, False))
    pl.when((j > 0) & (j < n_chunks - 1))(functools.partial(chunk, False, False))
    pl.when(j == n_chunks - 1)(functools.partial(chunk, False, True))


def _ffn(x, gain, shift, scale, gate, w_in, w_out, *, tm, th, name, rows_per_batch=None, cast=()):
    t, d = x.shape
    hidden = w_out.shape[0]
    assert t % tm == 0 and hidden % th == 0 and hidden // th >= 2
    nh = hidden // th
    n_steps = (t // tm) * nh
    rows_per_batch = rows_per_batch or t
    tiles_per_batch = rows_per_batch // tm
    nb = shift.shape[0]
    bsel = (lambda i: i // tiles_per_batch) if nb > 1 else (lambda i: 0)
    vec = pl.BlockSpec((1, 1, d), lambda i, j: (bsel(i), 0, 0))
    slabs = list(cast)
    slab_specs = []
    if slabs:
        nw_in, nw_out = slabs
        in_rows, in_cols = nw_in.shape[0] // (t // tm), nw_in.shape[1] // nh
        out_rows = nw_out.shape[0] // n_steps
        assert in_rows * (t // tm) == nw_in.shape[0] and in_cols * nh == nw_in.shape[1] and in_cols % LANES == 0
        assert out_rows * n_steps == nw_out.shape[0] and in_rows % BF16_ROWS == 0 and out_rows % BF16_ROWS == 0
        slab_specs = [pl.BlockSpec((in_rows, in_cols), lambda i, j: (i, j)),
                      pl.BlockSpec((out_rows, nw_out.shape[1]), lambda i, j: (i * nh + j, 0))]
    outs = pl.pallas_call(
        functools.partial(_ffn_kernel, n_cast=len(slabs)),
        grid=(t // tm, nh),
        in_specs=[pl.BlockSpec((tm, d), lambda i, j: (i, 0)),
                  pl.BlockSpec((1, d), lambda i, j: (0, 0)),
                  vec, vec, vec,
                  pl.BlockSpec((d, th), lambda i, j: (0, j)),
                  pl.BlockSpec((d, th), lambda i, j: (0, nh + j)),
                  pl.BlockSpec((th, d), lambda i, j: (j, 0))] + slab_specs,
        out_specs=[pl.BlockSpec((tm, d), lambda i, j: (i, 0))] + slab_specs,
        out_shape=[jax.ShapeDtypeStruct((t, d), F32)] + [jax.ShapeDtypeStruct(s.shape, BF16) for s in slabs],
        scratch_shapes=[pltpu.VMEM((tm, d), BF16)],
        compiler_params=_params("parallel", "arbitrary"),
        name=name,
    )(x, gain.reshape(1, d), shift, scale, gate, w_in, w_in, w_out, *slabs)
    return outs[0], tuple(outs[1:])


def _fill_kv(k_s, v_s, kv_refs):
    off = 0
    for k_ref, v_ref in kv_refs:
        n = k_ref.shape[0]
        k_s[off:off + n, :] = k_ref[...]
        v_s[off:off + n, :] = v_ref[...]
        off += n


ATTN_SUBTILE = 256


def _gqa_kernel(*refs, n_kv_src, n_group):
    q_ref = refs[0]
    kv_refs = [(refs[1 + 2 * s], refs[2 + 2 * s]) for s in range(n_kv_src)]
    o_ref, k_s, v_s = refs[1 + 2 * n_kv_src:]

    @pl.when(pl.program_id(2) == 0)
    def _():
        _fill_kv(k_s, v_s, kv_refs)

    k = k_s[...]
    v = v_s[...]
    sub = min(ATTN_SUBTILE, q_ref.shape[0])
    chains = [(slice(r0, r0 + sub), slice(g * HEAD_DIM, (g + 1) * HEAD_DIM))
              for r0 in range(0, q_ref.shape[0], sub) for g in range(n_group)]

    def scores(c):
        return lax.dot_general(q_ref[chains[c]], k, NT_DIMS, preferred_element_type=F32)

    s = scores(0)
    for c in range(len(chains)):
        s_next = scores(c + 1) if c + 1 < len(chains) else None
        p = jnp.exp2(s - jnp.max(s, axis=-1, keepdims=True))
        l = jnp.sum(p, axis=-1, keepdims=True)
        o = jnp.dot(p.astype(BF16), v, preferred_element_type=F32)
        o_ref[chains[c]] = (o / l).astype(o_ref.dtype)
        s = s_next


def _gqa_attention(q_src, kv_srcs, *, batch, n_q_heads, n_kv_heads, tq, q_rows, name):
    group = n_q_heads // n_kv_heads
    gw = group * HEAD_DIM
    nq = q_rows // tq
    k_blk0 = n_q_heads
    v_blk0 = n_q_heads + n_kv_heads
    specs = [pl.BlockSpec((tq, gw), lambda b, h, i: (b * nq + i, h))]
    args = [q_src]
    total = 0
    for src in kv_srcs:
        rows = src.shape[0] // batch
        total += rows
        specs += [pl.BlockSpec((rows, HEAD_DIM), lambda b, h, i: (b, k_blk0 + h)),
                  pl.BlockSpec((rows, HEAD_DIM), lambda b, h, i: (b, v_blk0 + h))]
        args += [src, src]
    kern = functools.partial(_gqa_kernel, n_kv_src=len(kv_srcs), n_group=group)
    return pl.pallas_call(
        kern,
        grid=(batch, n_kv_heads, nq),
        in_specs=specs,
        out_specs=pl.BlockSpec((tq, gw), lambda b, h, i: (b * nq + i, h)),
        out_shape=jax.ShapeDtypeStruct((batch * q_rows, n_q_heads * HEAD_DIM), BF16),
        scratch_shapes=[pltpu.VMEM((total, HEAD_DIM), BF16), pltpu.VMEM((total, HEAD_DIM), BF16)],
        compiler_params=_params("parallel", "parallel", "arbitrary"),
        name=name,
    )(*args)


def _diff_kernel(*refs, n_kv_src, lam_init):
    q_ref = refs[0]
    kv_refs = [(refs[1 + 2 * s], refs[2 + 2 * s]) for s in range(n_kv_src)]
    lq1, lk1, lq2, lk2, gain_ref, o_ref, k_s, v_s = refs[1 + 2 * n_kv_src:]

    @pl.when(pl.program_id(2) == 0)
    def _():
        _fill_kv(k_s, v_s, kv_refs)

    lam = (jnp.exp(jnp.sum(lq1[...] * lk1[...], axis=-1, keepdims=True))
           - jnp.exp(jnp.sum(lq2[...] * lk2[...], axis=-1, keepdims=True)) + lam_init)
    dh = HEAD_DIM
    sub = min(ATTN_SUBTILE, q_ref.shape[0])
    chains = [slice(r0, r0 + sub) for r0 in range(0, q_ref.shape[0], sub)]

    def scores(c):
        return [lax.dot_general(q_ref[chains[c], m * dh:(m + 1) * dh], k_s[:, m * dh:(m + 1) * dh], NT_DIMS,
                                preferred_element_type=F32) for m in range(2)]

    s = scores(0)
    for c in range(len(chains)):
        s_next = scores(c + 1) if c + 1 < len(chains) else None
        p = [jnp.exp2(sm - jnp.max(sm, axis=-1, keepdims=True)) for sm in s]
        l0, l1 = [jnp.sum(pm, axis=-1, keepdims=True) for pm in p]
        w = p[0] - p[1] * (lam * l0 / l1)
        o = jnp.dot(w.astype(BF16), v_s[...], preferred_element_type=F32) * (1.0 / l0)
        o_ref[chains[c], :] = ((_rms(o) * gain_ref[...]) * (1.0 - lam_init)).astype(o_ref.dtype)
        s = s_next


def _diff_attention(q_src, kv_srcs, lams, out_gain, *, batch, n_heads, tq, q_rows, lam_init, name):
    hw = 2 * HEAD_DIM
    nq = q_rows // tq
    specs = [pl.BlockSpec((tq, hw), lambda b, h, i: (b * nq + i, h))]
    args = [q_src]
    total = 0
    for src in kv_srcs:
        rows = src.shape[0] // batch
        total += rows
        specs += [pl.BlockSpec((rows, hw), lambda b, h, i: (b, n_heads + h)),
                  pl.BlockSpec((rows, hw), lambda b, h, i: (b, 2 * n_heads + h))]
        args += [src, src]
    small = pl.BlockSpec((1, HEAD_DIM), lambda b, h, i: (0, 0))
    specs += [small] * 4 + [pl.BlockSpec((1, hw), lambda b, h, i: (0, 0))]
    args += [v.reshape(1, HEAD_DIM) for v in lams] + [out_gain.reshape(1, hw)]
    kern = functools.partial(_diff_kernel, n_kv_src=len(kv_srcs), lam_init=lam_init)
    return pl.pallas_call(
        kern,
        grid=(batch, n_heads, nq),
        in_specs=specs,
        out_specs=pl.BlockSpec((tq, hw), lambda b, h, i: (b * nq + i, h)),
        out_shape=jax.ShapeDtypeStruct((batch * q_rows, n_heads * hw), BF16),
        scratch_shapes=[pltpu.VMEM((total, hw), BF16), pltpu.VMEM((total, hw), BF16)],
        compiler_params=_params("parallel", "parallel", "arbitrary"),
        name=name,
    )(*args)


def _gla_blocks(blocks):
    r, dk = blocks[0][0].shape
    c = GLA_CHUNK
    nc = r // c
    n = range(len(blocks))
    qs_, ks_, vs_, zs_, wgs, bgs, st_refs, revs = zip(*blocks)
    row = lax.broadcasted_iota(jnp.int32, (r, r), 0)
    col = lax.broadcasted_iota(jnp.int32, (r, r), 1)
    same = row // c == col // c
    tri_f = [jnp.where(same, jnp.where((col >= row) if revs[i] else (col <= row), 1.0, 0.0), 0.0) for i in n]
    tri = [t.astype(BF16) for t in tri_f]
    mid = [c // 2 if revs[i] else c // 2 - 1 for i in n]
    last = [0 if revs[i] else c - 1 for i in n]

    g = [jax.nn.log_sigmoid(jnp.dot(zs_[i], wgs[i], preferred_element_type=F32) + bgs[i]) / GLA_TAU for i in n]
    g_hi = [g[i].astype(BF16) for i in n]
    g_lo = [(g[i] - g_hi[i].astype(F32)).astype(BF16) for i in n]
    cum = [jnp.dot(tri[i], g_hi[i], preferred_element_type=F32) + jnp.dot(tri[i], g_lo[i], preferred_element_type=F32)
           for i in n]

    def chunk_row(x, idx):
        return jnp.concatenate(
            [jnp.broadcast_to(x[ci * c + idx:ci * c + idx + 1, :], (c, dk)) for ci in range(nc)], axis=0)

    cum_mid = [chunk_row(cum[i], mid[i]) for i in n]
    cum_last = [chunk_row(cum[i], last[i]) for i in n]
    qs = [(qs_[i] * jnp.exp(cum[i] - cum_mid[i])).astype(BF16) for i in n]
    ks = [(ks_[i] * jnp.exp(cum_mid[i] - cum[i])).astype(BF16) for i in n]
    a = [lax.dot_general(qs[i], ks[i], NT_DIMS, preferred_element_type=F32) for i in n]
    a = [jnp.where(tri_f[i] > 0.5, a[i], 0.0).astype(BF16) for i in n]
    o_intra = [jnp.dot(a[i], vs_[i], preferred_element_type=F32) for i in n]
    q_inter = [(qs_[i] * jnp.exp(cum[i])).astype(BF16) for i in n]
    k_carry = [(ks_[i] * jnp.exp(cum_last[i] - cum[i])).astype(BF16) for i in n]
    st = [st_refs[i][...] for i in n]
    o_inter = [[None] * nc for _ in n]
    for step in range(nc):
        for i in n:
            ci = nc - 1 - step if revs[i] else step
            rows = slice(ci * c, (ci + 1) * c)
            o_inter[i][ci] = lax.dot_general(q_inter[i][rows], st[i].astype(BF16), NT_DIMS,
                                             preferred_element_type=F32)
            decay = jnp.exp(cum[i][ci * c + last[i]:ci * c + last[i] + 1, :])
            st[i] = st[i] * decay + lax.dot_general(vs_[i][rows], k_carry[i][rows], TN_DIMS,
                                                    preferred_element_type=F32)
    for i in n:
        st_refs[i][...] = st[i]
    return [o_intra[i] + jnp.concatenate(o_inter[i], axis=0) for i in n]


def _gla_kernel(qc_ref, kc_ref, vc_ref, rc_ref, zc_ref, ql_ref, kl_ref, vl_ref, rl_ref, zl_ref,
                wgf_ref, bgf_ref, wgb_ref, bgb_ref, gain_ref, oc_ref, ol_ref, sf_ref, sb_ref, of_ref, ob_ref):
    blk = GLA_BLOCK
    n_ctx = qc_ref.shape[0] // blk
    n_lat = ql_ref.shape[0] // blk
    q_scale = qc_ref.shape[1] ** -0.5
    ctx_refs = (qc_ref, kc_ref, vc_ref, zc_ref)
    lat_refs = (ql_ref, kl_ref, vl_ref, zl_ref)

    def rows_of(bi):
        return pl.ds(pl.multiple_of(bi * blk, blk), blk)

    def block(refs, bi, reverse):
        q_ref, k_ref, v_ref, z_ref = refs
        rows = rows_of(bi)
        wg_ref, bg_ref, st_ref = (wgb_ref, bgb_ref, sb_ref) if reverse else (wgf_ref, bgf_ref, sf_ref)
        return (q_ref[rows, :].astype(F32) * q_scale, k_ref[rows, :].astype(F32), v_ref[rows, :], z_ref[rows, :],
                wg_ref[0], bg_ref[0], st_ref, reverse)

    def both(refs, base, n):
        def body(t, carry):
            o_f, o_b = _gla_blocks([block(refs, t, False), block(refs, n - 1 - t, True)])
            of_ref[rows_of(base + t), :] = o_f
            ob_ref[rows_of(base + n - 1 - t), :] = o_b
            return carry
        return body

    def finish(r_ref, out_ref, base):
        def body(bi, carry):
            rows = rows_of(bi)
            o = of_ref[rows_of(base + bi), :] + ob_ref[rows_of(base + bi), :]
            y = (_rms(o) * gain_ref[...]) * jax.nn.silu(r_ref[rows, :].astype(F32))
            out_ref[rows, :] = y.astype(out_ref.dtype)
            return carry
        return body

    sf_ref[...] = jnp.zeros_like(sf_ref)
    sb_ref[...] = jnp.zeros_like(sb_ref)
    lax.fori_loop(0, n_ctx, both(ctx_refs, 0, n_ctx), 0)
    lax.fori_loop(0, n_lat, both(lat_refs, n_ctx, n_lat), 0)
    lax.fori_loop(0, n_ctx, finish(rc_ref, oc_ref, 0), 0)
    lax.fori_loop(0, n_lat, finish(rl_ref, ol_ref, n_ctx), 0)


def _gla(p_ctx, p_lat, wgf, bgf, wgb, bgb, out_gain, *, batch, dk, dv):
    h = GLA_HEADS
    rows_c = p_ctx.shape[0] // batch
    rows_l = p_lat.shape[0] // batch
    assert rows_c % GLA_BLOCK == 0 and rows_l % GLA_BLOCK == 0
    zblk = (2 * h * dk + 2 * h * dv) // LANES
    k0 = h
    v0 = (2 * h * dk) // dv
    r0 = v0 + h

    def stream(rows):
        return [pl.BlockSpec((rows, dk), lambda b, hh: (b, hh)),
                pl.BlockSpec((rows, dk), lambda b, hh: (b, k0 + hh)),
                pl.BlockSpec((rows, dv), lambda b, hh: (b, v0 + hh)),
                pl.BlockSpec((rows, dv), lambda b, hh: (b, r0 + hh)),
                pl.BlockSpec((rows, LANES), lambda b, hh: (b, zblk))]

    wspec = pl.BlockSpec((1, LANES, dk), lambda b, hh: (hh, 0, 0))
    bspec = pl.BlockSpec((1, 1, dk), lambda b, hh: (hh, 0, 0))
    return pl.pallas_call(
        _gla_kernel,
        grid=(batch, h),
        in_specs=stream(rows_c) + stream(rows_l) + [wspec, bspec, wspec, bspec,
                                                    pl.BlockSpec((1, dv), lambda b, hh: (0, 0))],
        out_specs=[pl.BlockSpec((rows_c, dv), lambda b, hh: (b, hh)),
                   pl.BlockSpec((rows_l, dv), lambda b, hh: (b, hh))],
        out_shape=[jax.ShapeDtypeStruct((batch * rows_c, h * dv), BF16),
                   jax.ShapeDtypeStruct((batch * rows_l, h * dv), BF16)],
        scratch_shapes=[pltpu.VMEM((dv, dk), F32), pltpu.VMEM((dv, dk), F32),
                        pltpu.VMEM((rows_c + rows_l, dv), F32), pltpu.VMEM((rows_c + rows_l, dv), F32)],
        compiler_params=_params("parallel", "parallel"),
        name="gla",
    )(*([p_ctx] * 5 + [p_lat] * 5 + [wgf, bgf, wgb, bgb, out_gain.reshape(1, dv)]))


def _dft_tables(n):
    idx = jnp.arange(n, dtype=jnp.int32)
    ang = ((idx[:, None] * idx[None, :]) % n).astype(F32) * (2.0 * math.pi / n)
    return jnp.cos(ang).astype(BF16), jnp.sin(ang).astype(BF16)


def _fnet_chan_kernel(x_ref, gain_ref, shift_ref, scale_ref, csc_ref, p_ref, q_ref, xn_ref):
    gd = csc_ref.shape[0]
    _norm_mod_rows(x_ref, xn_ref, gain_ref[...], shift_ref[0], scale_ref[0])
    for g in range(x_ref.shape[1] // gd):
        cols = slice(g * gd, (g + 1) * gd)
        pq = jnp.dot(xn_ref[:, cols], csc_ref[...], preferred_element_type=F32)
        p_ref[:, cols] = pq[:, :gd].astype(p_ref.dtype)
        q_ref[:, cols] = pq[:, gd:].astype(q_ref.dtype)


def _fnet_seq_kernel(cs_ref, ss_ref, p_ref, q_ref, o_ref, *, inv_norm):
    acc = (jnp.dot(cs_ref[...], p_ref[...], preferred_element_type=F32)
           - jnp.dot(ss_ref[...], q_ref[...], preferred_element_type=F32))
    o_ref[...] = (acc * inv_norm).astype(o_ref.dtype)


def _fnet(x, gain, shift, scale, *, batch, tm):
    t, d = x.shape
    s = t // batch
    gd = d // FNET_GROUPS
    tiles_per_batch = s // tm
    cc, sc = _dft_tables(gd)
    cs, ss = _dft_tables(s)
    vec = pl.BlockSpec((1, 1, d), lambda i: (i // tiles_per_batch, 0, 0))
    blk = pl.BlockSpec((tm, d), lambda i: (i, 0))
    p, q = pl.pallas_call(
        _fnet_chan_kernel,
        grid=(t // tm,),
        in_specs=[blk, pl.BlockSpec((1, d), lambda i: (0, 0)), vec, vec,
                  pl.BlockSpec((gd, 2 * gd), lambda i: (0, 0))],
        out_specs=[blk, blk],
        out_shape=[jax.ShapeDtypeStruct((t, d), BF16)] * 2,
        scratch_shapes=[pltpu.VMEM((tm, d), BF16)],
        compiler_params=_params("parallel"),
        name="fnet_chan",
    )(x, gain.reshape(1, d), shift, scale, jnp.concatenate([cc, sc], axis=1))
    rows = pl.BlockSpec((tm, s), lambda b, j, i: (i, 0))
    cols = pl.BlockSpec((s, gd), lambda b, j, i: (b, j))
    return pl.pallas_call(
        functools.partial(_fnet_seq_kernel, inv_norm=float((s * gd) ** -0.5)),
        grid=(batch, d // gd, tiles_per_batch),
        in_specs=[rows, rows, cols, cols],
        out_specs=pl.BlockSpec((tm, gd), lambda b, j, i: (b * tiles_per_batch + i, j)),
        out_shape=jax.ShapeDtypeStruct((t, d), BF16),
        compiler_params=_params("parallel", "parallel", "arbitrary"),
        name="fnet_seq",
    )(cs, ss, p, q)


def _rope_tables(n_tokens):
    t = jnp.arange(n_tokens)
    row = (t // GRID_W).astype(F32)
    col = (t % GRID_W).astype(F32)
    half = HEAD_DIM // 2
    inv_freq = ROPE_THETA ** (-jnp.arange(0, half, 2, dtype=F32) / half)
    ang_r = row[:, None] * inv_freq[None, :]
    ang_c = col[:, None] * inv_freq[None, :]
    ang = jnp.concatenate([ang_r, ang_r, ang_c, ang_c], axis=-1)
    sign = jnp.concatenate([-jnp.ones((half // 2,), F32), jnp.ones((half // 2,), F32)] * 2)
    return jnp.cos(ang), jnp.sin(ang) * sign


def _column_gains(q_gain, k_gain, q_cols, k_cols, v_cols):
    qg = jnp.tile(q_gain.astype(F32) * (HEAD_DIM ** -0.5 * math.log2(math.e)), q_cols // HEAD_DIM)
    kg = jnp.tile(k_gain.astype(F32), k_cols // HEAD_DIM)
    return jnp.concatenate([qg, kg, jnp.ones((v_cols,), F32)])[None, :]


def _gate_weights(wg, bg, lane0, dk):
    r = wg.shape[0]
    w = wg.reshape(r, GLA_HEADS, dk).transpose(1, 0, 2)
    w = jnp.pad(w, ((0, 0), (lane0, LANES - lane0 - r), (0, 0))).astype(BF16)
    return w, bg.reshape(GLA_HEADS, 1, dk).astype(F32)


def kernel(x, c, ctx, c_ctx, l0_mod_w, l0_mod_b, l0_norm1, l0_gla_w_in, l0_gla_wg_f, l0_gla_bg_f, l0_gla_wg_b, l0_gla_bg_b, l0_gla_out_norm, l0_gla_w_out, l0_norm2, l0_ffn_w_in, l0_ffn_w_out, l1_mod_w, l1_mod_b, l1_norm1, l1_gqa_w_in, l1_gqa_q_norm, l1_gqa_k_norm, l1_gqa_w_out, l1_norm2, l1_ffn_w_in, l1_ffn_w_out, l2_mod_w, l2_mod_b, l2_norm1, l2_diff_w_in, l2_diff_q_norm, l2_diff_k_norm, l2_diff_lq1, l2_diff_lk1, l2_diff_lq2, l2_diff_lk2, l2_diff_out_norm, l2_diff_w_out, l2_norm2, l2_ffn_w_in, l2_ffn_w_out, l3_mod_w, l3_mod_b, l3_norm1, l3_fnet_w_out, l3_norm2, l3_ffn_w_in, l3_ffn_w_out):
    b, s, d = x.shape
    n_ctx = ctx.shape[1]
    xl = x.reshape(b * s, d)
    xc = ctx.reshape(b * n_ctx, d)
    tm = math.gcd(s, 1024)
    tm_c = math.gcd(b * n_ctx, 1024)
    tm_r = math.gcd(s, 512)
    tm_rc = math.gcd(b * n_ctx, 512)
    th = math.gcd(l0_ffn_w_out.shape[0], 512)
    tq_gqa = math.gcd(s, 4 * ATTN_SUBTILE)
    tq_diff = math.gcd(s, 4 * ATTN_SUBTILE)

    n_cond = -(-(b + 1) // BF16_ROWS) * BF16_ROWS
    cond = jnp.concatenate([c, c_ctx[None, :], jnp.zeros((n_cond - b - 1, d), F32)], axis=0)

    def modulation(mod_w, mod_b):
        m = _mm(cond, mod_w, tm=n_cond, tn=math.gcd(mod_w.shape[1], 1024), out_dtype=F32, prologue="silu", epilogue="bias", bias=mod_b,
                name="modulation")
        lat = [m[:b, k * d:(k + 1) * d].reshape(b, 1, d) for k in range(6)]
        cx = [m[b:b + 1, k * d:(k + 1) * d].reshape(1, 1, d) for k in range(6)]
        return lat, cx

    def tiling(rows, resident=False):
        if rows == s:
            return dict(tm=tm_r if resident else tm, rows_per_batch=s)
        return dict(tm=tm_rc if resident else tm_c, rows_per_batch=None)

    def proj(xs, w, n1, sh, sc, rows, tn, name, **kw):
        return _mm(xs, w, tn=tn, out_dtype=BF16, prologue="norm_mod", gain=n1, shift=sh, scale=sc, name=name,
                   **tiling(rows), **kw)

    def qkv_proj(xs, w, n1, sh, sc, col_gain, rows, name, **kw):
        return _qkv_proj(xs, w, n1, sh, sc, col_gain, tn=math.gcd(w.shape[1], 1024), name=name, **tiling(rows), **kw)

    def out_resid(y, w, xs, gate, rows):
        return _mm(y, w.astype(BF16), tn=w.shape[1], out_dtype=F32, epilogue="resid", res=xs, gate=gate,
                   name="out_resid", **tiling(rows, resident=True))

    def ffn(xs, n2, sh, sc, gate, w_bf16, rows, next_w=()):
        return _ffn(xs, n2, sh, sc, gate, *w_bf16, th=th, name="ffn", cast=next_w, **tiling(rows))

    ffn_w = (l0_ffn_w_in.astype(BF16), l0_ffn_w_out.astype(BF16))

    (sh1, sc1, g1, sh2, sc2, g2), (csh1, csc1, cg1, csh2, csc2, cg2) = modulation(l0_mod_w, l0_mod_b)
    dk = l0_gla_wg_f.shape[1] // GLA_HEADS
    dv = d // GLA_HEADS
    n_in = l0_gla_w_in.shape[1]
    tn0 = 1280
    n_pad = -(-(n_in - 2 * GLA_RANK + LANES) // tn0) * tn0
    w0 = jnp.pad(l0_gla_w_in, ((0, 0), (0, n_pad - n_in))).astype(BF16)
    pl0 = proj(xl, w0, l0_norm1, sh1, sc1, s, tn0, "gla_proj")
    pc0 = proj(xc, w0, l0_norm1, csh1, csc1, n_ctx, tn0, "gla_proj_ctx")
    wgf, bgf = _gate_weights(l0_gla_wg_f, l0_gla_bg_f, 0, dk)
    wgb, bgb = _gate_weights(l0_gla_wg_b, l0_gla_bg_b, GLA_RANK, dk)
    yc, yl = _gla(pc0, pl0, wgf, bgf, wgb, bgb, l0_gla_out_norm, batch=b, dk=dk, dv=dv)
    xl = out_resid(yl, l0_gla_w_out, xl, g1, s)
    xc = out_resid(yc, l0_gla_w_out, xc, cg1, n_ctx)
    xl, next_ffn_w = ffn(xl, l0_norm2, sh2, sc2, g2, ffn_w, s, next_w=(l1_ffn_w_in, l1_ffn_w_out))
    xc, _ = ffn(xc, l0_norm2, csh2, csc2, cg2, ffn_w, n_ctx)
    ffn_w = next_ffn_w

    (sh1, sc1, g1, sh2, sc2, g2), (csh1, csc1, cg1, csh2, csc2, cg2) = modulation(l1_mod_w, l1_mod_b)
    n_heads = d // HEAD_DIM
    cos, sin = _rope_tables(s)
    kv_cols = GQA_KV_HEADS * HEAD_DIM
    cg_gqa = _column_gains(l1_gqa_q_norm, l1_gqa_k_norm, d, kv_cols, kv_cols)
    w1 = l1_gqa_w_in.astype(BF16)
    pl1 = qkv_proj(xl, w1, l1_norm1, sh1, sc1, cg_gqa, s, "gqa_proj", normed_cols=d + kv_cols, cos=cos, sin=sin)
    pc1 = qkv_proj(xc, w1, l1_norm1, csh1, csc1, cg_gqa, n_ctx, "gqa_proj_ctx", normed_cols=d + kv_cols)
    yl = _gqa_attention(pl1, [pl1, pc1], batch=b, n_q_heads=n_heads, n_kv_heads=GQA_KV_HEADS, tq=tq_gqa, q_rows=s,
                        name="gqa_attn")
    yc = _gqa_attention(pc1, [pc1], batch=b, n_q_heads=n_heads, n_kv_heads=GQA_KV_HEADS, tq=n_ctx, q_rows=n_ctx,
                        name="gqa_attn_ctx")
    xl = out_resid(yl, l1_gqa_w_out, xl, g1, s)
    xc = out_resid(yc, l1_gqa_w_out, xc, cg1, n_ctx)
    xl, next_ffn_w = ffn(xl, l1_norm2, sh2, sc2, g2, ffn_w, s, next_w=(l2_ffn_w_in, l2_ffn_w_out))
    xc, _ = ffn(xc, l1_norm2, csh2, csc2, cg2, ffn_w, n_ctx)
    ffn_w = next_ffn_w

    (sh1, sc1, g1, sh2, sc2, g2), (csh1, csc1, _, _, _, _) = modulation(l2_mod_w, l2_mod_b)
    lam_init = 0.8 - 0.6 * math.exp(-0.3 * 2)
    cg_diff = _column_gains(l2_diff_q_norm, l2_diff_k_norm, d, d, d)
    w2 = l2_diff_w_in.astype(BF16)
    pl2 = qkv_proj(xl, w2, l2_norm1, sh1, sc1, cg_diff, s, "diff_proj", normed_cols=2 * d, cos=cos, sin=sin)
    pc2 = qkv_proj(xc, w2, l2_norm1, csh1, csc1, cg_diff, n_ctx, "diff_proj_ctx", normed_cols=2 * d)
    yl = _diff_attention(pl2, [pl2, pc2], (l2_diff_lq1, l2_diff_lk1, l2_diff_lq2, l2_diff_lk2),
                         l2_diff_out_norm, batch=b, n_heads=n_heads // 2, tq=tq_diff, q_rows=s, lam_init=lam_init,
                         name="diff_attn")
    xl = out_resid(yl, l2_diff_w_out, xl, g1, s)
    xl, ffn_w = ffn(xl, l2_norm2, sh2, sc2, g2, ffn_w, s, next_w=(l3_ffn_w_in, l3_ffn_w_out))

    (sh1, sc1, g1, sh2, sc2, g2), _ = modulation(l3_mod_w, l3_mod_b)
    yl = _fnet(xl, l3_norm1, sh1, sc1, batch=b, tm=tm)
    xl = out_resid(yl, l3_fnet_w_out, xl, g1, s)
    xl, _ = ffn(xl, l3_norm2, sh2, sc2, g2, ffn_w, s)
    return xl.reshape(b, s, d)
```

```python
import functools
import math

import jax
import jax.numpy as jnp
from jax import lax
from jax.experimental import pallas as pl
from jax.experimental.pallas import tpu as pltpu

F32 = jnp.float32
BF16 = jnp.bfloat16

NORM_EPS = 1e-6
ROPE_THETA = 10000.0
GRID_W = 64
HEAD_DIM = 128
GQA_KV_HEADS = 4
GLA_HEADS = 4
GLA_RANK = 16
GLA_TAU = 16.0
GLA_CHUNK = 64
GLA_BLOCK = 4 * GLA_CHUNK
FNET_GROUPS = 4

LANES = 128
BF16_ROWS = 16
STRIP_UNROLL = 8
VMEM_LIMIT = 56 * 1024 * 1024

NT_DIMS = (((1,), (1,)), ((), ()))
TN_DIMS = (((0,), (0,)), ((), ()))


def _params(*sem):
    return pltpu.CompilerParams(dimension_semantics=sem, vmem_limit_bytes=VMEM_LIMIT)


def _rms(x, eps=NORM_EPS):
    return x * lax.rsqrt(jnp.mean(x * x, axis=-1, keepdims=True) + eps)


def _norm_mod_rows(x_ref, xn_ref, gain, shift, scale):
    mult = gain * (1.0 + scale)

    def strip(r, carry):
        rows = pl.ds(pl.multiple_of(r * BF16_ROWS, BF16_ROWS), BF16_ROWS)
        xn_ref[rows, :] = (_rms(x_ref[rows, :]) * mult + shift).astype(BF16)
        return carry

    lax.fori_loop(0, x_ref.shape[0] // BF16_ROWS, strip, 0, unroll=STRIP_UNROLL)


ROPE_STRIP = 256
MXU_WIDTH = 256


def _head_group_matrices():
    src = lax.broadcasted_iota(jnp.int32, (MXU_WIDTH, MXU_WIDTH), 0)
    dst = lax.broadcasted_iota(jnp.int32, (MXU_WIDTH, MXU_WIDTH), 1)
    quarter = HEAD_DIM // 4
    partner = jnp.where((dst // quarter) % 2 == 0, dst + quarter, dst - quarter)
    mean = jnp.where(src // HEAD_DIM == dst // HEAD_DIM, 1.0 / HEAD_DIM, 0.0).astype(BF16)
    return mean, (src == partner).astype(BF16)


def _mm_kernel(*refs, prologue, epilogue):
    it = iter(refs)
    x_ref = next(it)
    if prologue == "norm_mod":
        gain_ref, shift_ref, scale_ref = next(it), next(it), next(it)
    w_ref = next(it)
    if epilogue == "bias":
        b_ref = next(it)
    elif epilogue == "resid":
        res_ref, gate_ref = next(it), next(it)
    o_ref = next(it)
    if prologue != "cast":
        xn_ref = next(it)
    j = pl.program_id(1)

    if prologue == "cast":
        a = x_ref[...].astype(BF16)
    else:
        @pl.when(j == 0)
        def _():
            if prologue == "norm_mod":
                _norm_mod_rows(x_ref, xn_ref, gain_ref[...], shift_ref[0], scale_ref[0])
            else:
                xn_ref[...] = jax.nn.silu(x_ref[...]).astype(BF16)

        a = xn_ref[...]
    acc = jnp.dot(a, w_ref[...].astype(BF16), preferred_element_type=F32)
    if epilogue == "store":
        o_ref[...] = acc.astype(o_ref.dtype)
    elif epilogue == "bias":
        o_ref[...] = (acc + b_ref[...]).astype(o_ref.dtype)
    elif epilogue == "resid":
        o_ref[...] = (res_ref[...] + gate_ref[0] * acc).astype(o_ref.dtype)


def _mm(x, w, *, tm, tn, out_dtype, name, prologue="cast", epilogue="store", rows_per_batch=None,
        gain=None, shift=None, scale=None, bias=None, res=None, gate=None):
    t, k = x.shape
    n = w.shape[1]
    assert t % tm == 0 and n % tn == 0, (t, tm, n, tn)
    rows_per_batch = rows_per_batch or t
    assert rows_per_batch % tm == 0
    tiles_per_batch = rows_per_batch // tm
    once = dict(pipeline_mode=pl.Buffered(1))

    def batch_of(i):
        return i // tiles_per_batch

    args = [x]
    specs = [pl.BlockSpec((tm, k), lambda i, j: (i, 0))]
    if prologue == "norm_mod":
        nb = shift.shape[0]
        bsel = (lambda i: batch_of(i)) if nb > 1 else (lambda i: 0)
        args += [gain.reshape(1, k), shift, scale]
        specs += [pl.BlockSpec((1, k), lambda i, j: (0, 0)),
                  pl.BlockSpec((1, 1, k), lambda i, j: (bsel(i), 0, 0)),
                  pl.BlockSpec((1, 1, k), lambda i, j: (bsel(i), 0, 0))]
    args.append(w)
    specs.append(pl.BlockSpec((k, tn), lambda i, j: (0, j), **(once if tn == n else {})))
    if epilogue == "bias":
        args.append(bias.reshape(1, n))
        specs.append(pl.BlockSpec((1, tn), lambda i, j: (0, j)))
    elif epilogue == "resid":
        nb = gate.shape[0]
        gsel = (lambda i: batch_of(i)) if nb > 1 else (lambda i: 0)
        args += [res, gate]
        specs += [pl.BlockSpec((tm, tn), lambda i, j: (i, j)),
                  pl.BlockSpec((1, 1, tn), lambda i, j: (gsel(i), 0, j))]
    scratch = []
    if prologue != "cast":
        scratch.append(pltpu.VMEM((tm, k), BF16))
    kern = functools.partial(_mm_kernel, prologue=prologue, epilogue=epilogue)
    return pl.pallas_call(
        kern,
        grid=(t // tm, n // tn),
        in_specs=specs,
        out_specs=pl.BlockSpec((tm, tn), lambda i, j: (i, j)),
        out_shape=jax.ShapeDtypeStruct((t, n), out_dtype),
        scratch_shapes=scratch,
        compiler_params=_params("parallel", "arbitrary"),
        name=name,
    )(*args)


def _qkv_kernel(*refs, normed_cols, rope):
    it = iter(refs)
    x_ref, gain_ref, shift_ref, scale_ref, w_ref, hg_ref, ones_ref = [next(it) for _ in range(7)]
    if rope:
        perm_ref, cos_ref, sin_ref = next(it), next(it), next(it)
    o_ref, xn_ref = next(it), next(it)
    j = pl.program_id(1)
    tm, tn = o_ref.shape
    gw = ones_ref.shape[0]
    n_sub = tn // gw
    full_tiles, rem = divmod(normed_cols, tn)

    @pl.when(j == 0)
    def _():
        _norm_mod_rows(x_ref, xn_ref, gain_ref[...], shift_ref[0], scale_ref[0])

    def norm_rope_store(acc, cols):
        for r0 in range(0, tm, ROPE_STRIP):
            rows = slice(r0, r0 + ROPE_STRIP)
            blk = acc[rows, :]
            ms = jnp.dot((blk * blk).astype(BF16), ones_ref[...], preferred_element_type=F32)
            y = blk * lax.rsqrt(ms + NORM_EPS) * hg_ref[:, cols]
            if rope:
                partner = jnp.dot(y.astype(BF16), perm_ref[...], preferred_element_type=F32)
                reps = gw // HEAD_DIM
                y = (y * jnp.concatenate([cos_ref[rows, :]] * reps, axis=-1)
                     + partner * jnp.concatenate([sin_ref[rows, :]] * reps, axis=-1))
            o_ref[rows, cols] = y.astype(o_ref.dtype)

    def tile(n_normed_subs):
        a = xn_ref[...]
        if n_normed_subs == 0:
            o_ref[...] = jnp.dot(a, w_ref[...], preferred_element_type=F32).astype(o_ref.dtype)
            return
        sub_cols = [slice(s * gw, (s + 1) * gw) for s in range(n_sub)]
        acc = jnp.dot(a, w_ref[:, sub_cols[0]], preferred_element_type=F32)
        for s in range(n_sub):
            nxt = jnp.dot(a, w_ref[:, sub_cols[s + 1]], preferred_element_type=F32) if s + 1 < n_sub else None
            if s < n_normed_subs:
                norm_rope_store(acc, sub_cols[s])
            else:
                o_ref[:, sub_cols[s]] = acc.astype(o_ref.dtype)
            acc = nxt

    if full_tiles:
        pl.when(j < full_tiles)(functools.partial(tile, n_sub))
    if rem:
        pl.when(j == full_tiles)(functools.partial(tile, rem // gw))
    pl.when(j >= full_tiles + (1 if rem else 0))(functools.partial(tile, 0))


def _qkv_proj(x, w, gain, shift, scale, col_gain, *, normed_cols, tm, tn, name, rows_per_batch=None,
              cos=None, sin=None):
    t, k = x.shape
    n = w.shape[1]
    assert t % tm == 0 and n % tn == 0 and normed_cols % MXU_WIDTH == 0
    rows_per_batch = rows_per_batch or t
    assert rows_per_batch % tm == 0
    tiles_per_batch = rows_per_batch // tm
    rope = cos is not None
    nb = shift.shape[0]
    bsel = (lambda i: i // tiles_per_batch) if nb > 1 else (lambda i: 0)
    ones, perm = _head_group_matrices()
    const = pl.BlockSpec(ones.shape, lambda i, j: (0, 0))
    vec = pl.BlockSpec((1, 1, k), lambda i, j: (bsel(i), 0, 0))
    args = [x, gain.reshape(1, k), shift, scale, w, col_gain, ones]
    specs = [pl.BlockSpec((tm, k), lambda i, j: (i, 0)),
             pl.BlockSpec((1, k), lambda i, j: (0, 0)), vec, vec,
             pl.BlockSpec((k, tn), lambda i, j: (0, j)),
             pl.BlockSpec((1, tn), lambda i, j: (0, j)), const]
    if rope:
        table = pl.BlockSpec((tm, HEAD_DIM), lambda i, j: (i % tiles_per_batch, 0))
        args += [perm, cos, sin]
        specs += [const, table, table]
    return pl.pallas_call(
        functools.partial(_qkv_kernel, normed_cols=normed_cols, rope=rope),
        grid=(t // tm, n // tn),
        in_specs=specs,
        out_specs=pl.BlockSpec((tm, tn), lambda i, j: (i, j)),
        out_shape=jax.ShapeDtypeStruct((t, n), BF16),
        scratch_shapes=[pltpu.VMEM((tm, k), BF16)],
        compiler_params=_params("parallel", "arbitrary"),
        name=name,
    )(*args)


FFN_HID_SPLIT = 2


def _ffn_kernel(*refs, n_cast):
    x_ref, gain_ref, shift_ref, scale_ref, gate_ref, wg_ref, wu_ref, wo_ref = refs[:8]
    cast_src = refs[8:8 + n_cast]
    o_ref = refs[8 + n_cast]
    cast_dst = refs[9 + n_cast:9 + 2 * n_cast]
    xn_ref = refs[9 + 2 * n_cast]
    j = pl.program_id(1)
    hc = wg_ref.shape[1] // FFN_HID_SPLIT
    cols = [slice(c * hc, (c + 1) * hc) for c in range(FFN_HID_SPLIT)]
    for src, dst in zip(cast_src, cast_dst):
        dst[...] = src[...].astype(dst.dtype)

    def chunk(first, last):
        if first:
            _norm_mod_rows(x_ref, xn_ref, gain_ref[...], shift_ref[0], scale_ref[0])
        xn = xn_ref[...]
        gu = [(jnp.dot(xn, wg_ref[:, cs], preferred_element_type=F32),
               jnp.dot(xn, wu_ref[:, cs], preferred_element_type=F32)) for cs in cols]
        for c, (g, u) in enumerate(gu):
            a = (jax.nn.silu(g) * u).astype(BF16)
            part = jnp.dot(a, wo_ref[cols[c], :], preferred_element_type=F32)
            if first and c == 0:
                o_ref[...] = part
            elif last and c == len(gu) - 1:
                o_ref[...] = x_ref[...] + gate_ref[0] * (o_ref[...] + part)
            else:
                o_ref[...] += part

    n_chunks = pl.num_programs(1)
    pl.when(j == 0)(functools.partial(chunk, True, False))
    pl.when((j > 0) & (j < n_chunks - 1))(functools.partial(chunk, False, False))
    pl.when(j == n_chunks - 1)(functools.partial(chunk, False, True))


def _ffn(x, gain, shift, scale, gate, w_in, w_out, *, tm, th, name, rows_per_batch=None, cast=()):
    t, d = x.shape
    hidden = w_out.shape[0]
    assert t % tm == 0 and hidden % th == 0 and hidden // th >= 2
    nh = hidden // th
    n_steps = (t // tm) * nh
    rows_per_batch = rows_per_batch or t
    tiles_per_batch = rows_per_batch // tm
    nb = shift.shape[0]
    bsel = (lambda i: i // tiles_per_batch) if nb > 1 else (lambda i: 0)
    vec = pl.BlockSpec((1, 1, d), lambda i, j: (bsel(i), 0, 0))
    slabs = list(cast)
    slab_specs = []
    if slabs:
        nw_in, nw_out = slabs
        in_rows, in_cols = nw_in.shape[0] // (t // tm), nw_in.shape[1] // nh
        out_rows = nw_out.shape[0] // n_steps
        assert in_rows * (t // tm) == nw_in.shape[0] and in_cols * nh == nw_in.shape[1] and in_cols % LANES == 0
        assert out_rows * n_steps == nw_out.shape[0] and in_rows % BF16_ROWS == 0 and out_rows % BF16_ROWS == 0
        slab_specs = [pl.BlockSpec((in_rows, in_cols), lambda i, j: (i, j)),
                      pl.BlockSpec((out_rows, nw_out.shape[1]), lambda i, j: (i * nh + j, 0))]
    outs = pl.pallas_call(
        functools.partial(_ffn_kernel, n_cast=len(slabs)),
        grid=(t // tm, nh),
        in_specs=[pl.BlockSpec((tm, d), lambda i, j: (i, 0)),
                  pl.BlockSpec((1, d), lambda i, j: (0, 0)),
                  vec, vec, vec,
                  pl.BlockSpec((d, th), lambda i, j: (0, j)),
                  pl.BlockSpec((d, th), lambda i, j: (0, nh + j)),
                  pl.BlockSpec((th, d), lambda i, j: (j, 0))] + slab_specs,
        out_specs=[pl.BlockSpec((tm, d), lambda i, j: (i, 0))] + slab_specs,
        out_shape=[jax.ShapeDtypeStruct((t, d), F32)] + [jax.ShapeDtypeStruct(s.shape, BF16) for s in slabs],
        scratch_shapes=[pltpu.VMEM((tm, d), BF16)],
        compiler_params=_params("parallel", "arbitrary"),
        name=name,
    )(x, gain.reshape(1, d), shift, scale, gate, w_in, w_in, w_out, *slabs)
    return outs[0], tuple(outs[1:])


def _fill_kv(k_s, v_s, kv_refs):
    off = 0
    for k_ref, v_ref in kv_refs:
        n = k_ref.shape[0]
        k_s[off:off + n, :] = k_ref[...]
        v_s[off:off + n, :] = v_ref[...]
        off += n


ATTN_SUBTILE = 256


def _gqa_kernel(*refs, n_kv_src, n_group):
    q_ref = refs[0]
    kv_refs = [(refs[1 + 2 * s], refs[2 + 2 * s]) for s in range(n_kv_src)]
    o_ref, k_s, v_s = refs[1 + 2 * n_kv_src:]

    @pl.when(pl.program_id(2) == 0)
    def _():
        _fill_kv(k_s, v_s, kv_refs)

    k = k_s[...]
    v = v_s[...]
    sub = min(ATTN_SUBTILE, q_ref.shape[0])
    chains = [(slice(r0, r0 + sub), slice(g * HEAD_DIM, (g + 1) * HEAD_DIM))
              for r0 in range(0, q_ref.shape[0], sub) for g in range(n_group)]

    def scores(c):
        return lax.dot_general(q_ref[chains[c]], k, NT_DIMS, preferred_element_type=F32)

    s = scores(0)
    for c in range(len(chains)):
        s_next = scores(c + 1) if c + 1 < len(chains) else None
        p = jnp.exp2(s - jnp.max(s, axis=-1, keepdims=True))
        l = jnp.sum(p, axis=-1, keepdims=True)
        o = jnp.dot(p.astype(BF16), v, preferred_element_type=F32)
        o_ref[chains[c]] = (o / l).astype(o_ref.dtype)
        s = s_next


def _gqa_attention(q_src, kv_srcs, *, batch, n_q_heads, n_kv_heads, tq, q_rows, name):
    group = n_q_heads // n_kv_heads
    gw = group * HEAD_DIM
    nq = q_rows // tq
    k_blk0 = n_q_heads
    v_blk0 = n_q_heads + n_kv_heads
    specs = [pl.BlockSpec((tq, gw), lambda b, h, i: (b * nq + i, h))]
    args = [q_src]
    total = 0
    for src in kv_srcs:
        rows = src.shape[0] // batch
        total += rows
        specs += [pl.BlockSpec((rows, HEAD_DIM), lambda b, h, i: (b, k_blk0 + h)),
                  pl.BlockSpec((rows, HEAD_DIM), lambda b, h, i: (b, v_blk0 + h))]
        args += [src, src]
    kern = functools.partial(_gqa_kernel, n_kv_src=len(kv_srcs), n_group=group)
    return pl.pallas_call(
        kern,
        grid=(batch, n_kv_heads, nq),
        in_specs=specs,
        out_specs=pl.BlockSpec((tq, gw), lambda b, h, i: (b * nq + i, h)),
        out_shape=jax.ShapeDtypeStruct((batch * q_rows, n_q_heads * HEAD_DIM), BF16),
        scratch_shapes=[pltpu.VMEM((total, HEAD_DIM), BF16), pltpu.VMEM((total, HEAD_DIM), BF16)],
        compiler_params=_params("parallel", "parallel", "arbitrary"),
        name=name,
    )(*args)


def _diff_kernel(*refs, n_kv_src, lam_init):
    q_ref = refs[0]
    kv_refs = [(refs[1 + 2 * s], refs[2 + 2 * s]) for s in range(n_kv_src)]
    lq1, lk1, lq2, lk2, gain_ref, o_ref, k_s, v_s = refs[1 + 2 * n_kv_src:]

    @pl.when(pl.program_id(2) == 0)
    def _():
        _fill_kv(k_s, v_s, kv_refs)

    lam = (jnp.exp(jnp.sum(lq1[...] * lk1[...], axis=-1, keepdims=True))
           - jnp.exp(jnp.sum(lq2[...] * lk2[...], axis=-1, keepdims=True)) + lam_init)
    dh = HEAD_DIM
    sub = min(ATTN_SUBTILE, q_ref.shape[0])
    chains = [slice(r0, r0 + sub) for r0 in range(0, q_ref.shape[0], sub)]

    def scores(c):
        return [lax.dot_general(q_ref[chains[c], m * dh:(m + 1) * dh], k_s[:, m * dh:(m + 1) * dh], NT_DIMS,
                                preferred_element_type=F32) for m in range(2)]

    s = scores(0)
    for c in range(len(chains)):
        s_next = scores(c + 1) if c + 1 < len(chains) else None
        p = [jnp.exp2(sm - jnp.max(sm, axis=-1, keepdims=True)) for sm in s]
        l0, l1 = [jnp.sum(pm, axis=-1, keepdims=True) for pm in p]
        w = p[0] - p[1] * (lam * l0 / l1)
        o = jnp.dot(w.astype(BF16), v_s[...], preferred_element_type=F32) * (1.0 / l0)
        o_ref[chains[c], :] = ((_rms(o) * gain_ref[...]) * (1.0 - lam_init)).astype(o_ref.dtype)
        s = s_next


def _diff_attention(q_src, kv_srcs, lams, out_gain, *, batch, n_heads, tq, q_rows, lam_init, name):
    hw = 2 * HEAD_DIM
    nq = q_rows // tq
    specs = [pl.BlockSpec((tq, hw), lambda b, h, i: (b * nq + i, h))]
    args = [q_src]
    total = 0
    for src in kv_srcs:
        rows = src.shape[0] // batch
        total += rows
        specs += [pl.BlockSpec((rows, hw), lambda b, h, i: (b, n_heads + h)),
                  pl.BlockSpec((rows, hw), lambda b, h, i: (b, 2 * n_heads + h))]
        args += [src, src]
    small = pl.BlockSpec((1, HEAD_DIM), lambda b, h, i: (0, 0))
    specs += [small] * 4 + [pl.BlockSpec((1, hw), lambda b, h, i: (0, 0))]
    args += [v.reshape(1, HEAD_DIM) for v in lams] + [out_gain.reshape(1, hw)]
    kern = functools.partial(_diff_kernel, n_kv_src=len(kv_srcs), lam_init=lam_init)
    return pl.pallas_call(
        kern,
        grid=(batch, n_heads, nq),
        in_specs=specs,
        out_specs=pl.BlockSpec((tq, hw), lambda b, h, i: (b * nq + i, h)),
        out_shape=jax.ShapeDtypeStruct((batch * q_rows, n_heads * hw), BF16),
        scratch_shapes=[pltpu.VMEM((total, hw), BF16), pltpu.VMEM((total, hw), BF16)],
        compiler_params=_params("parallel", "parallel", "arbitrary"),
        name=name,
    )(*args)


def _gla_blocks(blocks):
    r, dk = blocks[0][0].shape
    c = GLA_CHUNK
    nc = r // c
    n = range(len(blocks))
    qs_, ks_, vs_, zs_, wgs, bgs, st_refs, revs = zip(*blocks)
    row = lax.broadcasted_iota(jnp.int32, (r, r), 0)
    col = lax.broadcasted_iota(jnp.int32, (r, r), 1)
    same = row // c == col // c
    tri_f = [jnp.where(same, jnp.where((col >= row) if revs[i] else (col <= row), 1.0, 0.0), 0.0) for i in n]
    tri = [t.astype(BF16) for t in tri_f]
    mid = [c // 2 if revs[i] else c // 2 - 1 for i in n]
    last = [0 if revs[i] else c - 1 for i in n]

    g = [jax.nn.log_sigmoid(jnp.dot(zs_[i], wgs[i], preferred_element_type=F32) + bgs[i]) / GLA_TAU for i in n]
    g_hi = [g[i].astype(BF16) for i in n]
    g_lo = [(g[i] - g_hi[i].astype(F32)).astype(BF16) for i in n]
    cum = [jnp.dot(tri[i], g_hi[i], preferred_element_type=F32) + jnp.dot(tri[i], g_lo[i], preferred_element_type=F32)
           for i in n]

    def chunk_row(x, idx):
        return jnp.concatenate(
            [jnp.broadcast_to(x[ci * c + idx:ci * c + idx + 1, :], (c, dk)) for ci in range(nc)], axis=0)

    cum_mid = [chunk_row(cum[i], mid[i]) for i in n]
    cum_last = [chunk_row(cum[i], last[i]) for i in n]
    qs = [(qs_[i] * jnp.exp(cum[i] - cum_mid[i])).astype(BF16) for i in n]
    ks = [(ks_[i] * jnp.exp(cum_mid[i] - cum[i])).astype(BF16) for i in n]
    a = [lax.dot_general(qs[i], ks[i], NT_DIMS, preferred_element_type=F32) for i in n]
    q_inter = [qs_[i] * jnp.exp(cum[i]) for i in n]
    k_carry = [ks_[i] * jnp.exp(cum_last[i] - cum[i]) for i in n]

    assert nc % 2 == 0
    same_pair = row // (2 * c) == col // (2 * c)
    pair_f = [jnp.where(same_pair, jnp.where((col // c > row // c) if revs[i] else (col // c < row // c), 1.0, 0.0),
                        0.0) for i in n]
    a_pair = [lax.dot_general(q_inter[i].astype(BF16), k_carry[i].astype(BF16), NT_DIMS,
                              preferred_element_type=F32) for i in n]
    a = [(jnp.where(tri_f[i] > 0.5, a[i], 0.0) + jnp.where(pair_f[i] > 0.5, a_pair[i], 0.0)).astype(BF16) for i in n]
    o_intra = [jnp.dot(a[i], vs_[i], preferred_element_type=F32) for i in n]

    def tot(i, ci):
        return cum[i][ci * c + last[i]:ci * c + last[i] + 1, :]

    def is_first(i, ci):
        return (ci % 2 == 1) if revs[i] else (ci % 2 == 0)

    def pair_scaled(i, x, scale_first):
        parts = []
        for ci in range(nc):
            rows = slice(ci * c, (ci + 1) * c)
            if is_first(i, ci) == scale_first:
                parts.append(x[rows] * jnp.exp(tot(i, ci ^ 1)))
            else:
                parts.append(x[rows])
        return jnp.concatenate(parts, axis=0).astype(BF16)

    q_sup = [pair_scaled(i, q_inter[i], False) for i in n]
    k_sup = [pair_scaled(i, k_carry[i], True) for i in n]
    st = [st_refs[i][...] for i in n]
    n_pairs = nc // 2
    o_inter = [[None] * n_pairs for _ in n]
    for step in range(n_pairs):
        for i in n:
            pg = n_pairs - 1 - step if revs[i] else step
            rows = slice(pg * 2 * c, (pg + 1) * 2 * c)
            o_inter[i][pg] = lax.dot_general(q_sup[i][rows], st[i].astype(BF16), NT_DIMS,
                                             preferred_element_type=F32)
            decay = jnp.exp(tot(i, 2 * pg) + tot(i, 2 * pg + 1))
            st[i] = st[i] * decay + lax.dot_general(vs_[i][rows], k_sup[i][rows], TN_DIMS,
                                                    preferred_element_type=F32)
    for i in n:
        st_refs[i][...] = st[i]
    return [o_intra[i] + jnp.concatenate(o_inter[i], axis=0) for i in n]


def _gla_kernel(qc_ref, kc_ref, vc_ref, rc_ref, zc_ref, ql_ref, kl_ref, vl_ref, rl_ref, zl_ref,
                wgf_ref, bgf_ref, wgb_ref, bgb_ref, gain_ref, oc_ref, ol_ref, sf_ref, sb_ref, of_ref, ob_ref):
    blk = GLA_BLOCK
    n_ctx = qc_ref.shape[0] // blk
    n_lat = ql_ref.shape[0] // blk
    q_scale = qc_ref.shape[1] ** -0.5
    ctx_refs = (qc_ref, kc_ref, vc_ref, zc_ref)
    lat_refs = (ql_ref, kl_ref, vl_ref, zl_ref)

    def rows_of(bi):
        return pl.ds(pl.multiple_of(bi * blk, blk), blk)

    def block(refs, bi, reverse):
        q_ref, k_ref, v_ref, z_ref = refs
        rows = rows_of(bi)
        wg_ref, bg_ref, st_ref = (wgb_ref, bgb_ref, sb_ref) if reverse else (wgf_ref, bgf_ref, sf_ref)
        return (q_ref[rows, :].astype(F32) * q_scale, k_ref[rows, :].astype(F32), v_ref[rows, :], z_ref[rows, :],
                wg_ref[0], bg_ref[0], st_ref, reverse)

    def both(refs, base, n):
        def body(t, carry):
            o_f, o_b = _gla_blocks([block(refs, t, False), block(refs, n - 1 - t, True)])
            of_ref[rows_of(base + t), :] = o_f
            ob_ref[rows_of(base + n - 1 - t), :] = o_b
            return carry
        return body

    def finish(r_ref, out_ref, base):
        def body(bi, carry):
            rows = rows_of(bi)
            o = of_ref[rows_of(base + bi), :] + ob_ref[rows_of(base + bi), :]
            y = (_rms(o) * gain_ref[...]) * jax.nn.silu(r_ref[rows, :].astype(F32))
            out_ref[rows, :] = y.astype(out_ref.dtype)
            return carry
        return body

    sf_ref[...] = jnp.zeros_like(sf_ref)
    sb_ref[...] = jnp.zeros_like(sb_ref)
    lax.fori_loop(0, n_ctx, both(ctx_refs, 0, n_ctx), 0)
    lax.fori_loop(0, n_lat, both(lat_refs, n_ctx, n_lat), 0)
    lax.fori_loop(0, n_ctx, finish(rc_ref, oc_ref, 0), 0)
    lax.fori_loop(0, n_lat, finish(rl_ref, ol_ref, n_ctx), 0)


def _gla(p_ctx, p_lat, wgf, bgf, wgb, bgb, out_gain, *, batch, dk, dv):
    h = GLA_HEADS
    rows_c = p_ctx.shape[0] // batch
    rows_l = p_lat.shape[0] // batch
    assert rows_c % GLA_BLOCK == 0 and rows_l % GLA_BLOCK == 0
    zblk = (2 * h * dk + 2 * h * dv) // LANES
    k0 = h
    v0 = (2 * h * dk) // dv
    r0 = v0 + h

    def stream(rows):
        return [pl.BlockSpec((rows, dk), lambda b, hh: (b, hh)),
                pl.BlockSpec((rows, dk), lambda b, hh: (b, k0 + hh)),
                pl.BlockSpec((rows, dv), lambda b, hh: (b, v0 + hh)),
                pl.BlockSpec((rows, dv), lambda b, hh: (b, r0 + hh)),
                pl.BlockSpec((rows, LANES), lambda b, hh: (b, zblk))]

    wspec = pl.BlockSpec((1, LANES, dk), lambda b, hh: (hh, 0, 0))
    bspec = pl.BlockSpec((1, 1, dk), lambda b, hh: (hh, 0, 0))
    return pl.pallas_call(
        _gla_kernel,
        grid=(batch, h),
        in_specs=stream(rows_c) + stream(rows_l) + [wspec, bspec, wspec, bspec,
                                                    pl.BlockSpec((1, dv), lambda b, hh: (0, 0))],
        out_specs=[pl.BlockSpec((rows_c, dv), lambda b, hh: (b, hh)),
                   pl.BlockSpec((rows_l, dv), lambda b, hh: (b, hh))],
        out_shape=[jax.ShapeDtypeStruct((batch * rows_c, h * dv), BF16),
                   jax.ShapeDtypeStruct((batch * rows_l, h * dv), BF16)],
        scratch_shapes=[pltpu.VMEM((dv, dk), F32), pltpu.VMEM((dv, dk), F32),
                        pltpu.VMEM((rows_c + rows_l, dv), F32), pltpu.VMEM((rows_c + rows_l, dv), F32)],
        compiler_params=_params("parallel", "parallel"),
        name="gla",
    )(*([p_ctx] * 5 + [p_lat] * 5 + [wgf, bgf, wgb, bgb, out_gain.reshape(1, dv)]))


def _dft_tables(n):
    idx = jnp.arange(n, dtype=jnp.int32)
    ang = ((idx[:, None] * idx[None, :]) % n).astype(F32) * (2.0 * math.pi / n)
    return jnp.cos(ang).astype(BF16), jnp.sin(ang).astype(BF16)


def _fnet_chan_kernel(x_ref, gain_ref, shift_ref, scale_ref, csc_ref, p_ref, q_ref, xn_ref):
    gd = csc_ref.shape[0]
    _norm_mod_rows(x_ref, xn_ref, gain_ref[...], shift_ref[0], scale_ref[0])
    for g in range(x_ref.shape[1] // gd):
        cols = slice(g * gd, (g + 1) * gd)
        pq = jnp.dot(xn_ref[:, cols], csc_ref[...], preferred_element_type=F32)
        p_ref[:, cols] = pq[:, :gd].astype(p_ref.dtype)
        q_ref[:, cols] = pq[:, gd:].astype(q_ref.dtype)


def _fnet_seq_kernel(cs_ref, ss_ref, p_ref, q_ref, o_ref, *, inv_norm):
    acc = (jnp.dot(cs_ref[...], p_ref[...], preferred_element_type=F32)
           - jnp.dot(ss_ref[...], q_ref[...], preferred_element_type=F32))
    o_ref[...] = (acc * inv_norm).astype(o_ref.dtype)


def _fnet(x, gain, shift, scale, *, batch, tm):
    t, d = x.shape
    s = t // batch
    gd = d // FNET_GROUPS
    tiles_per_batch = s // tm
    cc, sc = _dft_tables(gd)
    cs, ss = _dft_tables(s)
    vec = pl.BlockSpec((1, 1, d), lambda i: (i // tiles_per_batch, 0, 0))
    blk = pl.BlockSpec((tm, d), lambda i: (i, 0))
    p, q = pl.pallas_call(
        _fnet_chan_kernel,
        grid=(t // tm,),
        in_specs=[blk, pl.BlockSpec((1, d), lambda i: (0, 0)), vec, vec,
                  pl.BlockSpec((gd, 2 * gd), lambda i: (0, 0))],
        out_specs=[blk, blk],
        out_shape=[jax.ShapeDtypeStruct((t, d), BF16)] * 2,
        scratch_shapes=[pltpu.VMEM((tm, d), BF16)],
        compiler_params=_params("parallel"),
        name="fnet_chan",
    )(x, gain.reshape(1, d), shift, scale, jnp.concatenate([cc, sc], axis=1))
    rows = pl.BlockSpec((tm, s), lambda b, j, i: (i, 0))
    cols = pl.BlockSpec((s, gd), lambda b, j, i: (b, j))
    return pl.pallas_call(
        functools.partial(_fnet_seq_kernel, inv_norm=float((s * gd) ** -0.5)),
        grid=(batch, d // gd, tiles_per_batch),
        in_specs=[rows, rows, cols, cols],
        out_specs=pl.BlockSpec((tm, gd), lambda b, j, i: (b * tiles_per_batch + i, j)),
        out_shape=jax.ShapeDtypeStruct((t, d), BF16),
        compiler_params=_params("parallel", "parallel", "arbitrary"),
        name="fnet_seq",
    )(cs, ss, p, q)


def _rope_tables(n_tokens):
    t = jnp.arange(n_tokens)
    row = (t // GRID_W).astype(F32)
    col = (t % GRID_W).astype(F32)
    half = HEAD_DIM // 2
    inv_freq = ROPE_THETA ** (-jnp.arange(0, half, 2, dtype=F32) / half)
    ang_r = row[:, None] * inv_freq[None, :]
    ang_c = col[:, None] * inv_freq[None, :]
    ang = jnp.concatenate([ang_r, ang_r, ang_c, ang_c], axis=-1)
    sign = jnp.concatenate([-jnp.ones((half // 2,), F32), jnp.ones((half // 2,), F32)] * 2)
    return jnp.cos(ang), jnp.sin(ang) * sign


def _column_gains(q_gain, k_gain, q_cols, k_cols, v_cols):
    qg = jnp.tile(q_gain.astype(F32) * (HEAD_DIM ** -0.5 * math.log2(math.e)), q_cols // HEAD_DIM)
    kg = jnp.tile(k_gain.astype(F32), k_cols // HEAD_DIM)
    return jnp.concatenate([qg, kg, jnp.ones((v_cols,), F32)])[None, :]


def _gate_weights(wg, bg, lane0, dk):
    r = wg.shape[0]
    w = wg.reshape(r, GLA_HEADS, dk).transpose(1, 0, 2)
    w = jnp.pad(w, ((0, 0), (lane0, LANES - lane0 - r), (0, 0))).astype(BF16)
    return w, bg.reshape(GLA_HEADS, 1, dk).astype(F32)


def kernel(x, c, ctx, c_ctx, l0_mod_w, l0_mod_b, l0_norm1, l0_gla_w_in, l0_gla_wg_f, l0_gla_bg_f, l0_gla_wg_b, l0_gla_bg_b, l0_gla_out_norm, l0_gla_w_out, l0_norm2, l0_ffn_w_in, l0_ffn_w_out, l1_mod_w, l1_mod_b, l1_norm1, l1_gqa_w_in, l1_gqa_q_norm, l1_gqa_k_norm, l1_gqa_w_out, l1_norm2, l1_ffn_w_in, l1_ffn_w_out, l2_mod_w, l2_mod_b, l2_norm1, l2_diff_w_in, l2_diff_q_norm, l2_diff_k_norm, l2_diff_lq1, l2_diff_lk1, l2_diff_lq2, l2_diff_lk2, l2_diff_out_norm, l2_diff_w_out, l2_norm2, l2_ffn_w_in, l2_ffn_w_out, l3_mod_w, l3_mod_b, l3_norm1, l3_fnet_w_out, l3_norm2, l3_ffn_w_in, l3_ffn_w_out):
    b, s, d = x.shape
    n_ctx = ctx.shape[1]
    xl = x.reshape(b * s, d)
    xc = ctx.reshape(b * n_ctx, d)
    tm = math.gcd(s, 1024)
    tm_c = math.gcd(b * n_ctx, 1024)
    tm_r = math.gcd(s, 512)
    tm_rc = math.gcd(b * n_ctx, 512)
    th = math.gcd(l0_ffn_w_out.shape[0], 512)
    tq_gqa = math.gcd(s, 4 * ATTN_SUBTILE)
    tq_diff = math.gcd(s, 4 * ATTN_SUBTILE)

    n_cond = -(-(b + 1) // BF16_ROWS) * BF16_ROWS
    cond = jnp.concatenate([c, c_ctx[None, :], jnp.zeros((n_cond - b - 1, d), F32)], axis=0)

    def modulation(mod_w, mod_b):
        m = _mm(cond, mod_w, tm=n_cond, tn=math.gcd(mod_w.shape[1], 1024), out_dtype=F32, prologue="silu", epilogue="bias", bias=mod_b,
                name="modulation")
        lat = [m[:b, k * d:(k + 1) * d].reshape(b, 1, d) for k in range(6)]
        cx = [m[b:b + 1, k * d:(k + 1) * d].reshape(1, 1, d) for k in range(6)]
        return lat, cx

    def tiling(rows, resident=False):
        if rows == s:
            return dict(tm=tm_r if resident else tm, rows_per_batch=s)
        return dict(tm=tm_rc if resident else tm_c, rows_per_batch=None)

    def proj(xs, w, n1, sh, sc, rows, tn, name, **kw):
        return _mm(xs, w, tn=tn, out_dtype=BF16, prologue="norm_mod", gain=n1, shift=sh, scale=sc, name=name,
                   **tiling(rows), **kw)

    def qkv_proj(xs, w, n1, sh, sc, col_gain, rows, name, **kw):
        return _qkv_proj(xs, w, n1, sh, sc, col_gain, tn=math.gcd(w.shape[1], 1024), name=name, **tiling(rows), **kw)

    def out_resid(y, w, xs, gate, rows):
        return _mm(y, w.astype(BF16), tn=w.shape[1], out_dtype=F32, epilogue="resid", res=xs, gate=gate,
                   name="out_resid", **tiling(rows, resident=True))

    def ffn(xs, n2, sh, sc, gate, w_bf16, rows, next_w=()):
        return _ffn(xs, n2, sh, sc, gate, *w_bf16, th=th, name="ffn", cast=next_w, **tiling(rows))

    ffn_w = (l0_ffn_w_in.astype(BF16), l0_ffn_w_out.astype(BF16))

    (sh1, sc1, g1, sh2, sc2, g2), (csh1, csc1, cg1, csh2, csc2, cg2) = modulation(l0_mod_w, l0_mod_b)
    dk = l0_gla_wg_f.shape[1] // GLA_HEADS
    dv = d // GLA_HEADS
    n_in = l0_gla_w_in.shape[1]
    tn0 = 1280
    n_pad = -(-(n_in - 2 * GLA_RANK + LANES) // tn0) * tn0
    w0 = jnp.pad(l0_gla_w_in, ((0, 0), (0, n_pad - n_in))).astype(BF16)
    pl0 = proj(xl, w0, l0_norm1, sh1, sc1, s, tn0, "gla_proj")
    pc0 = proj(xc, w0, l0_norm1, csh1, csc1, n_ctx, tn0, "gla_proj_ctx")
    wgf, bgf = _gate_weights(l0_gla_wg_f, l0_gla_bg_f, 0, dk)
    wgb, bgb = _gate_weights(l0_gla_wg_b, l0_gla_bg_b, GLA_RANK, dk)
    yc, yl = _gla(pc0, pl0, wgf, bgf, wgb, bgb, l0_gla_out_norm, batch=b, dk=dk, dv=dv)
    xl = out_resid(yl, l0_gla_w_out, xl, g1, s)
    xc = out_resid(yc, l0_gla_w_out, xc, cg1, n_ctx)
    xl, next_ffn_w = ffn(xl, l0_norm2, sh2, sc2, g2, ffn_w, s, next_w=(l1_ffn_w_in, l1_ffn_w_out))
    xc, _ = ffn(xc, l0_norm2, csh2, csc2, cg2, ffn_w, n_ctx)
    ffn_w = next_ffn_w

    (sh1, sc1, g1, sh2, sc2, g2), (csh1, csc1, cg1, csh2, csc2, cg2) = modulation(l1_mod_w, l1_mod_b)
    n_heads = d // HEAD_DIM
    cos, sin = _rope_tables(s)
    kv_cols = GQA_KV_HEADS * HEAD_DIM
    cg_gqa = _column_gains(l1_gqa_q_norm, l1_gqa_k_norm, d, kv_cols, kv_cols)
    w1 = l1_gqa_w_in.astype(BF16)
    pl1 = qkv_proj(xl, w1, l1_norm1, sh1, sc1, cg_gqa, s, "gqa_proj", normed_cols=d + kv_cols, cos=cos, sin=sin)
    pc1 = qkv_proj(xc, w1, l1_norm1, csh1, csc1, cg_gqa, n_ctx, "gqa_proj_ctx", normed_cols=d + kv_cols)
    yl = _gqa_attention(pl1, [pl1, pc1], batch=b, n_q_heads=n_heads, n_kv_heads=GQA_KV_HEADS, tq=tq_gqa, q_rows=s,
                        name="gqa_attn")
    yc = _gqa_attention(pc1, [pc1], batch=b, n_q_heads=n_heads, n_kv_heads=GQA_KV_HEADS, tq=n_ctx, q_rows=n_ctx,
                        name="gqa_attn_ctx")
    xl = out_resid(yl, l1_gqa_w_out, xl, g1, s)
    xc = out_resid(yc, l1_gqa_w_out, xc, cg1, n_ctx)
    xl, next_ffn_w = ffn(xl, l1_norm2, sh2, sc2, g2, ffn_w, s, next_w=(l2_ffn_w_in, l2_ffn_w_out))
    xc, _ = ffn(xc, l1_norm2, csh2, csc2, cg2, ffn_w, n_ctx)
    ffn_w = next_ffn_w

    (sh1, sc1, g1, sh2, sc2, g2), (csh1, csc1, _, _, _, _) = modulation(l2_mod_w, l2_mod_b)
    lam_init = 0.8 - 0.6 * math.exp(-0.3 * 2)
    cg_diff = _column_gains(l2_diff_q_norm, l2_diff_k_norm, d, d, d)
    w2 = l2_diff_w_in.astype(BF16)
    pl2 = qkv_proj(xl, w2, l2_norm1, sh1, sc1, cg_diff, s, "diff_proj", normed_cols=2 * d, cos=cos, sin=sin)
    pc2 = qkv_proj(xc, w2, l2_norm1, csh1, csc1, cg_diff, n_ctx, "diff_proj_ctx", normed_cols=2 * d)
    yl = _diff_attention(pl2, [pl2, pc2], (l2_diff_lq1, l2_diff_lk1, l2_diff_lq2, l2_diff_lk2),
                         l2_diff_out_norm, batch=b, n_heads=n_heads // 2, tq=tq_diff, q_rows=s, lam_init=lam_init,
                         name="diff_attn")
    xl = out_resid(yl, l2_diff_w_out, xl, g1, s)
    xl, ffn_w = ffn(xl, l2_norm2, sh2, sc2, g2, ffn_w, s, next_w=(l3_ffn_w_in, l3_ffn_w_out))

    (sh1, sc1, g1, sh2, sc2, g2), _ = modulation(l3_mod_w, l3_mod_b)
    yl = _fnet(xl, l3_norm1, sh1, sc1, batch=b, tm=tm)
    xl = out_resid(yl, l3_fnet_w_out, xl, g1, s)
    xl, _ = ffn(xl, l3_norm2, sh2, sc2, g2, ffn_w, s)
    return xl.reshape(b, s, d)
```

```python
import functools
import math

import jax
import jax.numpy as jnp
from jax import lax
from jax.experimental import pallas as pl
from jax.experimental.pallas import tpu as pltpu

F32 = jnp.float32
BF16 = jnp.bfloat16

NORM_EPS = 1e-6
ROPE_THETA = 10000.0
GRID_W = 64
HEAD_DIM = 128
GQA_KV_HEADS = 4
GLA_HEADS = 4
GLA_RANK = 16
GLA_TAU = 16.0
GLA_CHUNK = 64
GLA_BLOCK = 4 * GLA_CHUNK
FNET_GROUPS = 4

LANES = 128
BF16_ROWS = 16
STRIP_UNROLL = 8
VMEM_LIMIT = 56 * 1024 * 1024

NT_DIMS = (((1,), (1,)), ((), ()))
TN_DIMS = (((0,), (0,)), ((), ()))


def _params(*sem):
    return pltpu.CompilerParams(dimension_semantics=sem, vmem_limit_bytes=VMEM_LIMIT)


def _rms(x, eps=NORM_EPS):
    return x * lax.rsqrt(jnp.mean(x * x, axis=-1, keepdims=True) + eps)


def _norm_mod_rows(x_ref, xn_ref, gain, shift, scale):
    mult = gain * (1.0 + scale)

    def strip(r, carry):
        rows = pl.ds(pl.multiple_of(r * BF16_ROWS, BF16_ROWS), BF16_ROWS)
        xn_ref[rows, :] = (_rms(x_ref[rows, :]) * mult + shift).astype(BF16)
        return carry

    lax.fori_loop(0, x_ref.shape[0] // BF16_ROWS, strip, 0, unroll=STRIP_UNROLL)


ROPE_STRIP = 256
MXU_WIDTH = 256


def _head_group_matrices():
    src = lax.broadcasted_iota(jnp.int32, (MXU_WIDTH, MXU_WIDTH), 0)
    dst = lax.broadcasted_iota(jnp.int32, (MXU_WIDTH, MXU_WIDTH), 1)
    quarter = HEAD_DIM // 4
    partner = jnp.where((dst // quarter) % 2 == 0, dst + quarter, dst - quarter)
    mean = jnp.where(src // HEAD_DIM == dst // HEAD_DIM, 1.0 / HEAD_DIM, 0.0).astype(BF16)
    return mean, (src == partner).astype(BF16)


def _mm_kernel(*refs, prologue, epilogue):
    it = iter(refs)
    x_ref = next(it)
    if prologue == "norm_mod":
        gain_ref, shift_ref, scale_ref = next(it), next(it), next(it)
    w_ref = next(it)
    if epilogue == "bias":
        b_ref = next(it)
    elif epilogue == "resid":
        res_ref, gate_ref = next(it), next(it)
    o_ref = next(it)
    if prologue != "cast":
        xn_ref = next(it)
    j = pl.program_id(1)

    if prologue == "cast":
        a = x_ref[...].astype(BF16)
    else:
        @pl.when(j == 0)
        def _():
            if prologue == "norm_mod":
                _norm_mod_rows(x_ref, xn_ref, gain_ref[...], shift_ref[0], scale_ref[0])
            else:
                xn_ref[...] = jax.nn.silu(x_ref[...]).astype(BF16)

        a = xn_ref[...]
    acc = jnp.dot(a, w_ref[...].astype(BF16), preferred_element_type=F32)
    if epilogue == "store":
        o_ref[...] = acc.astype(o_ref.dtype)
    elif epilogue == "bias":
        o_ref[...] = (acc + b_ref[...]).astype(o_ref.dtype)
    elif epilogue == "resid":
        o_ref[...] = (res_ref[...] + gate_ref[0] * acc).astype(o_ref.dtype)


def _mm(x, w, *, tm, tn, out_dtype, name, prologue="cast", epilogue="store", rows_per_batch=None,
        gain=None, shift=None, scale=None, bias=None, res=None, gate=None):
    t, k = x.shape
    n = w.shape[1]
    assert t % tm == 0 and n % tn == 0, (t, tm, n, tn)
    rows_per_batch = rows_per_batch or t
    assert rows_per_batch % tm == 0
    tiles_per_batch = rows_per_batch // tm
    once = dict(pipeline_mode=pl.Buffered(1))

    def batch_of(i):
        return i // tiles_per_batch

    args = [x]
    specs = [pl.BlockSpec((tm, k), lambda i, j: (i, 0))]
    if prologue == "norm_mod":
        nb = shift.shape[0]
        bsel = (lambda i: batch_of(i)) if nb > 1 else (lambda i: 0)
        args += [gain.reshape(1, k), shift, scale]
        specs += [pl.BlockSpec((1, k), lambda i, j: (0, 0)),
                  pl.BlockSpec((1, 1, k), lambda i, j: (bsel(i), 0, 0)),
                  pl.BlockSpec((1, 1, k), lambda i, j: (bsel(i), 0, 0))]
    args.append(w)
    specs.append(pl.BlockSpec((k, tn), lambda i, j: (0, j), **(once if tn == n else {})))
    if epilogue == "bias":
        args.append(bias.reshape(1, n))
        specs.append(pl.BlockSpec((1, tn), lambda i, j: (0, j)))
    elif epilogue == "resid":
        nb = gate.shape[0]
        gsel = (lambda i: batch_of(i)) if nb > 1 else (lambda i: 0)
        args += [res, gate]
        specs += [pl.BlockSpec((tm, tn), lambda i, j: (i, j)),
                  pl.BlockSpec((1, 1, tn), lambda i, j: (gsel(i), 0, j))]
    scratch = []
    if prologue != "cast":
        scratch.append(pltpu.VMEM((tm, k), BF16))
    kern = functools.partial(_mm_kernel, prologue=prologue, epilogue=epilogue)
    return pl.pallas_call(
        kern,
        grid=(t // tm, n // tn),
        in_specs=specs,
        out_specs=pl.BlockSpec((tm, tn), lambda i, j: (i, j)),
        out_shape=jax.ShapeDtypeStruct((t, n), out_dtype),
        scratch_shapes=scratch,
        compiler_params=_params("parallel", "arbitrary"),
        name=name,
    )(*args)


def _qkv_kernel(*refs, normed_cols, rope):
    it = iter(refs)
    x_ref, gain_ref, shift_ref, scale_ref, w_ref, hg_ref, ones_ref = [next(it) for _ in range(7)]
    if rope:
        perm_ref, cos_ref, sin_ref = next(it), next(it), next(it)
    o_ref, xn_ref = next(it), next(it)
    j = pl.program_id(1)
    tm, tn = o_ref.shape
    gw = ones_ref.shape[0]
    n_sub = tn // gw
    full_tiles, rem = divmod(normed_cols, tn)

    @pl.when(j == 0)
    def _():
        _norm_mod_rows(x_ref, xn_ref, gain_ref[...], shift_ref[0], scale_ref[0])

    def norm_rope_store(acc, cols):
        for r0 in range(0, tm, ROPE_STRIP):
            rows = slice(r0, r0 + ROPE_STRIP)
            blk = acc[rows, :]
            ms = jnp.dot((blk * blk).astype(BF16), ones_ref[...], preferred_element_type=F32)
            y = blk * lax.rsqrt(ms + NORM_EPS) * hg_ref[:, cols]
            if rope:
                partner = jnp.dot(y.astype(BF16), perm_ref[...], preferred_element_type=F32)
                reps = gw // HEAD_DIM
                y = (y * jnp.concatenate([cos_ref[rows, :]] * reps, axis=-1)
                     + partner * jnp.concatenate([sin_ref[rows, :]] * reps, axis=-1))
            o_ref[rows, cols] = y.astype(o_ref.dtype)

    def tile(n_normed_subs):
        a = xn_ref[...]
        if n_normed_subs == 0:
            o_ref[...] = jnp.dot(a, w_ref[...], preferred_element_type=F32).astype(o_ref.dtype)
            return
        sub_cols = [slice(s * gw, (s + 1) * gw) for s in range(n_sub)]
        acc = jnp.dot(a, w_ref[:, sub_cols[0]], preferred_element_type=F32)
        for s in range(n_sub):
            nxt = jnp.dot(a, w_ref[:, sub_cols[s + 1]], preferred_element_type=F32) if s + 1 < n_sub else None
            if s < n_normed_subs:
                norm_rope_store(acc, sub_cols[s])
            else:
                o_ref[:, sub_cols[s]] = acc.astype(o_ref.dtype)
            acc = nxt

    if full_tiles:
        pl.when(j < full_tiles)(functools.partial(tile, n_sub))
    if rem:
        pl.when(j == full_tiles)(functools.partial(tile, rem // gw))
    pl.when(j >= full_tiles + (1 if rem else 0))(functools.partial(tile, 0))


def _qkv_proj(x, w, gain, shift, scale, col_gain, *, normed_cols, tm, tn, name, rows_per_batch=None,
              cos=None, sin=None):
    t, k = x.shape
    n = w.shape[1]
    assert t % tm == 0 and n % tn == 0 and normed_cols % MXU_WIDTH == 0
    rows_per_batch = rows_per_batch or t
    assert rows_per_batch % tm == 0
    tiles_per_batch = rows_per_batch // tm
    rope = cos is not None
    nb = shift.shape[0]
    bsel = (lambda i: i // tiles_per_batch) if nb > 1 else (lambda i: 0)
    ones, perm = _head_group_matrices()
    const = pl.BlockSpec(ones.shape, lambda i, j: (0, 0))
    vec = pl.BlockSpec((1, 1, k), lambda i, j: (bsel(i), 0, 0))
    args = [x, gain.reshape(1, k), shift, scale, w, col_gain, ones]
    specs = [pl.BlockSpec((tm, k), lambda i, j: (i, 0)),
             pl.BlockSpec((1, k), lambda i, j: (0, 0)), vec, vec,
             pl.BlockSpec((k, tn), lambda i, j: (0, j)),
             pl.BlockSpec((1, tn), lambda i, j: (0, j)), const]
    if rope:
        table = pl.BlockSpec((tm, HEAD_DIM), lambda i, j: (i % tiles_per_batch, 0))
        args += [perm, cos, sin]
        specs += [const, table, table]
    return pl.pallas_call(
        functools.partial(_qkv_kernel, normed_cols=normed_cols, rope=rope),
        grid=(t // tm, n // tn),
        in_specs=specs,
        out_specs=pl.BlockSpec((tm, tn), lambda i, j: (i, j)),
        out_shape=jax.ShapeDtypeStruct((t, n), BF16),
        scratch_shapes=[pltpu.VMEM((tm, k), BF16)],
        compiler_params=_params("parallel", "arbitrary"),
        name=name,
    )(*args)


FFN_HID_SPLIT = 2


def _ffn_kernel(*refs, n_cast):
    x_ref, gain_ref, shift_ref, scale_ref, gate_ref, wg_ref, wu_ref, wo_ref = refs[:8]
    cast_src = refs[8:8 + n_cast]
    o_ref = refs[8 + n_cast]
    cast_dst = refs[9 + n_cast:9 + 2 * n_cast]
    xn_ref = refs[9 + 2 * n_cast]
    j = pl.program_id(1)
    hc = wg_ref.shape[1] // FFN_HID_SPLIT
    cols = [slice(c * hc, (c + 1) * hc) for c in range(FFN_HID_SPLIT)]
    for src, dst in zip(cast_src, cast_dst):
        dst[...] = src[...].astype(dst.dtype)

    def chunk(first, last):
        if first:
            _norm_mod_rows(x_ref, xn_ref, gain_ref[...], shift_ref[0], scale_ref[0])
        xn = xn_ref[...]
        gu = [(jnp.dot(xn, wg_ref[:, cs], preferred_element_type=F32),
               jnp.dot(xn, wu_ref[:, cs], preferred_element_type=F32)) for cs in cols]
        for c, (g, u) in enumerate(gu):
            a = (jax.nn.silu(g) * u).astype(BF16)
            part = jnp.dot(a, wo_ref[cols[c], :], preferred_element_type=F32)
            if first and c == 0:
                o_ref[...] = part
            elif last and c == len(gu) - 1:
                o_ref[...] = x_ref[...] + gate_ref[0] * (o_ref[...] + part)
            else:
                o_ref[...] += part

    n_chunks = pl.num_programs(1)
    pl.when(j == 0)(functools.partial(chunk, True, False))
    pl.when((j > 0) & (j < n_chunks - 1))(functools.partial(chunk, False, False))
    pl.when(j == n_chunks - 1)(functools.partial(chunk, False, True))


def _ffn(x, gain, shift, scale, gate, w_in, w_out, *, tm, th, name, rows_per_batch=None, cast=()):
    t, d = x.shape
    hidden = w_out.shape[0]
    assert t % tm == 0 and hidden % th == 0 and hidden // th >= 2
    nh = hidden // th
    n_steps = (t // tm) * nh
    rows_per_batch = rows_per_batch or t
    tiles_per_batch = rows_per_batch // tm
    nb = shift.shape[0]
    bsel = (lambda i: i // tiles_per_batch) if nb > 1 else (lambda i: 0)
    vec = pl.BlockSpec((1, 1, d), lambda i, j: (bsel(i), 0, 0))
    slabs = list(cast)
    slab_specs = []
    if slabs:
        nw_in, nw_out = slabs
        in_rows, in_cols = nw_in.shape[0] // (t // tm), nw_in.shape[1] // nh
        out_rows = nw_out.shape[0] // n_steps
        assert in_rows * (t // tm) == nw_in.shape[0] and in_cols * nh == nw_in.shape[1] and in_cols % LANES == 0
        assert out_rows * n_steps == nw_out.shape[0] and in_rows % BF16_ROWS == 0 and out_rows % BF16_ROWS == 0
        slab_specs = [pl.BlockSpec((in_rows, in_cols), lambda i, j: (i, j)),
                      pl.BlockSpec((out_rows, nw_out.shape[1]), lambda i, j: (i * nh + j, 0))]
    outs = pl.pallas_call(
        functools.partial(_ffn_kernel, n_cast=len(slabs)),
        grid=(t // tm, nh),
        in_specs=[pl.BlockSpec((tm, d), lambda i, j: (i, 0)),
                  pl.BlockSpec((1, d), lambda i, j: (0, 0)),
                  vec, vec, vec,
                  pl.BlockSpec((d, th), lambda i, j: (0, j)),
                  pl.BlockSpec((d, th), lambda i, j: (0, nh + j)),
                  pl.BlockSpec((th, d), lambda i, j: (j, 0))] + slab_specs,
        out_specs=[pl.BlockSpec((tm, d), lambda i, j: (i, 0))] + slab_specs,
        out_shape=[jax.ShapeDtypeStruct((t, d), F32)] + [jax.ShapeDtypeStruct(s.shape, BF16) for s in slabs],
        scratch_shapes=[pltpu.VMEM((tm, d), BF16)],
        compiler_params=_params("parallel", "arbitrary"),
        name=name,
    )(x, gain.reshape(1, d), shift, scale, gate, w_in, w_in, w_out, *slabs)
    return outs[0], tuple(outs[1:])


def _fill_kv(k_s, v_s, kv_refs):
    off = 0
    for k_ref, v_ref in kv_refs:
        n = k_ref.shape[0]
        k_s[off:off + n, :] = k_ref[...]
        v_s[off:off + n, :] = v_ref[...]
        off += n


ATTN_SUBTILE = 256


def _gqa_kernel(*refs, n_kv_src, n_group):
    q_ref = refs[0]
    kv_refs = [(refs[1 + 2 * s], refs[2 + 2 * s]) for s in range(n_kv_src)]
    o_ref, k_s, v_s = refs[1 + 2 * n_kv_src:]

    @pl.when(pl.program_id(2) == 0)
    def _():
        _fill_kv(k_s, v_s, kv_refs)

    k = k_s[...]
    v = v_s[...]
    sub = min(ATTN_SUBTILE, q_ref.shape[0])
    chains = [(slice(r0, r0 + sub), slice(g * HEAD_DIM, (g + 1) * HEAD_DIM))
              for r0 in range(0, q_ref.shape[0], sub) for g in range(n_group)]

    def scores(c):
        return lax.dot_general(q_ref[chains[c]], k, NT_DIMS, preferred_element_type=F32)

    s = scores(0)
    for c in range(len(chains)):
        s_next = scores(c + 1) if c + 1 < len(chains) else None
        p = jnp.exp2(s - jnp.max(s, axis=-1, keepdims=True))
        l = jnp.sum(p, axis=-1, keepdims=True)
        o = jnp.dot(p.astype(BF16), v, preferred_element_type=F32)
        o_ref[chains[c]] = (o / l).astype(o_ref.dtype)
        s = s_next


def _gqa_attention(q_src, kv_srcs, *, batch, n_q_heads, n_kv_heads, tq, q_rows, name):
    group = n_q_heads // n_kv_heads
    gw = group * HEAD_DIM
    nq = q_rows // tq
    k_blk0 = n_q_heads
    v_blk0 = n_q_heads + n_kv_heads
    specs = [pl.BlockSpec((tq, gw), lambda b, h, i: (b * nq + i, h))]
    args = [q_src]
    total = 0
    for src in kv_srcs:
        rows = src.shape[0] // batch
        total += rows
        specs += [pl.BlockSpec((rows, HEAD_DIM), lambda b, h, i: (b, k_blk0 + h)),
                  pl.BlockSpec((rows, HEAD_DIM), lambda b, h, i: (b, v_blk0 + h))]
        args += [src, src]
    kern = functools.partial(_gqa_kernel, n_kv_src=len(kv_srcs), n_group=group)
    return pl.pallas_call(
        kern,
        grid=(batch, n_kv_heads, nq),
        in_specs=specs,
        out_specs=pl.BlockSpec((tq, gw), lambda b, h, i: (b * nq + i, h)),
        out_shape=jax.ShapeDtypeStruct((batch * q_rows, n_q_heads * HEAD_DIM), BF16),
        scratch_shapes=[pltpu.VMEM((total, HEAD_DIM), BF16), pltpu.VMEM((total, HEAD_DIM), BF16)],
        compiler_params=_params("parallel", "parallel", "arbitrary"),
        name=name,
    )(*args)


def _diff_kernel(*refs, n_kv_src, lam_init):
    q_ref = refs[0]
    kv_refs = [(refs[1 + 2 * s], refs[2 + 2 * s]) for s in range(n_kv_src)]
    lq1, lk1, lq2, lk2, gain_ref, o_ref, k_s, v_s = refs[1 + 2 * n_kv_src:]

    @pl.when(pl.program_id(2) == 0)
    def _():
        _fill_kv(k_s, v_s, kv_refs)

    lam = (jnp.exp(jnp.sum(lq1[...] * lk1[...], axis=-1, keepdims=True))
           - jnp.exp(jnp.sum(lq2[...] * lk2[...], axis=-1, keepdims=True)) + lam_init)
    dh = HEAD_DIM
    sub = min(ATTN_SUBTILE, q_ref.shape[0])
    chains = [slice(r0, r0 + sub) for r0 in range(0, q_ref.shape[0], sub)]

    def scores(c):
        return [lax.dot_general(q_ref[chains[c], m * dh:(m + 1) * dh], k_s[:, m * dh:(m + 1) * dh], NT_DIMS,
                                preferred_element_type=F32) for m in range(2)]

    s = scores(0)
    for c in range(len(chains)):
        s_next = scores(c + 1) if c + 1 < len(chains) else None
        p = [jnp.exp2(sm - jnp.max(sm, axis=-1, keepdims=True)) for sm in s]
        l0, l1 = [jnp.sum(pm, axis=-1, keepdims=True) for pm in p]
        w = p[0] - p[1] * (lam * l0 / l1)
        o = jnp.dot(w.astype(BF16), v_s[...], preferred_element_type=F32) * (1.0 / l0)
        o_ref[chains[c], :] = ((_rms(o) * gain_ref[...]) * (1.0 - lam_init)).astype(o_ref.dtype)
        s = s_next


def _diff_attention(q_src, kv_srcs, lams, out_gain, *, batch, n_heads, tq, q_rows, lam_init, name):
    hw = 2 * HEAD_DIM
    nq = q_rows // tq
    specs = [pl.BlockSpec((tq, hw), lambda b, h, i: (b * nq + i, h))]
    args = [q_src]
    total = 0
    for src in kv_srcs:
        rows = src.shape[0] // batch
        total += rows
        specs += [pl.BlockSpec((rows, hw), lambda b, h, i: (b, n_heads + h)),
                  pl.BlockSpec((rows, hw), lambda b, h, i: (b, 2 * n_heads + h))]
        args += [src, src]
    small = pl.BlockSpec((1, HEAD_DIM), lambda b, h, i: (0, 0))
    specs += [small] * 4 + [pl.BlockSpec((1, hw), lambda b, h, i: (0, 0))]
    args += [v.reshape(1, HEAD_DIM) for v in lams] + [out_gain.reshape(1, hw)]
    kern = functools.partial(_diff_kernel, n_kv_src=len(kv_srcs), lam_init=lam_init)
    return pl.pallas_call(
        kern,
        grid=(batch, n_heads, nq),
        in_specs=specs,
        out_specs=pl.BlockSpec((tq, hw), lambda b, h, i: (b * nq + i, h)),
        out_shape=jax.ShapeDtypeStruct((batch * q_rows, n_heads * hw), BF16),
        scratch_shapes=[pltpu.VMEM((total, hw), BF16), pltpu.VMEM((total, hw), BF16)],
        compiler_params=_params("parallel", "parallel", "arbitrary"),
        name=name,
    )(*args)


def _gla_blocks(blocks):
    r, dk = blocks[0][0].shape
    c = GLA_CHUNK
    nc = r // c
    n = range(len(blocks))
    qs_, ks_, vs_, zs_, wgs, bgs, st_refs, revs = zip(*blocks)
    row = lax.broadcasted_iota(jnp.int32, (r, r), 0)
    col = lax.broadcasted_iota(jnp.int32, (r, r), 1)
    same = row // c == col // c
    tri_f = [jnp.where(same, jnp.where((col >= row) if revs[i] else (col <= row), 1.0, 0.0), 0.0) for i in n]
    tri = [t.astype(BF16) for t in tri_f]
    mid = [c // 2 if revs[i] else c // 2 - 1 for i in n]
    last = [0 if revs[i] else c - 1 for i in n]

    g = [jax.nn.log_sigmoid(jnp.dot(zs_[i], wgs[i], preferred_element_type=F32) + bgs[i]) / GLA_TAU for i in n]
    g_hi = [g[i].astype(BF16) for i in n]
    g_lo = [(g[i] - g_hi[i].astype(F32)).astype(BF16) for i in n]
    cum = [jnp.dot(tri[i], g_hi[i], preferred_element_type=F32) + jnp.dot(tri[i], g_lo[i], preferred_element_type=F32)
           for i in n]

    def chunk_row(x, idx):
        return jnp.concatenate(
            [jnp.broadcast_to(x[ci * c + idx:ci * c + idx + 1, :], (c, dk)) for ci in range(nc)], axis=0)

    cum_mid = [chunk_row(cum[i], mid[i]) for i in n]
    cum_last = [chunk_row(cum[i], last[i]) for i in n]
    qs = [(qs_[i] * jnp.exp(cum[i] - cum_mid[i])).astype(BF16) for i in n]
    ks = [(ks_[i] * jnp.exp(cum_mid[i] - cum[i])).astype(BF16) for i in n]
    a = [lax.dot_general(qs[i], ks[i], NT_DIMS, preferred_element_type=F32) for i in n]
    q_inter = [qs_[i] * jnp.exp(cum[i]) for i in n]
    k_carry = [ks_[i] * jnp.exp(cum_last[i] - cum[i]) for i in n]

    assert nc % 2 == 0
    same_pair = row // (2 * c) == col // (2 * c)
    pair_f = [jnp.where(same_pair, jnp.where((col // c > row // c) if revs[i] else (col // c < row // c), 1.0, 0.0),
                        0.0) for i in n]
    a_pair = [lax.dot_general(q_inter[i].astype(BF16), k_carry[i].astype(BF16), NT_DIMS,
                              preferred_element_type=F32) for i in n]
    a = [(jnp.where(tri_f[i] > 0.5, a[i], 0.0) + jnp.where(pair_f[i] > 0.5, a_pair[i], 0.0)).astype(BF16) for i in n]
    o_intra = [jnp.dot(a[i], vs_[i], preferred_element_type=F32) for i in n]

    def tot(i, ci):
        return cum[i][ci * c + last[i]:ci * c + last[i] + 1, :]

    def is_first(i, ci):
        return (ci % 2 == 1) if revs[i] else (ci % 2 == 0)

    def pair_scaled(i, x, scale_first):
        parts = []
        for ci in range(nc):
            rows = slice(ci * c, (ci + 1) * c)
            if is_first(i, ci) == scale_first:
                parts.append(x[rows] * jnp.exp(tot(i, ci ^ 1)))
            else:
                parts.append(x[rows])
        return jnp.concatenate(parts, axis=0).astype(BF16)

    q_sup = [pair_scaled(i, q_inter[i], False) for i in n]
    k_sup = [pair_scaled(i, k_carry[i], True) for i in n]
    st = [st_refs[i][...] for i in n]
    n_pairs = nc // 2
    o_inter = [[None] * n_pairs for _ in n]
    for step in range(n_pairs):
        for i in n:
            pg = n_pairs - 1 - step if revs[i] else step
            rows = slice(pg * 2 * c, (pg + 1) * 2 * c)
            o_inter[i][pg] = lax.dot_general(q_sup[i][rows], st[i].astype(BF16), NT_DIMS,
                                             preferred_element_type=F32)
            decay = jnp.exp(tot(i, 2 * pg) + tot(i, 2 * pg + 1))
            st[i] = st[i] * decay + lax.dot_general(vs_[i][rows], k_sup[i][rows], TN_DIMS,
                                                    preferred_element_type=F32)
    for i in n:
        st_refs[i][...] = st[i]
    return [o_intra[i] + jnp.concatenate(o_inter[i], axis=0) for i in n]


def _gla_kernel(qc_ref, kc_ref, vc_ref, rc_ref, zc_ref, ql_ref, kl_ref, vl_ref, rl_ref, zl_ref,
                wgf_ref, bgf_ref, wgb_ref, bgb_ref, gain_ref, oc_ref, ol_ref, sf_ref, sb_ref, of_ref, ob_ref):
    blk = GLA_BLOCK
    n_ctx = qc_ref.shape[0] // blk
    n_lat = ql_ref.shape[0] // blk
    q_scale = qc_ref.shape[1] ** -0.5
    ctx_refs = (qc_ref, kc_ref, vc_ref, zc_ref)
    lat_refs = (ql_ref, kl_ref, vl_ref, zl_ref)

    def rows_of(bi):
        return pl.ds(bi * blk if isinstance(bi, int) else pl.multiple_of(bi * blk, blk), blk)

    def block(refs, bi, reverse):
        q_ref, k_ref, v_ref, z_ref = refs
        rows = rows_of(bi)
        wg_ref, bg_ref, st_ref = (wgb_ref, bgb_ref, sb_ref) if reverse else (wgf_ref, bgf_ref, sf_ref)
        return (q_ref[rows, :].astype(F32) * q_scale, k_ref[rows, :].astype(F32), v_ref[rows, :], z_ref[rows, :],
                wg_ref[0], bg_ref[0], st_ref, reverse)

    def emit(r_ref, out_ref, bi, o):
        rows = rows_of(bi)
        y = (_rms(o) * gain_ref[...]) * jax.nn.silu(r_ref[rows, :].astype(F32))
        out_ref[rows, :] = y.astype(out_ref.dtype)

    def scan(refs, r_ref, out_ref, base, n):
        def both(t):
            return _gla_blocks([block(refs, t, False), block(refs, n - 1 - t, True)])

        def park(t, carry):
            o_f, o_b = both(t)
            of_ref[rows_of(base + t), :] = o_f
            ob_ref[rows_of(base + n - 1 - t), :] = o_b
            return carry

        def meet(t, carry):
            o_f, o_b = both(t)
            emit(r_ref, out_ref, t, o_f + ob_ref[rows_of(base + t), :])
            emit(r_ref, out_ref, n - 1 - t, of_ref[rows_of(base + n - 1 - t), :] + o_b)
            return carry

        lax.fori_loop(0, n // 2, park, 0)
        if n % 2:
            o_f, o_b = both(n // 2)
            emit(r_ref, out_ref, n // 2, o_f + o_b)
        lax.fori_loop((n + 1) // 2, n, meet, 0)

    sf_ref[...] = jnp.zeros_like(sf_ref)
    sb_ref[...] = jnp.zeros_like(sb_ref)
    scan(ctx_refs, rc_ref, oc_ref, 0, n_ctx)
    scan(lat_refs, rl_ref, ol_ref, n_ctx, n_lat)


def _gla(p_ctx, p_lat, wgf, bgf, wgb, bgb, out_gain, *, batch, dk, dv):
    h = GLA_HEADS
    rows_c = p_ctx.shape[0] // batch
    rows_l = p_lat.shape[0] // batch
    assert rows_c % GLA_BLOCK == 0 and rows_l % GLA_BLOCK == 0
    zblk = (2 * h * dk + 2 * h * dv) // LANES
    k0 = h
    v0 = (2 * h * dk) // dv
    r0 = v0 + h

    def stream(rows):
        return [pl.BlockSpec((rows, dk), lambda b, hh: (b, hh)),
                pl.BlockSpec((rows, dk), lambda b, hh: (b, k0 + hh)),
                pl.BlockSpec((rows, dv), lambda b, hh: (b, v0 + hh)),
                pl.BlockSpec((rows, dv), lambda b, hh: (b, r0 + hh)),
                pl.BlockSpec((rows, LANES), lambda b, hh: (b, zblk))]

    wspec = pl.BlockSpec((1, LANES, dk), lambda b, hh: (hh, 0, 0))
    bspec = pl.BlockSpec((1, 1, dk), lambda b, hh: (hh, 0, 0))
    return pl.pallas_call(
        _gla_kernel,
        grid=(batch, h),
        in_specs=stream(rows_c) + stream(rows_l) + [wspec, bspec, wspec, bspec,
                                                    pl.BlockSpec((1, dv), lambda b, hh: (0, 0))],
        out_specs=[pl.BlockSpec((rows_c, dv), lambda b, hh: (b, hh)),
                   pl.BlockSpec((rows_l, dv), lambda b, hh: (b, hh))],
        out_shape=[jax.ShapeDtypeStruct((batch * rows_c, h * dv), BF16),
                   jax.ShapeDtypeStruct((batch * rows_l, h * dv), BF16)],
        scratch_shapes=[pltpu.VMEM((dv, dk), F32), pltpu.VMEM((dv, dk), F32),
                        pltpu.VMEM((rows_c + rows_l, dv), F32), pltpu.VMEM((rows_c + rows_l, dv), F32)],
        compiler_params=_params("parallel", "parallel"),
        name="gla",
    )(*([p_ctx] * 5 + [p_lat] * 5 + [wgf, bgf, wgb, bgb, out_gain.reshape(1, dv)]))


def _dft_tables(n):
    idx = jnp.arange(n, dtype=jnp.int32)
    ang = ((idx[:, None] * idx[None, :]) % n).astype(F32) * (2.0 * math.pi / n)
    return jnp.cos(ang).astype(BF16), jnp.sin(ang).astype(BF16)


def _fnet_chan_kernel(x_ref, gain_ref, shift_ref, scale_ref, csc_ref, p_ref, q_ref, xn_ref):
    gd = csc_ref.shape[0]
    _norm_mod_rows(x_ref, xn_ref, gain_ref[...], shift_ref[0], scale_ref[0])
    for g in range(x_ref.shape[1] // gd):
        cols = slice(g * gd, (g + 1) * gd)
        pq = jnp.dot(xn_ref[:, cols], csc_ref[...], preferred_element_type=F32)
        p_ref[:, cols] = pq[:, :gd].astype(p_ref.dtype)
        q_ref[:, cols] = pq[:, gd:].astype(q_ref.dtype)


def _fnet_seq_kernel(cs_ref, ss_ref, p_ref, q_ref, o_ref, *, inv_norm):
    acc = (jnp.dot(cs_ref[...], p_ref[...], preferred_element_type=F32)
           - jnp.dot(ss_ref[...], q_ref[...], preferred_element_type=F32))
    o_ref[...] = (acc * inv_norm).astype(o_ref.dtype)


def _fnet(x, gain, shift, scale, *, batch, tm):
    t, d = x.shape
    s = t // batch
    gd = d // FNET_GROUPS
    tiles_per_batch = s // tm
    cc, sc = _dft_tables(gd)
    cs, ss = _dft_tables(s)
    vec = pl.BlockSpec((1, 1, d), lambda i: (i // tiles_per_batch, 0, 0))
    blk = pl.BlockSpec((tm, d), lambda i: (i, 0))
    p, q = pl.pallas_call(
        _fnet_chan_kernel,
        grid=(t // tm,),
        in_specs=[blk, pl.BlockSpec((1, d), lambda i: (0, 0)), vec, vec,
                  pl.BlockSpec((gd, 2 * gd), lambda i: (0, 0))],
        out_specs=[blk, blk],
        out_shape=[jax.ShapeDtypeStruct((t, d), BF16)] * 2,
        scratch_shapes=[pltpu.VMEM((tm, d), BF16)],
        compiler_params=_params("parallel"),
        name="fnet_chan",
    )(x, gain.reshape(1, d), shift, scale, jnp.concatenate([cc, sc], axis=1))
    rows = pl.BlockSpec((tm, s), lambda b, j, i: (i, 0))
    cols = pl.BlockSpec((s, gd), lambda b, j, i: (b, j))
    return pl.pallas_call(
        functools.partial(_fnet_seq_kernel, inv_norm=float((s * gd) ** -0.5)),
        grid=(batch, d // gd, tiles_per_batch),
        in_specs=[rows, rows, cols, cols],
        out_specs=pl.BlockSpec((tm, gd), lambda b, j, i: (b * tiles_per_batch + i, j)),
        out_shape=jax.ShapeDtypeStruct((t, d), BF16),
        compiler_params=_params("parallel", "parallel", "arbitrary"),
        name="fnet_seq",
    )(cs, ss, p, q)


def _rope_tables(n_tokens):
    t = jnp.arange(n_tokens)
    row = (t // GRID_W).astype(F32)
    col = (t % GRID_W).astype(F32)
    half = HEAD_DIM // 2
    inv_freq = ROPE_THETA ** (-jnp.arange(0, half, 2, dtype=F32) / half)
    ang_r = row[:, None] * inv_freq[None, :]
    ang_c = col[:, None] * inv_freq[None, :]
    ang = jnp.concatenate([ang_r, ang_r, ang_c, ang_c], axis=-1)
    sign = jnp.concatenate([-jnp.ones((half // 2,), F32), jnp.ones((half // 2,), F32)] * 2)
    return jnp.cos(ang), jnp.sin(ang) * sign


def _column_gains(q_gain, k_gain, q_cols, k_cols, v_cols):
    qg = jnp.tile(q_gain.astype(F32) * (HEAD_DIM ** -0.5 * math.log2(math.e)), q_cols // HEAD_DIM)
    kg = jnp.tile(k_gain.astype(F32), k_cols // HEAD_DIM)
    return jnp.concatenate([qg, kg, jnp.ones((v_cols,), F32)])[None, :]


def _gate_weights(wg, bg, lane0, dk):
    r = wg.shape[0]
    w = wg.reshape(r, GLA_HEADS, dk).transpose(1, 0, 2)
    w = jnp.pad(w, ((0, 0), (lane0, LANES - lane0 - r), (0, 0))).astype(BF16)
    return w, bg.reshape(GLA_HEADS, 1, dk).astype(F32)


def kernel(x, c, ctx, c_ctx, l0_mod_w, l0_mod_b, l0_norm1, l0_gla_w_in, l0_gla_wg_f, l0_gla_bg_f, l0_gla_wg_b, l0_gla_bg_b, l0_gla_out_norm, l0_gla_w_out, l0_norm2, l0_ffn_w_in, l0_ffn_w_out, l1_mod_w, l1_mod_b, l1_norm1, l1_gqa_w_in, l1_gqa_q_norm, l1_gqa_k_norm, l1_gqa_w_out, l1_norm2, l1_ffn_w_in, l1_ffn_w_out, l2_mod_w, l2_mod_b, l2_norm1, l2_diff_w_in, l2_diff_q_norm, l2_diff_k_norm, l2_diff_lq1, l2_diff_lk1, l2_diff_lq2, l2_diff_lk2, l2_diff_out_norm, l2_diff_w_out, l2_norm2, l2_ffn_w_in, l2_ffn_w_out, l3_mod_w, l3_mod_b, l3_norm1, l3_fnet_w_out, l3_norm2, l3_ffn_w_in, l3_ffn_w_out):
    b, s, d = x.shape
    n_ctx = ctx.shape[1]
    xl = x.reshape(b * s, d)
    xc = ctx.reshape(b * n_ctx, d)
    tm = math.gcd(s, 1024)
    tm_c = math.gcd(b * n_ctx, 1024)
    tm_r = math.gcd(s, 512)
    tm_rc = math.gcd(b * n_ctx, 512)
    th = math.gcd(l0_ffn_w_out.shape[0], 512)
    tq_gqa = math.gcd(s, 4 * ATTN_SUBTILE)
    tq_diff = math.gcd(s, 4 * ATTN_SUBTILE)

    n_cond = -(-(b + 1) // BF16_ROWS) * BF16_ROWS
    cond = jnp.concatenate([c, c_ctx[None, :], jnp.zeros((n_cond - b - 1, d), F32)], axis=0)

    def modulation(mod_w, mod_b):
        m = _mm(cond, mod_w, tm=n_cond, tn=math.gcd(mod_w.shape[1], 1024), out_dtype=F32, prologue="silu", epilogue="bias", bias=mod_b,
                name="modulation")
        lat = [m[:b, k * d:(k + 1) * d].reshape(b, 1, d) for k in range(6)]
        cx = [m[b:b + 1, k * d:(k + 1) * d].reshape(1, 1, d) for k in range(6)]
        return lat, cx

    def tiling(rows, resident=False):
        if rows == s:
            return dict(tm=tm_r if resident else tm, rows_per_batch=s)
        return dict(tm=tm_rc if resident else tm_c, rows_per_batch=None)

    def proj(xs, w, n1, sh, sc, rows, tn, name, **kw):
        return _mm(xs, w, tn=tn, out_dtype=BF16, prologue="norm_mod", gain=n1, shift=sh, scale=sc, name=name,
                   **tiling(rows), **kw)

    def qkv_proj(xs, w, n1, sh, sc, col_gain, rows, name, **kw):
        return _qkv_proj(xs, w, n1, sh, sc, col_gain, tn=math.gcd(w.shape[1], 1024), name=name, **tiling(rows), **kw)

    def out_resid(y, w, xs, gate, rows):
        return _mm(y, w.astype(BF16), tn=w.shape[1], out_dtype=F32, epilogue="resid", res=xs, gate=gate,
                   name="out_resid", **tiling(rows, resident=True))

    def ffn(xs, n2, sh, sc, gate, w_bf16, rows, next_w=()):
        return _ffn(xs, n2, sh, sc, gate, *w_bf16, th=th, name="ffn", cast=next_w, **tiling(rows))

    ffn_w = (l0_ffn_w_in.astype(BF16), l0_ffn_w_out.astype(BF16))

    (sh1, sc1, g1, sh2, sc2, g2), (csh1, csc1, cg1, csh2, csc2, cg2) = modulation(l0_mod_w, l0_mod_b)
    dk = l0_gla_wg_f.shape[1] // GLA_HEADS
    dv = d // GLA_HEADS
    n_in = l0_gla_w_in.shape[1]
    tn0 = 1280
    n_pad = -(-(n_in - 2 * GLA_RANK + LANES) // tn0) * tn0
    w0 = jnp.pad(l0_gla_w_in, ((0, 0), (0, n_pad - n_in))).astype(BF16)
    pl0 = proj(xl, w0, l0_norm1, sh1, sc1, s, tn0, "gla_proj")
    pc0 = proj(xc, w0, l0_norm1, csh1, csc1, n_ctx, tn0, "gla_proj_ctx")
    wgf, bgf = _gate_weights(l0_gla_wg_f, l0_gla_bg_f, 0, dk)
    wgb, bgb = _gate_weights(l0_gla_wg_b, l0_gla_bg_b, GLA_RANK, dk)
    yc, yl = _gla(pc0, pl0, wgf, bgf, wgb, bgb, l0_gla_out_norm, batch=b, dk=dk, dv=dv)
    xl = out_resid(yl, l0_gla_w_out, xl, g1, s)
    xc = out_resid(yc, l0_gla_w_out, xc, cg1, n_ctx)
    xl, next_ffn_w = ffn(xl, l0_norm2, sh2, sc2, g2, ffn_w, s, next_w=(l1_ffn_w_in, l1_ffn_w_out))
    xc, _ = ffn(xc, l0_norm2, csh2, csc2, cg2, ffn_w, n_ctx)
    ffn_w = next_ffn_w

    (sh1, sc1, g1, sh2, sc2, g2), (csh1, csc1, cg1, csh2, csc2, cg2) = modulation(l1_mod_w, l1_mod_b)
    n_heads = d // HEAD_DIM
    cos, sin = _rope_tables(s)
    kv_cols = GQA_KV_HEADS * HEAD_DIM
    cg_gqa = _column_gains(l1_gqa_q_norm, l1_gqa_k_norm, d, kv_cols, kv_cols)
    w1 = l1_gqa_w_in.astype(BF16)
    pl1 = qkv_proj(xl, w1, l1_norm1, sh1, sc1, cg_gqa, s, "gqa_proj", normed_cols=d + kv_cols, cos=cos, sin=sin)
    pc1 = qkv_proj(xc, w1, l1_norm1, csh1, csc1, cg_gqa, n_ctx, "gqa_proj_ctx", normed_cols=d + kv_cols)
    yl = _gqa_attention(pl1, [pl1, pc1], batch=b, n_q_heads=n_heads, n_kv_heads=GQA_KV_HEADS, tq=tq_gqa, q_rows=s,
                        name="gqa_attn")
    yc = _gqa_attention(pc1, [pc1], batch=b, n_q_heads=n_heads, n_kv_heads=GQA_KV_HEADS, tq=n_ctx, q_rows=n_ctx,
                        name="gqa_attn_ctx")
    xl = out_resid(yl, l1_gqa_w_out, xl, g1, s)
    xc = out_resid(yc, l1_gqa_w_out, xc, cg1, n_ctx)
    xl, next_ffn_w = ffn(xl, l1_norm2, sh2, sc2, g2, ffn_w, s, next_w=(l2_ffn_w_in, l2_ffn_w_out))
    xc, _ = ffn(xc, l1_norm2, csh2, csc2, cg2, ffn_w, n_ctx)
    ffn_w = next_ffn_w

    (sh1, sc1, g1, sh2, sc2, g2), (csh1, csc1, _, _, _, _) = modulation(l2_mod_w, l2_mod_b)
    lam_init = 0.8 - 0.6 * math.exp(-0.3 * 2)
    cg_diff = _column_gains(l2_diff_q_norm, l2_diff_k_norm, d, d, d)
    w2 = l2_diff_w_in.astype(BF16)
    pl2 = qkv_proj(xl, w2, l2_norm1, sh1, sc1, cg_diff, s, "diff_proj", normed_cols=2 * d, cos=cos, sin=sin)
    pc2 = qkv_proj(xc, w2, l2_norm1, csh1, csc1, cg_diff, n_ctx, "diff_proj_ctx", normed_cols=2 * d)
    yl = _diff_attention(pl2, [pl2, pc2], (l2_diff_lq1, l2_diff_lk1, l2_diff_lq2, l2_diff_lk2),
                         l2_diff_out_norm, batch=b, n_heads=n_heads // 2, tq=tq_diff, q_rows=s, lam_init=lam_init,
                         name="diff_attn")
    xl = out_resid(yl, l2_diff_w_out, xl, g1, s)
    xl, ffn_w = ffn(xl, l2_norm2, sh2, sc2, g2, ffn_w, s, next_w=(l3_ffn_w_in, l3_ffn_w_out))

    (sh1, sc1, g1, sh2, sc2, g2), _ = modulation(l3_mod_w, l3_mod_b)
    yl = _fnet(xl, l3_norm1, sh1, sc1, batch=b, tm=tm)
    xl = out_resid(yl, l3_fnet_w_out, xl, g1, s)
    xl, _ = ffn(xl, l3_norm2, sh2, sc2, g2, ffn_w, s)
    return xl.reshape(b, s, d)
```

```python
import functools
import math

import jax
import jax.numpy as jnp
from jax import lax
from jax.experimental import pallas as pl
from jax.experimental.pallas import tpu as pltpu

F32 = jnp.float32
BF16 = jnp.bfloat16

NORM_EPS = 1e-6
ROPE_THETA = 10000.0
GRID_W = 64
HEAD_DIM = 128
GQA_KV_HEADS = 4
GLA_HEADS = 4
GLA_RANK = 16
GLA_TAU = 16.0
GLA_CHUNK = 64
GLA_BLOCK = 4 * GLA_CHUNK
FNET_GROUPS = 4

LANES = 128
BF16_ROWS = 16
STRIP_UNROLL = 8
VMEM_LIMIT = 56 * 1024 * 1024

NT_DIMS = (((1,), (1,)), ((), ()))
TN_DIMS = (((0,), (0,)), ((), ()))


def _params(*sem):
    return pltpu.CompilerParams(dimension_semantics=sem, vmem_limit_bytes=VMEM_LIMIT)


def _rms(x, eps=NORM_EPS):
    return x * lax.rsqrt(jnp.mean(x * x, axis=-1, keepdims=True) + eps)


def _norm_mod_rows(x_ref, xn_ref, gain, shift, scale):
    mult = gain * (1.0 + scale)

    def strip(r, carry):
        rows = pl.ds(pl.multiple_of(r * BF16_ROWS, BF16_ROWS), BF16_ROWS)
        xn_ref[rows, :] = (_rms(x_ref[rows, :]) * mult + shift).astype(BF16)
        return carry

    lax.fori_loop(0, x_ref.shape[0] // BF16_ROWS, strip, 0, unroll=STRIP_UNROLL)


ROPE_STRIP = 256
MXU_WIDTH = 256


def _head_group_matrices():
    src = lax.broadcasted_iota(jnp.int32, (MXU_WIDTH, MXU_WIDTH), 0)
    dst = lax.broadcasted_iota(jnp.int32, (MXU_WIDTH, MXU_WIDTH), 1)
    quarter = HEAD_DIM // 4
    partner = jnp.where((dst // quarter) % 2 == 0, dst + quarter, dst - quarter)
    mean = jnp.where(src // HEAD_DIM == dst // HEAD_DIM, 1.0 / HEAD_DIM, 0.0).astype(BF16)
    return mean, (src == partner).astype(BF16)


def _mm_kernel(*refs, prologue, epilogue):
    it = iter(refs)
    x_ref = next(it)
    if prologue == "norm_mod":
        gain_ref, shift_ref, scale_ref = next(it), next(it), next(it)
    w_ref = next(it)
    if epilogue == "bias":
        b_ref = next(it)
    elif epilogue == "resid":
        res_ref, gate_ref = next(it), next(it)
    o_ref = next(it)
    if prologue != "cast":
        xn_ref = next(it)
    j = pl.program_id(1)

    if prologue == "cast":
        a = x_ref[...].astype(BF16)
    else:
        @pl.when(j == 0)
        def _():
            if prologue == "norm_mod":
                _norm_mod_rows(x_ref, xn_ref, gain_ref[...], shift_ref[0], scale_ref[0])
            else:
                xn_ref[...] = jax.nn.silu(x_ref[...]).astype(BF16)

        a = xn_ref[...]
    acc = jnp.dot(a, w_ref[...].astype(BF16), preferred_element_type=F32)
    if epilogue == "store":
        o_ref[...] = acc.astype(o_ref.dtype)
    elif epilogue == "bias":
        o_ref[...] = (acc + b_ref[...]).astype(o_ref.dtype)
    elif epilogue == "resid":
        o_ref[...] = (res_ref[...] + gate_ref[0] * acc).astype(o_ref.dtype)


def _mm(x, w, *, tm, tn, out_dtype, name, prologue="cast", epilogue="store", rows_per_batch=None,
        gain=None, shift=None, scale=None, bias=None, res=None, gate=None):
    t, k = x.shape
    n = w.shape[1]
    assert t % tm == 0 and n % tn == 0, (t, tm, n, tn)
    rows_per_batch = rows_per_batch or t
    assert rows_per_batch % tm == 0
    tiles_per_batch = rows_per_batch // tm
    once = dict(pipeline_mode=pl.Buffered(1))

    def batch_of(i):
        return i // tiles_per_batch

    args = [x]
    specs = [pl.BlockSpec((tm, k), lambda i, j: (i, 0))]
    if prologue == "norm_mod":
        nb = shift.shape[0]
        bsel = (lambda i: batch_of(i)) if nb > 1 else (lambda i: 0)
        args += [gain.reshape(1, k), shift, scale]
        specs += [pl.BlockSpec((1, k), lambda i, j: (0, 0)),
                  pl.BlockSpec((1, 1, k), lambda i, j: (bsel(i), 0, 0)),
                  pl.BlockSpec((1, 1, k), lambda i, j: (bsel(i), 0, 0))]
    args.append(w)
    specs.append(pl.BlockSpec((k, tn), lambda i, j: (0, j), **(once if tn == n else {})))
    if epilogue == "bias":
        args.append(bias.reshape(1, n))
        specs.append(pl.BlockSpec((1, tn), lambda i, j: (0, j)))
    elif epilogue == "resid":
        nb = gate.shape[0]
        gsel = (lambda i: batch_of(i)) if nb > 1 else (lambda i: 0)
        args += [res, gate]
        specs += [pl.BlockSpec((tm, tn), lambda i, j: (i, j)),
                  pl.BlockSpec((1, 1, tn), lambda i, j: (gsel(i), 0, j))]
    scratch = []
    if prologue != "cast":
        scratch.append(pltpu.VMEM((tm, k), BF16))
    kern = functools.partial(_mm_kernel, prologue=prologue, epilogue=epilogue)
    return pl.pallas_call(
        kern,
        grid=(t // tm, n // tn),
        in_specs=specs,
        out_specs=pl.BlockSpec((tm, tn), lambda i, j: (i, j)),
        out_shape=jax.ShapeDtypeStruct((t, n), out_dtype),
        scratch_shapes=scratch,
        compiler_params=_params("parallel", "arbitrary"),
        name=name,
    )(*args)


def _qkv_kernel(*refs, normed_cols, rope):
    it = iter(refs)
    x_ref, gain_ref, shift_ref, scale_ref, w_ref, hg_ref, ones_ref = [next(it) for _ in range(7)]
    if rope:
        perm_ref, cos_ref, sin_ref = next(it), next(it), next(it)
    o_ref, xn_ref = next(it), next(it)
    j = pl.program_id(1)
    tm, tn = o_ref.shape
    gw = ones_ref.shape[0]
    n_sub = tn // gw
    full_tiles, rem = divmod(normed_cols, tn)

    @pl.when(j == 0)
    def _():
        _norm_mod_rows(x_ref, xn_ref, gain_ref[...], shift_ref[0], scale_ref[0])

    def norm_rope_store(acc, cols):
        for r0 in range(0, tm, ROPE_STRIP):
            rows = slice(r0, r0 + ROPE_STRIP)
            blk = acc[rows, :]
            ms = jnp.dot((blk * blk).astype(BF16), ones_ref[...], preferred_element_type=F32)
            y = blk * lax.rsqrt(ms + NORM_EPS) * hg_ref[:, cols]
            if rope:
                partner = jnp.dot(y.astype(BF16), perm_ref[...], preferred_element_type=F32)
                reps = gw // HEAD_DIM
                y = (y * jnp.concatenate([cos_ref[rows, :]] * reps, axis=-1)
                     + partner * jnp.concatenate([sin_ref[rows, :]] * reps, axis=-1))
            o_ref[rows, cols] = y.astype(o_ref.dtype)

    def tile(n_normed_subs):
        a = xn_ref[...]
        if n_normed_subs == 0:
            o_ref[...] = jnp.dot(a, w_ref[...], preferred_element_type=F32).astype(o_ref.dtype)
            return
        sub_cols = [slice(s * gw, (s + 1) * gw) for s in range(n_sub)]
        acc = jnp.dot(a, w_ref[:, sub_cols[0]], preferred_element_type=F32)
        for s in range(n_sub):
            nxt = jnp.dot(a, w_ref[:, sub_cols[s + 1]], preferred_element_type=F32) if s + 1 < n_sub else None
            if s < n_normed_subs:
                norm_rope_store(acc, sub_cols[s])
            else:
                o_ref[:, sub_cols[s]] = acc.astype(o_ref.dtype)
            acc = nxt

    if full_tiles:
        pl.when(j < full_tiles)(functools.partial(tile, n_sub))
    if rem:
        pl.when(j == full_tiles)(functools.partial(tile, rem // gw))
    pl.when(j >= full_tiles + (1 if rem else 0))(functools.partial(tile, 0))


def _qkv_proj(x, w, gain, shift, scale, col_gain, *, normed_cols, tm, tn, name, rows_per_batch=None,
              cos=None, sin=None):
    t, k = x.shape
    n = w.shape[1]
    assert t % tm == 0 and n % tn == 0 and normed_cols % MXU_WIDTH == 0
    rows_per_batch = rows_per_batch or t
    assert rows_per_batch % tm == 0
    tiles_per_batch = rows_per_batch // tm
    rope = cos is not None
    nb = shift.shape[0]
    bsel = (lambda i: i // tiles_per_batch) if nb > 1 else (lambda i: 0)
    ones, perm = _head_group_matrices()
    const = pl.BlockSpec(ones.shape, lambda i, j: (0, 0))
    vec = pl.BlockSpec((1, 1, k), lambda i, j: (bsel(i), 0, 0))
    args = [x, gain.reshape(1, k), shift, scale, w, col_gain, ones]
    specs = [pl.BlockSpec((tm, k), lambda i, j: (i, 0)),
             pl.BlockSpec((1, k), lambda i, j: (0, 0)), vec, vec,
             pl.BlockSpec((k, tn), lambda i, j: (0, j)),
             pl.BlockSpec((1, tn), lambda i, j: (0, j)), const]
    if rope:
        table = pl.BlockSpec((tm, HEAD_DIM), lambda i, j: (i % tiles_per_batch, 0))
        args += [perm, cos, sin]
        specs += [const, table, table]
    return pl.pallas_call(
        functools.partial(_qkv_kernel, normed_cols=normed_cols, rope=rope),
        grid=(t // tm, n // tn),
        in_specs=specs,
        out_specs=pl.BlockSpec((tm, tn), lambda i, j: (i, j)),
        out_shape=jax.ShapeDtypeStruct((t, n), BF16),
        scratch_shapes=[pltpu.VMEM((tm, k), BF16)],
        compiler_params=_params("parallel", "arbitrary"),
        name=name,
    )(*args)


FFN_HID_SPLIT = 2


def _ffn_kernel(*refs, n_cast):
    x_ref, gain_ref, shift_ref, scale_ref, gate_ref, wg_ref, wu_ref, wo_ref = refs[:8]
    cast_src = refs[8:8 + n_cast]
    o_ref = refs[8 + n_cast]
    cast_dst = refs[9 + n_cast:9 + 2 * n_cast]
    xn_ref = refs[9 + 2 * n_cast]
    j = pl.program_id(1)
    hc = wg_ref.shape[1] // FFN_HID_SPLIT
    cols = [slice(c * hc, (c + 1) * hc) for c in range(FFN_HID_SPLIT)]
    for src, dst in zip(cast_src, cast_dst):
        dst[...] = src[...].astype(dst.dtype)

    def chunk(first, last):
        if first:
            _norm_mod_rows(x_ref, xn_ref, gain_ref[...], shift_ref[0], scale_ref[0])
        xn = xn_ref[...]
        gu = [(jnp.dot(xn, wg_ref[:, cs], preferred_element_type=F32),
               jnp.dot(xn, wu_ref[:, cs], preferred_element_type=F32)) for cs in cols]
        for c, (g, u) in enumerate(gu):
            a = (jax.nn.silu(g) * u).astype(BF16)
            part = jnp.dot(a, wo_ref[cols[c], :], preferred_element_type=F32)
            if first and c == 0:
                o_ref[...] = part
            elif last and c == len(gu) - 1:
                o_ref[...] = x_ref[...] + gate_ref[0] * (o_ref[...] + part)
            else:
                o_ref[...] += part

    n_chunks = pl.num_programs(1)
    pl.when(j == 0)(functools.partial(chunk, True, False))
    pl.when((j > 0) & (j < n_chunks - 1))(functools.partial(chunk, False, False))
    pl.when(j == n_chunks - 1)(functools.partial(chunk, False, True))


def _ffn(x, gain, shift, scale, gate, w_in, w_out, *, tm, th, name, rows_per_batch=None, cast=()):
    t, d = x.shape
    hidden = w_out.shape[0]
    assert t % tm == 0 and hidden % th == 0 and hidden // th >= 2
    nh = hidden // th
    n_steps = (t // tm) * nh
    rows_per_batch = rows_per_batch or t
    tiles_per_batch = rows_per_batch // tm
    nb = shift.shape[0]
    bsel = (lambda i: i // tiles_per_batch) if nb > 1 else (lambda i: 0)
    vec = pl.BlockSpec((1, 1, d), lambda i, j: (bsel(i), 0, 0))
    slabs = list(cast)
    slab_specs = []
    if slabs:
        nw_in, nw_out = slabs
        in_rows, in_cols = nw_in.shape[0] // (t // tm), nw_in.shape[1] // nh
        out_rows = nw_out.shape[0] // n_steps
        assert in_rows * (t // tm) == nw_in.shape[0] and in_cols * nh == nw_in.shape[1] and in_cols % LANES == 0
        assert out_rows * n_steps == nw_out.shape[0] and in_rows % BF16_ROWS == 0 and out_rows % BF16_ROWS == 0
        slab_specs = [pl.BlockSpec((in_rows, in_cols), lambda i, j: (i, j)),
                      pl.BlockSpec((out_rows, nw_out.shape[1]), lambda i, j: (i * nh + j, 0))]
    outs = pl.pallas_call(
        functools.partial(_ffn_kernel, n_cast=len(slabs)),
        grid=(t // tm, nh),
        in_specs=[pl.BlockSpec((tm, d), lambda i, j: (i, 0)),
                  pl.BlockSpec((1, d), lambda i, j: (0, 0)),
                  vec, vec, vec,
                  pl.BlockSpec((d, th), lambda i, j: (0, j)),
                  pl.BlockSpec((d, th), lambda i, j: (0, nh + j)),
                  pl.BlockSpec((th, d), lambda i, j: (j, 0))] + slab_specs,
        out_specs=[pl.BlockSpec((tm, d), lambda i, j: (i, 0))] + slab_specs,
        out_shape=[jax.ShapeDtypeStruct((t, d), F32)] + [jax.ShapeDtypeStruct(s.shape, BF16) for s in slabs],
        scratch_shapes=[pltpu.VMEM((tm, d), BF16)],
        compiler_params=_params("parallel", "arbitrary"),
        name=name,
    )(x, gain.reshape(1, d), shift, scale, gate, w_in, w_in, w_out, *slabs)
    return outs[0], tuple(outs[1:])


def _fill_kv(k_s, v_s, kv_refs):
    off = 0
    for k_ref, v_ref in kv_refs:
        n = k_ref.shape[0]
        k_s[off:off + n, :] = k_ref[...]
        v_s[off:off + n, :] = v_ref[...]
        off += n


ATTN_SUBTILE = 256


def _gqa_kernel(*refs, n_kv_src, n_group):
    q_ref = refs[0]
    kv_refs = [(refs[1 + 2 * s], refs[2 + 2 * s]) for s in range(n_kv_src)]
    o_ref, k_s, v_s = refs[1 + 2 * n_kv_src:]

    @pl.when(pl.program_id(2) == 0)
    def _():
        _fill_kv(k_s, v_s, kv_refs)

    k = k_s[...]
    v = v_s[...]
    sub = min(ATTN_SUBTILE, q_ref.shape[0])
    chains = [(slice(r0, r0 + sub), slice(g * HEAD_DIM, (g + 1) * HEAD_DIM))
              for r0 in range(0, q_ref.shape[0], sub) for g in range(n_group)]

    def scores(c):
        return lax.dot_general(q_ref[chains[c]], k, NT_DIMS, preferred_element_type=F32)

    s = scores(0)
    for c in range(len(chains)):
        s_next = scores(c + 1) if c + 1 < len(chains) else None
        p = jnp.exp2(s - jnp.max(s, axis=-1, keepdims=True))
        l = jnp.sum(p, axis=-1, keepdims=True)
        o = jnp.dot(p.astype(BF16), v, preferred_element_type=F32)
        o_ref[chains[c]] = (o / l).astype(o_ref.dtype)
        s = s_next


def _gqa_attention(q_src, kv_srcs, *, batch, n_q_heads, n_kv_heads, tq, q_rows, name):
    group = n_q_heads // n_kv_heads
    gw = group * HEAD_DIM
    nq = q_rows // tq
    k_blk0 = n_q_heads
    v_blk0 = n_q_heads + n_kv_heads
    specs = [pl.BlockSpec((tq, gw), lambda b, h, i: (b * nq + i, h))]
    args = [q_src]
    total = 0
    for src in kv_srcs:
        rows = src.shape[0] // batch
        total += rows
        specs += [pl.BlockSpec((rows, HEAD_DIM), lambda b, h, i: (b, k_blk0 + h)),
                  pl.BlockSpec((rows, HEAD_DIM), lambda b, h, i: (b, v_blk0 + h))]
        args += [src, src]
    kern = functools.partial(_gqa_kernel, n_kv_src=len(kv_srcs), n_group=group)
    return pl.pallas_call(
        kern,
        grid=(batch, n_kv_heads, nq),
        in_specs=specs,
        out_specs=pl.BlockSpec((tq, gw), lambda b, h, i: (b * nq + i, h)),
        out_shape=jax.ShapeDtypeStruct((batch * q_rows, n_q_heads * HEAD_DIM), BF16),
        scratch_shapes=[pltpu.VMEM((total, HEAD_DIM), BF16), pltpu.VMEM((total, HEAD_DIM), BF16)],
        compiler_params=_params("parallel", "parallel", "arbitrary"),
        name=name,
    )(*args)


def _diff_kernel(*refs, n_kv_src, lam_init):
    q_ref = refs[0]
    kv_refs = [(refs[1 + 2 * s], refs[2 + 2 * s]) for s in range(n_kv_src)]
    lq1, lk1, lq2, lk2, gain_ref, o_ref, k_s, v_s = refs[1 + 2 * n_kv_src:]

    @pl.when(pl.program_id(2) == 0)
    def _():
        _fill_kv(k_s, v_s, kv_refs)

    lam = (jnp.exp(jnp.sum(lq1[...] * lk1[...], axis=-1, keepdims=True))
           - jnp.exp(jnp.sum(lq2[...] * lk2[...], axis=-1, keepdims=True)) + lam_init)
    dh = HEAD_DIM
    sub = min(ATTN_SUBTILE, q_ref.shape[0])
    chains = [slice(r0, r0 + sub) for r0 in range(0, q_ref.shape[0], sub)]

    def scores(c):
        return [lax.dot_general(q_ref[chains[c], m * dh:(m + 1) * dh], k_s[:, m * dh:(m + 1) * dh], NT_DIMS,
                                preferred_element_type=F32) for m in range(2)]

    s = scores(0)
    for c in range(len(chains)):
        s_next = scores(c + 1) if c + 1 < len(chains) else None
        p = [jnp.exp2(sm - jnp.max(sm, axis=-1, keepdims=True)) for sm in s]
        l0, l1 = [jnp.sum(pm, axis=-1, keepdims=True) for pm in p]
        w = p[0] - p[1] * (lam * l0 / l1)
        o = jnp.dot(w.astype(BF16), v_s[...], preferred_element_type=F32) * (1.0 / l0)
        o_ref[chains[c], :] = ((_rms(o) * gain_ref[...]) * (1.0 - lam_init)).astype(o_ref.dtype)
        s = s_next


def _diff_attention(q_src, kv_srcs, lams, out_gain, *, batch, n_heads, tq, q_rows, lam_init, name):
    hw = 2 * HEAD_DIM
    nq = q_rows // tq
    specs = [pl.BlockSpec((tq, hw), lambda b, h, i: (b * nq + i, h))]
    args = [q_src]
    total = 0
    for src in kv_srcs:
        rows = src.shape[0] // batch
        total += rows
        specs += [pl.BlockSpec((rows, hw), lambda b, h, i: (b, n_heads + h)),
                  pl.BlockSpec((rows, hw), lambda b, h, i: (b, 2 * n_heads + h))]
        args += [src, src]
    small = pl.BlockSpec((1, HEAD_DIM), lambda b, h, i: (0, 0))
    specs += [small] * 4 + [pl.BlockSpec((1, hw), lambda b, h, i: (0, 0))]
    args += [v.reshape(1, HEAD_DIM) for v in lams] + [out_gain.reshape(1, hw)]
    kern = functools.partial(_diff_kernel, n_kv_src=len(kv_srcs), lam_init=lam_init)
    return pl.pallas_call(
        kern,
        grid=(batch, n_heads, nq),
        in_specs=specs,
        out_specs=pl.BlockSpec((tq, hw), lambda b, h, i: (b * nq + i, h)),
        out_shape=jax.ShapeDtypeStruct((batch * q_rows, n_heads * hw), BF16),
        scratch_shapes=[pltpu.VMEM((total, hw), BF16), pltpu.VMEM((total, hw), BF16)],
        compiler_params=_params("parallel", "parallel", "arbitrary"),
        name=name,
    )(*args)


def _gla_blocks(blocks):
    r, dk = blocks[0][0].shape
    c = GLA_CHUNK
    nc = r // c
    n = range(len(blocks))
    qs_, ks_, vs_, zs_, wgs, bgs, st_refs, revs = zip(*blocks)
    row = lax.broadcasted_iota(jnp.int32, (r, r), 0)
    col = lax.broadcasted_iota(jnp.int32, (r, r), 1)
    same = row // c == col // c
    tri_f = [jnp.where(same, jnp.where((col >= row) if revs[i] else (col <= row), 1.0, 0.0), 0.0) for i in n]
    tri = [t.astype(BF16) for t in tri_f]
    mid = [c // 2 if revs[i] else c // 2 - 1 for i in n]
    last = [0 if revs[i] else c - 1 for i in n]

    g = [jax.nn.log_sigmoid(jnp.dot(zs_[i], wgs[i], preferred_element_type=F32) + bgs[i]) / GLA_TAU for i in n]
    g_hi = [g[i].astype(BF16) for i in n]
    g_lo = [(g[i] - g_hi[i].astype(F32)).astype(BF16) for i in n]
    cum = [jnp.dot(tri[i], g_hi[i], preferred_element_type=F32) + jnp.dot(tri[i], g_lo[i], preferred_element_type=F32)
           for i in n]

    def chunk_row(x, idx):
        return jnp.concatenate(
            [jnp.broadcast_to(x[ci * c + idx:ci * c + idx + 1, :], (c, dk)) for ci in range(nc)], axis=0)

    cum_mid = [chunk_row(cum[i], mid[i]) for i in n]
    cum_last = [chunk_row(cum[i], last[i]) for i in n]
    qs = [(qs_[i] * jnp.exp(cum[i] - cum_mid[i])).astype(BF16) for i in n]
    ks = [(ks_[i] * jnp.exp(cum_mid[i] - cum[i])).astype(BF16) for i in n]
    a = [lax.dot_general(qs[i], ks[i], NT_DIMS, preferred_element_type=F32) for i in n]
    q_inter = [qs_[i] * jnp.exp(cum[i]) for i in n]
    k_carry = [ks_[i] * jnp.exp(cum_last[i] - cum[i]) for i in n]

    assert nc % 2 == 0
    same_pair = row // (2 * c) == col // (2 * c)
    pair_f = [jnp.where(same_pair, jnp.where((col // c > row // c) if revs[i] else (col // c < row // c), 1.0, 0.0),
                        0.0) for i in n]
    a_pair = [lax.dot_general(q_inter[i].astype(BF16), k_carry[i].astype(BF16), NT_DIMS,
                              preferred_element_type=F32) for i in n]
    a = [(jnp.where(tri_f[i] > 0.5, a[i], 0.0) + jnp.where(pair_f[i] > 0.5, a_pair[i], 0.0)).astype(BF16) for i in n]
    o_intra = [jnp.dot(a[i], vs_[i], preferred_element_type=F32) for i in n]

    def tot(i, ci):
        return cum[i][ci * c + last[i]:ci * c + last[i] + 1, :]

    def is_first(i, ci):
        return (ci % 2 == 1) if revs[i] else (ci % 2 == 0)

    def pair_scaled(i, x, scale_first):
        parts = []
        for ci in range(nc):
            rows = slice(ci * c, (ci + 1) * c)
            if is_first(i, ci) == scale_first:
                parts.append(x[rows] * jnp.exp(tot(i, ci ^ 1)))
            else:
                parts.append(x[rows])
        return jnp.concatenate(parts, axis=0).astype(BF16)

    q_sup = [pair_scaled(i, q_inter[i], False) for i in n]
    k_sup = [pair_scaled(i, k_carry[i], True) for i in n]
    st = [st_refs[i][...] for i in n]
    n_pairs = nc // 2
    o_inter = [[None] * n_pairs for _ in n]
    for step in range(n_pairs):
        for i in n:
            pg = n_pairs - 1 - step if revs[i] else step
            rows = slice(pg * 2 * c, (pg + 1) * 2 * c)
            o_inter[i][pg] = lax.dot_general(q_sup[i][rows], st[i].astype(BF16), NT_DIMS,
                                             preferred_element_type=F32)
            decay = jnp.exp(tot(i, 2 * pg) + tot(i, 2 * pg + 1))
            st[i] = st[i] * decay + lax.dot_general(vs_[i][rows], k_sup[i][rows], TN_DIMS,
                                                    preferred_element_type=F32)
    for i in n:
        st_refs[i][...] = st[i]
    return [o_intra[i] + jnp.concatenate(o_inter[i], axis=0) for i in n]


def _gla_kernel(qc_ref, kc_ref, vc_ref, rc_ref, zc_ref, ql_ref, kl_ref, vl_ref, rl_ref, zl_ref,
                wgf_ref, bgf_ref, wgb_ref, bgb_ref, gain_ref, oc_ref, ol_ref, sf_ref, sb_ref, of_ref, ob_ref):
    blk = GLA_BLOCK
    n_ctx = qc_ref.shape[0] // blk
    n_lat = ql_ref.shape[0] // blk
    q_scale = qc_ref.shape[1] ** -0.5
    ctx_refs = (qc_ref, kc_ref, vc_ref, zc_ref)
    lat_refs = (ql_ref, kl_ref, vl_ref, zl_ref)

    def rows_of(bi):
        return pl.ds(bi * blk if isinstance(bi, int) else pl.multiple_of(bi * blk, blk), blk)

    def block(refs, bi, reverse):
        q_ref, k_ref, v_ref, z_ref = refs
        rows = rows_of(bi)
        wg_ref, bg_ref, st_ref = (wgb_ref, bgb_ref, sb_ref) if reverse else (wgf_ref, bgf_ref, sf_ref)
        return (q_ref[rows, :].astype(F32) * q_scale, k_ref[rows, :].astype(F32), v_ref[rows, :], z_ref[rows, :],
                wg_ref[0], bg_ref[0], st_ref, reverse)

    def emit(r_ref, out_ref, bi, o):
        rows = rows_of(bi)
        y = (_rms(o) * gain_ref[...]) * jax.nn.silu(r_ref[rows, :].astype(F32))
        out_ref[rows, :] = y.astype(out_ref.dtype)

    def scan(refs, r_ref, out_ref, base, n):
        def both(t):
            return _gla_blocks([block(refs, t, False), block(refs, n - 1 - t, True)])

        def park(t, carry):
            o_f, o_b = both(t)
            of_ref[rows_of(base + t), :] = o_f
            ob_ref[rows_of(base + n - 1 - t), :] = o_b
            return carry

        def meet(t, carry):
            o_f, o_b = both(t)
            emit(r_ref, out_ref, t, o_f + ob_ref[rows_of(base + t), :])
            emit(r_ref, out_ref, n - 1 - t, of_ref[rows_of(base + n - 1 - t), :] + o_b)
            return carry

        lax.fori_loop(0, n // 2, park, 0)
        if n % 2:
            o_f, o_b = both(n // 2)
            emit(r_ref, out_ref, n // 2, o_f + o_b)
        lax.fori_loop((n + 1) // 2, n, meet, 0)

    sf_ref[...] = jnp.zeros_like(sf_ref)
    sb_ref[...] = jnp.zeros_like(sb_ref)
    scan(ctx_refs, rc_ref, oc_ref, 0, n_ctx)
    scan(lat_refs, rl_ref, ol_ref, n_ctx, n_lat)


def _gla(p_ctx, p_lat, wgf, bgf, wgb, bgb, out_gain, *, batch, dk, dv):
    h = GLA_HEADS
    rows_c = p_ctx.shape[0] // batch
    rows_l = p_lat.shape[0] // batch
    assert rows_c % GLA_BLOCK == 0 and rows_l % GLA_BLOCK == 0
    zblk = (2 * h * dk + 2 * h * dv) // LANES
    k0 = h
    v0 = (2 * h * dk) // dv
    r0 = v0 + h

    def stream(rows):
        return [pl.BlockSpec((rows, dk), lambda b, hh: (b, hh)),
                pl.BlockSpec((rows, dk), lambda b, hh: (b, k0 + hh)),
                pl.BlockSpec((rows, dv), lambda b, hh: (b, v0 + hh)),
                pl.BlockSpec((rows, dv), lambda b, hh: (b, r0 + hh)),
                pl.BlockSpec((rows, LANES), lambda b, hh: (b, zblk))]

    wspec = pl.BlockSpec((1, LANES, dk), lambda b, hh: (hh, 0, 0))
    bspec = pl.BlockSpec((1, 1, dk), lambda b, hh: (hh, 0, 0))
    return pl.pallas_call(
        _gla_kernel,
        grid=(batch, h),
        in_specs=stream(rows_c) + stream(rows_l) + [wspec, bspec, wspec, bspec,
                                                    pl.BlockSpec((1, dv), lambda b, hh: (0, 0))],
        out_specs=[pl.BlockSpec((rows_c, dv), lambda b, hh: (b, hh)),
                   pl.BlockSpec((rows_l, dv), lambda b, hh: (b, hh))],
        out_shape=[jax.ShapeDtypeStruct((batch * rows_c, h * dv), BF16),
                   jax.ShapeDtypeStruct((batch * rows_l, h * dv), BF16)],
        scratch_shapes=[pltpu.VMEM((dv, dk), F32), pltpu.VMEM((dv, dk), F32),
                        pltpu.VMEM((rows_c + rows_l, dv), F32), pltpu.VMEM((rows_c + rows_l, dv), F32)],
        compiler_params=_params("parallel", "parallel"),
        name="gla",
    )(*([p_ctx] * 5 + [p_lat] * 5 + [wgf, bgf, wgb, bgb, out_gain.reshape(1, dv)]))


def _dft_tables(n):
    idx = jnp.arange(n, dtype=jnp.int32)
    ang = ((idx[:, None] * idx[None, :]) % n).astype(F32) * (2.0 * math.pi / n)
    return jnp.cos(ang).astype(BF16), jnp.sin(ang).astype(BF16)


def _fnet_chan_kernel(x_ref, gain_ref, shift_ref, scale_ref, csc_ref, p_ref, q_ref, xn_ref):
    gd = csc_ref.shape[0]
    _norm_mod_rows(x_ref, xn_ref, gain_ref[...], shift_ref[0], scale_ref[0])
    for g in range(x_ref.shape[1] // gd):
        cols = slice(g * gd, (g + 1) * gd)
        pq = jnp.dot(xn_ref[:, cols], csc_ref[...], preferred_element_type=F32)
        p_ref[:, cols] = pq[:, :gd].astype(p_ref.dtype)
        q_ref[:, cols] = pq[:, gd:].astype(q_ref.dtype)


def _fnet_seq_kernel(cs_ref, ss_ref, p_ref, q_ref, o_ref, *, inv_norm):
    acc = (jnp.dot(cs_ref[...], p_ref[...], preferred_element_type=F32)
           - jnp.dot(ss_ref[...], q_ref[...], preferred_element_type=F32))
    o_ref[...] = (acc * inv_norm).astype(o_ref.dtype)


def _fnet(x, gain, shift, scale, *, batch, tm):
    t, d = x.shape
    s = t // batch
    gd = d // FNET_GROUPS
    tiles_per_batch = s // tm
    cc, sc = _dft_tables(gd)
    cs, ss = _dft_tables(s)
    vec = pl.BlockSpec((1, 1, d), lambda i: (i // tiles_per_batch, 0, 0))
    blk = pl.BlockSpec((tm, d), lambda i: (i, 0))
    p, q = pl.pallas_call(
        _fnet_chan_kernel,
        grid=(t // tm,),
        in_specs=[blk, pl.BlockSpec((1, d), lambda i: (0, 0)), vec, vec,
                  pl.BlockSpec((gd, 2 * gd), lambda i: (0, 0))],
        out_specs=[blk, blk],
        out_shape=[jax.ShapeDtypeStruct((t, d), BF16)] * 2,
        scratch_shapes=[pltpu.VMEM((tm, d), BF16)],
        compiler_params=_params("parallel"),
        name="fnet_chan",
    )(x, gain.reshape(1, d), shift, scale, jnp.concatenate([cc, sc], axis=1))
    rows = pl.BlockSpec((tm, s), lambda i, b, j: (i, 0))
    cols = pl.BlockSpec((s, gd), lambda i, b, j: (b, j))
    return pl.pallas_call(
        functools.partial(_fnet_seq_kernel, inv_norm=float((s * gd) ** -0.5)),
        grid=(tiles_per_batch, batch, d // gd),
        in_specs=[rows, rows, cols, cols],
        out_specs=pl.BlockSpec((tm, gd), lambda i, b, j: (b * tiles_per_batch + i, j)),
        out_shape=jax.ShapeDtypeStruct((t, d), BF16),
        compiler_params=_params("parallel", "parallel", "parallel"),
        name="fnet_seq",
    )(cs, ss, p, q)


def _rope_tables(n_tokens):
    t = jnp.arange(n_tokens)
    row = (t // GRID_W).astype(F32)
    col = (t % GRID_W).astype(F32)
    half = HEAD_DIM // 2
    inv_freq = ROPE_THETA ** (-jnp.arange(0, half, 2, dtype=F32) / half)
    ang_r = row[:, None] * inv_freq[None, :]
    ang_c = col[:, None] * inv_freq[None, :]
    ang = jnp.concatenate([ang_r, ang_r, ang_c, ang_c], axis=-1)
    sign = jnp.concatenate([-jnp.ones((half // 2,), F32), jnp.ones((half // 2,), F32)] * 2)
    return jnp.cos(ang), jnp.sin(ang) * sign


def _column_gains(q_gain, k_gain, q_cols, k_cols, v_cols):
    qg = jnp.tile(q_gain.astype(F32) * (HEAD_DIM ** -0.5 * math.log2(math.e)), q_cols // HEAD_DIM)
    kg = jnp.tile(k_gain.astype(F32), k_cols // HEAD_DIM)
    return jnp.concatenate([qg, kg, jnp.ones((v_cols,), F32)])[None, :]


def _gate_weights(wg, bg, lane0, dk):
    r = wg.shape[0]
    w = wg.reshape(r, GLA_HEADS, dk).transpose(1, 0, 2)
    w = jnp.pad(w, ((0, 0), (lane0, LANES - lane0 - r), (0, 0))).astype(BF16)
    return w, bg.reshape(GLA_HEADS, 1, dk).astype(F32)


def kernel(x, c, ctx, c_ctx, l0_mod_w, l0_mod_b, l0_norm1, l0_gla_w_in, l0_gla_wg_f, l0_gla_bg_f, l0_gla_wg_b, l0_gla_bg_b, l0_gla_out_norm, l0_gla_w_out, l0_norm2, l0_ffn_w_in, l0_ffn_w_out, l1_mod_w, l1_mod_b, l1_norm1, l1_gqa_w_in, l1_gqa_q_norm, l1_gqa_k_norm, l1_gqa_w_out, l1_norm2, l1_ffn_w_in, l1_ffn_w_out, l2_mod_w, l2_mod_b, l2_norm1, l2_diff_w_in, l2_diff_q_norm, l2_diff_k_norm, l2_diff_lq1, l2_diff_lk1, l2_diff_lq2, l2_diff_lk2, l2_diff_out_norm, l2_diff_w_out, l2_norm2, l2_ffn_w_in, l2_ffn_w_out, l3_mod_w, l3_mod_b, l3_norm1, l3_fnet_w_out, l3_norm2, l3_ffn_w_in, l3_ffn_w_out):
    b, s, d = x.shape
    n_ctx = ctx.shape[1]
    xl = x.reshape(b * s, d)
    xc = ctx.reshape(b * n_ctx, d)
    tm = math.gcd(s, 1024)
    tm_c = math.gcd(b * n_ctx, 1024)
    tm_r = math.gcd(s, 512)
    tm_rc = math.gcd(b * n_ctx, 512)
    th = math.gcd(l0_ffn_w_out.shape[0], 512)
    tq_gqa = math.gcd(s, 4 * ATTN_SUBTILE)
    tq_diff = math.gcd(s, 4 * ATTN_SUBTILE)

    n_cond = -(-(b + 1) // BF16_ROWS) * BF16_ROWS
    cond = jnp.concatenate([c, c_ctx[None, :], jnp.zeros((n_cond - b - 1, d), F32)], axis=0)

    def modulation(mod_w, mod_b):
        m = _mm(cond, mod_w, tm=n_cond, tn=math.gcd(mod_w.shape[1], 1024), out_dtype=F32, prologue="silu", epilogue="bias", bias=mod_b,
                name="modulation")
        lat = [m[:b, k * d:(k + 1) * d].reshape(b, 1, d) for k in range(6)]
        cx = [m[b:b + 1, k * d:(k + 1) * d].reshape(1, 1, d) for k in range(6)]
        return lat, cx

    def tiling(rows, resident=False):
        if rows == s:
            return dict(tm=tm_r if resident else tm, rows_per_batch=s)
        return dict(tm=tm_rc if resident else tm_c, rows_per_batch=None)

    def proj(xs, w, n1, sh, sc, rows, tn, name, **kw):
        return _mm(xs, w, tn=tn, out_dtype=BF16, prologue="norm_mod", gain=n1, shift=sh, scale=sc, name=name,
                   **tiling(rows), **kw)

    def qkv_proj(xs, w, n1, sh, sc, col_gain, rows, name, **kw):
        return _qkv_proj(xs, w, n1, sh, sc, col_gain, tn=math.gcd(w.shape[1], 1024), name=name, **tiling(rows), **kw)

    def out_resid(y, w, xs, gate, rows):
        return _mm(y, w.astype(BF16), tn=w.shape[1], out_dtype=F32, epilogue="resid", res=xs, gate=gate,
                   name="out_resid", **tiling(rows, resident=True))

    def ffn(xs, n2, sh, sc, gate, w_bf16, rows, next_w=()):
        return _ffn(xs, n2, sh, sc, gate, *w_bf16, th=th, name="ffn", cast=next_w, **tiling(rows))

    ffn_w = (l0_ffn_w_in.astype(BF16), l0_ffn_w_out.astype(BF16))

    (sh1, sc1, g1, sh2, sc2, g2), (csh1, csc1, cg1, csh2, csc2, cg2) = modulation(l0_mod_w, l0_mod_b)
    dk = l0_gla_wg_f.shape[1] // GLA_HEADS
    dv = d // GLA_HEADS
    n_in = l0_gla_w_in.shape[1]
    tn0 = 1280
    n_pad = -(-(n_in - 2 * GLA_RANK + LANES) // tn0) * tn0
    w0 = jnp.pad(l0_gla_w_in, ((0, 0), (0, n_pad - n_in))).astype(BF16)
    pl0 = proj(xl, w0, l0_norm1, sh1, sc1, s, tn0, "gla_proj")
    pc0 = proj(xc, w0, l0_norm1, csh1, csc1, n_ctx, tn0, "gla_proj_ctx")
    wgf, bgf = _gate_weights(l0_gla_wg_f, l0_gla_bg_f, 0, dk)
    wgb, bgb = _gate_weights(l0_gla_wg_b, l0_gla_bg_b, GLA_RANK, dk)
    yc, yl = _gla(pc0, pl0, wgf, bgf, wgb, bgb, l0_gla_out_norm, batch=b, dk=dk, dv=dv)
    xl = out_resid(yl, l0_gla_w_out, xl, g1, s)
    xc = out_resid(yc, l0_gla_w_out, xc, cg1, n_ctx)
    xl, next_ffn_w = ffn(xl, l0_norm2, sh2, sc2, g2, ffn_w, s, next_w=(l1_ffn_w_in, l1_ffn_w_out))
    xc, _ = ffn(xc, l0_norm2, csh2, csc2, cg2, ffn_w, n_ctx)
    ffn_w = next_ffn_w

    (sh1, sc1, g1, sh2, sc2, g2), (csh1, csc1, cg1, csh2, csc2, cg2) = modulation(l1_mod_w, l1_mod_b)
    n_heads = d // HEAD_DIM
    cos, sin = _rope_tables(s)
    kv_cols = GQA_KV_HEADS * HEAD_DIM
    cg_gqa = _column_gains(l1_gqa_q_norm, l1_gqa_k_norm, d, kv_cols, kv_cols)
    w1 = l1_gqa_w_in.astype(BF16)
    pl1 = qkv_proj(xl, w1, l1_norm1, sh1, sc1, cg_gqa, s, "gqa_proj", normed_cols=d + kv_cols, cos=cos, sin=sin)
    pc1 = qkv_proj(xc, w1, l1_norm1, csh1, csc1, cg_gqa, n_ctx, "gqa_proj_ctx", normed_cols=d + kv_cols)
    yl = _gqa_attention(pl1, [pl1, pc1], batch=b, n_q_heads=n_heads, n_kv_heads=GQA_KV_HEADS, tq=tq_gqa, q_rows=s,
                        name="gqa_attn")
    yc = _gqa_attention(pc1, [pc1], batch=b, n_q_heads=n_heads, n_kv_heads=GQA_KV_HEADS, tq=n_ctx, q_rows=n_ctx,
                        name="gqa_attn_ctx")
    xl = out_resid(yl, l1_gqa_w_out, xl, g1, s)
    xc = out_resid(yc, l1_gqa_w_out, xc, cg1, n_ctx)
    xl, next_ffn_w = ffn(xl, l1_norm2, sh2, sc2, g2, ffn_w, s, next_w=(l2_ffn_w_in, l2_ffn_w_out))
    xc, _ = ffn(xc, l1_norm2, csh2, csc2, cg2, ffn_w, n_ctx)
    ffn_w = next_ffn_w

    (sh1, sc1, g1, sh2, sc2, g2), (csh1, csc1, _, _, _, _) = modulation(l2_mod_w, l2_mod_b)
    lam_init = 0.8 - 0.6 * math.exp(-0.3 * 2)
    cg_diff = _column_gains(l2_diff_q_norm, l2_diff_k_norm, d, d, d)
    w2 = l2_diff_w_in.astype(BF16)
    pl2 = qkv_proj(xl, w2, l2_norm1, sh1, sc1, cg_diff, s, "diff_proj", normed_cols=2 * d, cos=cos, sin=sin)
    pc2 = qkv_proj(xc, w2, l2_norm1, csh1, csc1, cg_diff, n_ctx, "diff_proj_ctx", normed_cols=2 * d)
    yl = _diff_attention(pl2, [pl2, pc2], (l2_diff_lq1, l2_diff_lk1, l2_diff_lq2, l2_diff_lk2),
                         l2_diff_out_norm, batch=b, n_heads=n_heads // 2, tq=tq_diff, q_rows=s, lam_init=lam_init,
                         name="diff_attn")
    xl = out_resid(yl, l2_diff_w_out, xl, g1, s)
    xl, ffn_w = ffn(xl, l2_norm2, sh2, sc2, g2, ffn_w, s, next_w=(l3_ffn_w_in, l3_ffn_w_out))

    (sh1, sc1, g1, sh2, sc2, g2), _ = modulation(l3_mod_w, l3_mod_b)
    yl = _fnet(xl, l3_norm1, sh1, sc1, batch=b, tm=tm)
    xl = out_resid(yl, l3_fnet_w_out, xl, g1, s)
    xl, _ = ffn(xl, l3_norm2, sh2, sc2, g2, ffn_w, s)
    return xl.reshape(b, s, d)
```

```python
import functools
import math

import jax
import jax.numpy as jnp
from jax import lax
from jax.experimental import pallas as pl
from jax.experimental.pallas import tpu as pltpu

F32 = jnp.float32
BF16 = jnp.bfloat16

NORM_EPS = 1e-6
ROPE_THETA = 10000.0
GRID_W = 64
HEAD_DIM = 128
GQA_KV_HEADS = 4
GLA_HEADS = 4
GLA_RANK = 16
GLA_TAU = 16.0
GLA_CHUNK = 64
GLA_BLOCK = 4 * GLA_CHUNK
FNET_GROUPS = 4

LANES = 128
BF16_ROWS = 16
STRIP_UNROLL = 8
VMEM_LIMIT = 56 * 1024 * 1024

NT_DIMS = (((1,), (1,)), ((), ()))
TN_DIMS = (((0,), (0,)), ((), ()))


def _params(*sem):
    return pltpu.CompilerParams(dimension_semantics=sem, vmem_limit_bytes=VMEM_LIMIT)


def _rms(x, eps=NORM_EPS):
    return x * lax.rsqrt(jnp.mean(x * x, axis=-1, keepdims=True) + eps)


def _norm_mod_rows(x_ref, xn_ref, gain, shift, scale):
    mult = gain * (1.0 + scale)

    def strip(r, carry):
        rows = pl.ds(pl.multiple_of(r * BF16_ROWS, BF16_ROWS), BF16_ROWS)
        xn_ref[rows, :] = (_rms(x_ref[rows, :]) * mult + shift).astype(BF16)
        return carry

    lax.fori_loop(0, x_ref.shape[0] // BF16_ROWS, strip, 0, unroll=STRIP_UNROLL)


ROPE_STRIP = 256
MXU_WIDTH = 256


def _head_group_matrices():
    src = lax.broadcasted_iota(jnp.int32, (MXU_WIDTH, MXU_WIDTH), 0)
    dst = lax.broadcasted_iota(jnp.int32, (MXU_WIDTH, MXU_WIDTH), 1)
    quarter = HEAD_DIM // 4
    partner = jnp.where((dst // quarter) % 2 == 0, dst + quarter, dst - quarter)
    mean = jnp.where(src // HEAD_DIM == dst // HEAD_DIM, 1.0 / HEAD_DIM, 0.0).astype(BF16)
    return mean, (src == partner).astype(BF16)


def _mm_kernel(*refs, prologue, epilogue):
    it = iter(refs)
    x_ref = next(it)
    if prologue == "norm_mod":
        gain_ref, shift_ref, scale_ref = next(it), next(it), next(it)
    w_ref = next(it)
    if epilogue == "bias":
        b_ref = next(it)
    elif epilogue == "resid":
        res_ref, gate_ref = next(it), next(it)
    o_ref = next(it)
    if prologue != "cast":
        xn_ref = next(it)
    j = pl.program_id(1)

    if prologue == "cast":
        a = x_ref[...].astype(BF16)
    else:
        @pl.when(j == 0)
        def _():
            if prologue == "norm_mod":
                _norm_mod_rows(x_ref, xn_ref, gain_ref[...], shift_ref[0], scale_ref[0])
            else:
                xn_ref[...] = jax.nn.silu(x_ref[...]).astype(BF16)

        a = xn_ref[...]
    acc = jnp.dot(a, w_ref[...].astype(BF16), preferred_element_type=F32)
    if epilogue == "store":
        o_ref[...] = acc.astype(o_ref.dtype)
    elif epilogue == "bias":
        o_ref[...] = (acc + b_ref[...]).astype(o_ref.dtype)
    elif epilogue == "resid":
        o_ref[...] = (res_ref[...] + gate_ref[0] * acc).astype(o_ref.dtype)


def _mm(x, w, *, tm, tn, out_dtype, name, prologue="cast", epilogue="store", rows_per_batch=None,
        gain=None, shift=None, scale=None, bias=None, res=None, gate=None):
    t, k = x.shape
    n = w.shape[1]
    assert t % tm == 0 and n % tn == 0, (t, tm, n, tn)
    rows_per_batch = rows_per_batch or t
    assert rows_per_batch % tm == 0
    tiles_per_batch = rows_per_batch // tm
    once = dict(pipeline_mode=pl.Buffered(1))

    def batch_of(i):
        return i // tiles_per_batch

    args = [x]
    specs = [pl.BlockSpec((tm, k), lambda i, j: (i, 0))]
    if prologue == "norm_mod":
        nb = shift.shape[0]
        bsel = (lambda i: batch_of(i)) if nb > 1 else (lambda i: 0)
        args += [gain.reshape(1, k), shift, scale]
        specs += [pl.BlockSpec((1, k), lambda i, j: (0, 0)),
                  pl.BlockSpec((1, 1, k), lambda i, j: (bsel(i), 0, 0)),
                  pl.BlockSpec((1, 1, k), lambda i, j: (bsel(i), 0, 0))]
    args.append(w)
    specs.append(pl.BlockSpec((k, tn), lambda i, j: (0, j), **(once if tn == n else {})))
    if epilogue == "bias":
        args.append(bias.reshape(1, n))
        specs.append(pl.BlockSpec((1, tn), lambda i, j: (0, j)))
    elif epilogue == "resid":
        nb = gate.shape[0]
        gsel = (lambda i: batch_of(i)) if nb > 1 else (lambda i: 0)
        args += [res, gate]
        specs += [pl.BlockSpec((tm, tn), lambda i, j: (i, j)),
                  pl.BlockSpec((1, 1, tn), lambda i, j: (gsel(i), 0, j))]
    scratch = []
    if prologue != "cast":
        scratch.append(pltpu.VMEM((tm, k), BF16))
    kern = functools.partial(_mm_kernel, prologue=prologue, epilogue=epilogue)
    return pl.pallas_call(
        kern,
        grid=(t // tm, n // tn),
        in_specs=specs,
        out_specs=pl.BlockSpec((tm, tn), lambda i, j: (i, j)),
        out_shape=jax.ShapeDtypeStruct((t, n), out_dtype),
        scratch_shapes=scratch,
        compiler_params=_params("parallel", "arbitrary"),
        name=name,
    )(*args)


def _qkv_kernel(*refs, normed_cols, rope):
    it = iter(refs)
    x_ref, gain_ref, shift_ref, scale_ref, w_ref, hg_ref, ones_ref = [next(it) for _ in range(7)]
    if rope:
        perm_ref, cos_ref, sin_ref = next(it), next(it), next(it)
    o_ref, xn_ref = next(it), next(it)
    j = pl.program_id(1)
    tm, tn = o_ref.shape
    gw = ones_ref.shape[0]
    n_sub = tn // gw
    full_tiles, rem = divmod(normed_cols, tn)

    @pl.when(j == 0)
    def _():
        _norm_mod_rows(x_ref, xn_ref, gain_ref[...], shift_ref[0], scale_ref[0])

    def norm_rope_store(acc, cols):
        for r0 in range(0, tm, ROPE_STRIP):
            rows = slice(r0, r0 + ROPE_STRIP)
            blk = acc[rows, :]
            ms = jnp.dot((blk * blk).astype(BF16), ones_ref[...], preferred_element_type=F32)
            y = blk * lax.rsqrt(ms + NORM_EPS) * hg_ref[:, cols]
            if rope:
                partner = jnp.dot(y.astype(BF16), perm_ref[...], preferred_element_type=F32)
                reps = gw // HEAD_DIM
                y = (y * jnp.concatenate([cos_ref[rows, :]] * reps, axis=-1)
                     + partner * jnp.concatenate([sin_ref[rows, :]] * reps, axis=-1))
            o_ref[rows, cols] = y.astype(o_ref.dtype)

    def tile(n_normed_subs):
        a = xn_ref[...]
        if n_normed_subs == 0:
            o_ref[...] = jnp.dot(a, w_ref[...], preferred_element_type=F32).astype(o_ref.dtype)
            return
        sub_cols = [slice(s * gw, (s + 1) * gw) for s in range(n_sub)]
        acc = jnp.dot(a, w_ref[:, sub_cols[0]], preferred_element_type=F32)
        for s in range(n_sub):
            nxt = jnp.dot(a, w_ref[:, sub_cols[s + 1]], preferred_element_type=F32) if s + 1 < n_sub else None
            if s < n_normed_subs:
                norm_rope_store(acc, sub_cols[s])
            else:
                o_ref[:, sub_cols[s]] = acc.astype(o_ref.dtype)
            acc = nxt

    if full_tiles:
        pl.when(j < full_tiles)(functools.partial(tile, n_sub))
    if rem:
        pl.when(j == full_tiles)(functools.partial(tile, rem // gw))
    pl.when(j >= full_tiles + (1 if rem else 0))(functools.partial(tile, 0))


def _qkv_proj(x, w, gain, shift, scale, col_gain, *, normed_cols, tm, tn, name, rows_per_batch=None,
              cos=None, sin=None):
    t, k = x.shape
    n = w.shape[1]
    assert t % tm == 0 and n % tn == 0 and normed_cols % MXU_WIDTH == 0
    rows_per_batch = rows_per_batch or t
    assert rows_per_batch % tm == 0
    tiles_per_batch = rows_per_batch // tm
    rope = cos is not None
    nb = shift.shape[0]
    bsel = (lambda i: i // tiles_per_batch) if nb > 1 else (lambda i: 0)
    ones, perm = _head_group_matrices()
    const = pl.BlockSpec(ones.shape, lambda i, j: (0, 0))
    vec = pl.BlockSpec((1, 1, k), lambda i, j: (bsel(i), 0, 0))
    args = [x, gain.reshape(1, k), shift, scale, w, col_gain, ones]
    specs = [pl.BlockSpec((tm, k), lambda i, j: (i, 0)),
             pl.BlockSpec((1, k), lambda i, j: (0, 0)), vec, vec,
             pl.BlockSpec((k, tn), lambda i, j: (0, j)),
             pl.BlockSpec((1, tn), lambda i, j: (0, j)), const]
    if rope:
        table = pl.BlockSpec((tm, HEAD_DIM), lambda i, j: (i % tiles_per_batch, 0))
        args += [perm, cos, sin]
        specs += [const, table, table]
    return pl.pallas_call(
        functools.partial(_qkv_kernel, normed_cols=normed_cols, rope=rope),
        grid=(t // tm, n // tn),
        in_specs=specs,
        out_specs=pl.BlockSpec((tm, tn), lambda i, j: (i, j)),
        out_shape=jax.ShapeDtypeStruct((t, n), BF16),
        scratch_shapes=[pltpu.VMEM((tm, k), BF16)],
        compiler_params=_params("parallel", "arbitrary"),
        name=name,
    )(*args)


FFN_HID_SPLIT = 2


def _ffn_kernel(*refs, n_cast):
    x_ref, gain_ref, shift_ref, scale_ref, gate_ref, wg_ref, wu_ref, wo_ref = refs[:8]
    cast_src = refs[8:8 + n_cast]
    o_ref = refs[8 + n_cast]
    cast_dst = refs[9 + n_cast:9 + 2 * n_cast]
    xn_ref = refs[9 + 2 * n_cast]
    j = pl.program_id(1)
    hc = wg_ref.shape[1] // FFN_HID_SPLIT
    cols = [slice(c * hc, (c + 1) * hc) for c in range(FFN_HID_SPLIT)]
    for src, dst in zip(cast_src, cast_dst):
        dst[...] = src[...].astype(dst.dtype)

    def chunk(first, last):
        if first:
            _norm_mod_rows(x_ref, xn_ref, gain_ref[...], shift_ref[0], scale_ref[0])
        xn = xn_ref[...]
        gu = [(jnp.dot(xn, wg_ref[:, cs], preferred_element_type=F32),
               jnp.dot(xn, wu_ref[:, cs], preferred_element_type=F32)) for cs in cols]
        for c, (g, u) in enumerate(gu):
            a = (jax.nn.silu(g) * u).astype(BF16)
            part = jnp.dot(a, wo_ref[cols[c], :], preferred_element_type=F32)
            if first and c == 0:
                o_ref[...] = part
            elif last and c == len(gu) - 1:
                o_ref[...] = x_ref[...] + gate_ref[0] * (o_ref[...] + part)
            else:
                o_ref[...] += part

    n_chunks = pl.num_programs(1)
    pl.when(j == 0)(functools.partial(chunk, True, False))
    pl.when((j > 0) & (j < n_chunks - 1))(functools.partial(chunk, False, False))
    pl.when(j == n_chunks - 1)(functools.partial(chunk, False, True))


def _ffn(x, gain, shift, scale, gate, w_in, w_out, *, tm, th, name, rows_per_batch=None, cast=()):
    t, d = x.shape
    hidden = w_out.shape[0]
    assert t % tm == 0 and hidden % th == 0 and hidden // th >= 2
    nh = hidden // th
    n_steps = (t // tm) * nh
    rows_per_batch = rows_per_batch or t
    tiles_per_batch = rows_per_batch // tm
    nb = shift.shape[0]
    bsel = (lambda i: i // tiles_per_batch) if nb > 1 else (lambda i: 0)
    vec = pl.BlockSpec((1, 1, d), lambda i, j: (bsel(i), 0, 0))
    slabs = list(cast)
    slab_specs = []
    if slabs:
        nw_in, nw_out = slabs
        in_rows, in_cols = nw_in.shape[0] // (t // tm), nw_in.shape[1] // nh
        out_rows = nw_out.shape[0] // n_steps
        assert in_rows * (t // tm) == nw_in.shape[0] and in_cols * nh == nw_in.shape[1] and in_cols % LANES == 0
        assert out_rows * n_steps == nw_out.shape[0] and in_rows % BF16_ROWS == 0 and out_rows % BF16_ROWS == 0
        slab_specs = [pl.BlockSpec((in_rows, in_cols), lambda i, j: (i, j)),
                      pl.BlockSpec((out_rows, nw_out.shape[1]), lambda i, j: (i * nh + j, 0))]
    outs = pl.pallas_call(
        functools.partial(_ffn_kernel, n_cast=len(slabs)),
        grid=(t // tm, nh),
        in_specs=[pl.BlockSpec((tm, d), lambda i, j: (i, 0)),
                  pl.BlockSpec((1, d), lambda i, j: (0, 0)),
                  vec, vec, vec,
                  pl.BlockSpec((d, th), lambda i, j: (0, j)),
                  pl.BlockSpec((d, th), lambda i, j: (0, nh + j)),
                  pl.BlockSpec((th, d), lambda i, j: (j, 0))] + slab_specs,
        out_specs=[pl.BlockSpec((tm, d), lambda i, j: (i, 0))] + slab_specs,
        out_shape=[jax.ShapeDtypeStruct((t, d), F32)] + [jax.ShapeDtypeStruct(s.shape, BF16) for s in slabs],
        scratch_shapes=[pltpu.VMEM((tm, d), BF16)],
        compiler_params=_params("parallel", "arbitrary"),
        name=name,
    )(x, gain.reshape(1, d), shift, scale, gate, w_in, w_in, w_out, *slabs)
    return outs[0], tuple(outs[1:])


def _fill_kv(k_s, v_s, kv_refs):
    off = 0
    for k_ref, v_ref in kv_refs:
        n = k_ref.shape[0]
        k_s[off:off + n, :] = k_ref[...]
        v_s[off:off + n, :] = v_ref[...]
        off += n


ATTN_SUBTILE = 256


def _gqa_kernel(*refs, n_kv_src, n_group):
    q_ref = refs[0]
    kv_refs = [(refs[1 + 2 * s], refs[2 + 2 * s]) for s in range(n_kv_src)]
    o_ref, k_s, v_s = refs[1 + 2 * n_kv_src:]

    @pl.when(pl.program_id(2) == 0)
    def _():
        _fill_kv(k_s, v_s, kv_refs)

    k = k_s[...]
    v = v_s[...]
    sub = min(ATTN_SUBTILE, q_ref.shape[0])
    chains = [(slice(r0, r0 + sub), slice(g * HEAD_DIM, (g + 1) * HEAD_DIM))
              for r0 in range(0, q_ref.shape[0], sub) for g in range(n_group)]

    def scores(c):
        return lax.dot_general(q_ref[chains[c]], k, NT_DIMS, preferred_element_type=F32)

    s = scores(0)
    for c in range(len(chains)):
        s_next = scores(c + 1) if c + 1 < len(chains) else None
        p = jnp.exp2(s - jnp.max(s, axis=-1, keepdims=True))
        l = jnp.sum(p, axis=-1, keepdims=True)
        o = jnp.dot(p.astype(BF16), v, preferred_element_type=F32)
        o_ref[chains[c]] = (o / l).astype(o_ref.dtype)
        s = s_next


def _gqa_attention(q_src, kv_srcs, *, batch, n_q_heads, n_kv_heads, tq, q_rows, name):
    group = n_q_heads // n_kv_heads
    gw = group * HEAD_DIM
    nq = q_rows // tq
    k_blk0 = n_q_heads
    v_blk0 = n_q_heads + n_kv_heads
    specs = [pl.BlockSpec((tq, gw), lambda b, h, i: (b * nq + i, h))]
    args = [q_src]
    total = 0
    for src in kv_srcs:
        rows = src.shape[0] // batch
        total += rows
        specs += [pl.BlockSpec((rows, HEAD_DIM), lambda b, h, i: (b, k_blk0 + h)),
                  pl.BlockSpec((rows, HEAD_DIM), lambda b, h, i: (b, v_blk0 + h))]
        args += [src, src]
    kern = functools.partial(_gqa_kernel, n_kv_src=len(kv_srcs), n_group=group)
    return pl.pallas_call(
        kern,
        grid=(batch, n_kv_heads, nq),
        in_specs=specs,
        out_specs=pl.BlockSpec((tq, gw), lambda b, h, i: (b * nq + i, h)),
        out_shape=jax.ShapeDtypeStruct((batch * q_rows, n_q_heads * HEAD_DIM), BF16),
        scratch_shapes=[pltpu.VMEM((total, HEAD_DIM), BF16), pltpu.VMEM((total, HEAD_DIM), BF16)],
        compiler_params=_params("parallel", "parallel", "arbitrary"),
        name=name,
    )(*args)


def _diff_kernel(*refs, n_kv_src, lam_init):
    q_ref = refs[0]
    kv_refs = [(refs[1 + 2 * s], refs[2 + 2 * s]) for s in range(n_kv_src)]
    lq1, lk1, lq2, lk2, gain_ref, o_ref, k_s, v_s = refs[1 + 2 * n_kv_src:]

    @pl.when(pl.program_id(2) == 0)
    def _():
        _fill_kv(k_s, v_s, kv_refs)

    lam = (jnp.exp(jnp.sum(lq1[...] * lk1[...], axis=-1, keepdims=True))
           - jnp.exp(jnp.sum(lq2[...] * lk2[...], axis=-1, keepdims=True)) + lam_init)
    dh = HEAD_DIM
    sub = min(ATTN_SUBTILE, q_ref.shape[0])
    chains = [slice(r0, r0 + sub) for r0 in range(0, q_ref.shape[0], sub)]

    def scores(c):
        return [lax.dot_general(q_ref[chains[c], m * dh:(m + 1) * dh], k_s[:, m * dh:(m + 1) * dh], NT_DIMS,
                                preferred_element_type=F32) for m in range(2)]

    s = scores(0)
    for c in range(len(chains)):
        s_next = scores(c + 1) if c + 1 < len(chains) else None
        p = [jnp.exp2(sm - jnp.max(sm, axis=-1, keepdims=True)) for sm in s]
        l0, l1 = [jnp.sum(pm, axis=-1, keepdims=True) for pm in p]
        w = p[0] - p[1] * (lam * l0 / l1)
        o = jnp.dot(w.astype(BF16), v_s[...], preferred_element_type=F32) * (1.0 / l0)
        o_ref[chains[c], :] = ((_rms(o) * gain_ref[...]) * (1.0 - lam_init)).astype(o_ref.dtype)
        s = s_next


def _diff_attention(q_src, kv_srcs, lams, out_gain, *, batch, n_heads, tq, q_rows, lam_init, name):
    hw = 2 * HEAD_DIM
    nq = q_rows // tq
    specs = [pl.BlockSpec((tq, hw), lambda b, h, i: (b * nq + i, h))]
    args = [q_src]
    total = 0
    for src in kv_srcs:
        rows = src.shape[0] // batch
        total += rows
        k0 = src.shape[1] // hw - 2 * n_heads
        specs += [pl.BlockSpec((rows, hw), lambda b, h, i, k0=k0: (b, k0 + h)),
                  pl.BlockSpec((rows, hw), lambda b, h, i, k0=k0: (b, k0 + n_heads + h))]
        args += [src, src]
    small = pl.BlockSpec((1, HEAD_DIM), lambda b, h, i: (0, 0))
    specs += [small] * 4 + [pl.BlockSpec((1, hw), lambda b, h, i: (0, 0))]
    args += [v.reshape(1, HEAD_DIM) for v in lams] + [out_gain.reshape(1, hw)]
    kern = functools.partial(_diff_kernel, n_kv_src=len(kv_srcs), lam_init=lam_init)
    return pl.pallas_call(
        kern,
        grid=(batch, n_heads, nq),
        in_specs=specs,
        out_specs=pl.BlockSpec((tq, hw), lambda b, h, i: (b * nq + i, h)),
        out_shape=jax.ShapeDtypeStruct((batch * q_rows, n_heads * hw), BF16),
        scratch_shapes=[pltpu.VMEM((total, hw), BF16), pltpu.VMEM((total, hw), BF16)],
        compiler_params=_params("parallel", "parallel", "arbitrary"),
        name=name,
    )(*args)


def _gla_blocks(blocks):
    r, dk = blocks[0][0].shape
    c = GLA_CHUNK
    nc = r // c
    n = range(len(blocks))
    qs_, ks_, vs_, zs_, wgs, bgs, st_refs, revs = zip(*blocks)
    row = lax.broadcasted_iota(jnp.int32, (r, r), 0)
    col = lax.broadcasted_iota(jnp.int32, (r, r), 1)
    same = row // c == col // c
    tri_f = [jnp.where(same, jnp.where((col >= row) if revs[i] else (col <= row), 1.0, 0.0), 0.0) for i in n]
    tri = [t.astype(BF16) for t in tri_f]
    mid = [c // 2 if revs[i] else c // 2 - 1 for i in n]
    last = [0 if revs[i] else c - 1 for i in n]

    g = [jax.nn.log_sigmoid(jnp.dot(zs_[i], wgs[i], preferred_element_type=F32) + bgs[i]) / GLA_TAU for i in n]
    g_hi = [g[i].astype(BF16) for i in n]
    g_lo = [(g[i] - g_hi[i].astype(F32)).astype(BF16) for i in n]
    cum = [jnp.dot(tri[i], g_hi[i], preferred_element_type=F32) + jnp.dot(tri[i], g_lo[i], preferred_element_type=F32)
           for i in n]

    def chunk_row(x, idx):
        return jnp.concatenate(
            [jnp.broadcast_to(x[ci * c + idx:ci * c + idx + 1, :], (c, dk)) for ci in range(nc)], axis=0)

    cum_mid = [chunk_row(cum[i], mid[i]) for i in n]
    cum_last = [chunk_row(cum[i], last[i]) for i in n]
    qs = [(qs_[i] * jnp.exp(cum[i] - cum_mid[i])).astype(BF16) for i in n]
    ks = [(ks_[i] * jnp.exp(cum_mid[i] - cum[i])).astype(BF16) for i in n]
    a = [lax.dot_general(qs[i], ks[i], NT_DIMS, preferred_element_type=F32) for i in n]
    q_inter = [qs_[i] * jnp.exp(cum[i]) for i in n]
    k_carry = [ks_[i] * jnp.exp(cum_last[i] - cum[i]) for i in n]

    assert nc % 2 == 0
    same_pair = row // (2 * c) == col // (2 * c)
    pair_f = [jnp.where(same_pair, jnp.where((col // c > row // c) if revs[i] else (col // c < row // c), 1.0, 0.0),
                        0.0) for i in n]
    a_pair = [lax.dot_general(q_inter[i].astype(BF16), k_carry[i].astype(BF16), NT_DIMS,
                              preferred_element_type=F32) for i in n]
    a = [(jnp.where(tri_f[i] > 0.5, a[i], 0.0) + jnp.where(pair_f[i] > 0.5, a_pair[i], 0.0)).astype(BF16) for i in n]
    o_intra = [jnp.dot(a[i], vs_[i], preferred_element_type=F32) for i in n]

    def tot(i, ci):
        return cum[i][ci * c + last[i]:ci * c + last[i] + 1, :]

    def is_first(i, ci):
        return (ci % 2 == 1) if revs[i] else (ci % 2 == 0)

    def pair_scaled(i, x, scale_first):
        parts = []
        for ci in range(nc):
            rows = slice(ci * c, (ci + 1) * c)
            if is_first(i, ci) == scale_first:
                parts.append(x[rows] * jnp.exp(tot(i, ci ^ 1)))
            else:
                parts.append(x[rows])
        return jnp.concatenate(parts, axis=0).astype(BF16)

    q_sup = [pair_scaled(i, q_inter[i], False) for i in n]
    k_sup = [pair_scaled(i, k_carry[i], True) for i in n]
    st = [st_refs[i][...] for i in n]
    n_pairs = nc // 2
    o_inter = [[None] * n_pairs for _ in n]
    for step in range(n_pairs):
        for i in n:
            pg = n_pairs - 1 - step if revs[i] else step
            rows = slice(pg * 2 * c, (pg + 1) * 2 * c)
            o_inter[i][pg] = lax.dot_general(q_sup[i][rows], st[i].astype(BF16), NT_DIMS,
                                             preferred_element_type=F32)
            decay = jnp.exp(tot(i, 2 * pg) + tot(i, 2 * pg + 1))
            st[i] = st[i] * decay + lax.dot_general(vs_[i][rows], k_sup[i][rows], TN_DIMS,
                                                    preferred_element_type=F32)
    for i in n:
        st_refs[i][...] = st[i]
    return [o_intra[i] + jnp.concatenate(o_inter[i], axis=0) for i in n]


def _gla_kernel(qc_ref, kc_ref, vc_ref, rc_ref, zc_ref, ql_ref, kl_ref, vl_ref, rl_ref, zl_ref,
                wgf_ref, bgf_ref, wgb_ref, bgb_ref, gain_ref, oc_ref, ol_ref, sf_ref, sb_ref, of_ref, ob_ref):
    blk = GLA_BLOCK
    n_ctx = qc_ref.shape[0] // blk
    n_lat = ql_ref.shape[0] // blk
    q_scale = qc_ref.shape[1] ** -0.5
    ctx_refs = (qc_ref, kc_ref, vc_ref, zc_ref)
    lat_refs = (ql_ref, kl_ref, vl_ref, zl_ref)

    def rows_of(bi):
        return pl.ds(bi * blk if isinstance(bi, int) else pl.multiple_of(bi * blk, blk), blk)

    def block(refs, bi, reverse):
        q_ref, k_ref, v_ref, z_ref = refs
        rows = rows_of(bi)
        wg_ref, bg_ref, st_ref = (wgb_ref, bgb_ref, sb_ref) if reverse else (wgf_ref, bgf_ref, sf_ref)
        return (q_ref[rows, :].astype(F32) * q_scale, k_ref[rows, :].astype(F32), v_ref[rows, :], z_ref[rows, :],
                wg_ref[0], bg_ref[0], st_ref, reverse)

    def emit(r_ref, out_ref, bi, o):
        rows = rows_of(bi)
        y = (_rms(o) * gain_ref[...]) * jax.nn.silu(r_ref[rows, :].astype(F32))
        out_ref[rows, :] = y.astype(out_ref.dtype)

    def scan(refs, r_ref, out_ref, base, n):
        def both(t):
            return _gla_blocks([block(refs, t, False), block(refs, n - 1 - t, True)])

        def park(t, carry):
            o_f, o_b = both(t)
            of_ref[rows_of(base + t), :] = o_f
            ob_ref[rows_of(base + n - 1 - t), :] = o_b
            return carry

        def meet(t, carry):
            o_f, o_b = both(t)
            emit(r_ref, out_ref, t, o_f + ob_ref[rows_of(base + t), :])
            emit(r_ref, out_ref, n - 1 - t, of_ref[rows_of(base + n - 1 - t), :] + o_b)
            return carry

        lax.fori_loop(0, n // 2, park, 0)
        if n % 2:
            o_f, o_b = both(n // 2)
            emit(r_ref, out_ref, n // 2, o_f + o_b)
        lax.fori_loop((n + 1) // 2, n, meet, 0)

    sf_ref[...] = jnp.zeros_like(sf_ref)
    sb_ref[...] = jnp.zeros_like(sb_ref)
    scan(ctx_refs, rc_ref, oc_ref, 0, n_ctx)
    scan(lat_refs, rl_ref, ol_ref, n_ctx, n_lat)


def _gla(p_ctx, p_lat, wgf, bgf, wgb, bgb, out_gain, *, batch, dk, dv):
    h = GLA_HEADS
    rows_c = p_ctx.shape[0] // batch
    rows_l = p_lat.shape[0] // batch
    assert rows_c % GLA_BLOCK == 0 and rows_l % GLA_BLOCK == 0
    zblk = (2 * h * dk + 2 * h * dv) // LANES
    k0 = h
    v0 = (2 * h * dk) // dv
    r0 = v0 + h

    def stream(rows):
        return [pl.BlockSpec((rows, dk), lambda b, hh: (b, hh)),
                pl.BlockSpec((rows, dk), lambda b, hh: (b, k0 + hh)),
                pl.BlockSpec((rows, dv), lambda b, hh: (b, v0 + hh)),
                pl.BlockSpec((rows, dv), lambda b, hh: (b, r0 + hh)),
                pl.BlockSpec((rows, LANES), lambda b, hh: (b, zblk))]

    wspec = pl.BlockSpec((1, LANES, dk), lambda b, hh: (hh, 0, 0))
    bspec = pl.BlockSpec((1, 1, dk), lambda b, hh: (hh, 0, 0))
    return pl.pallas_call(
        _gla_kernel,
        grid=(batch, h),
        in_specs=stream(rows_c) + stream(rows_l) + [wspec, bspec, wspec, bspec,
                                                    pl.BlockSpec((1, dv), lambda b, hh: (0, 0))],
        out_specs=[pl.BlockSpec((rows_c, dv), lambda b, hh: (b, hh)),
                   pl.BlockSpec((rows_l, dv), lambda b, hh: (b, hh))],
        out_shape=[jax.ShapeDtypeStruct((batch * rows_c, h * dv), BF16),
                   jax.ShapeDtypeStruct((batch * rows_l, h * dv), BF16)],
        scratch_shapes=[pltpu.VMEM((dv, dk), F32), pltpu.VMEM((dv, dk), F32),
                        pltpu.VMEM((rows_c + rows_l, dv), F32), pltpu.VMEM((rows_c + rows_l, dv), F32)],
        compiler_params=_params("parallel", "parallel"),
        name="gla",
    )(*([p_ctx] * 5 + [p_lat] * 5 + [wgf, bgf, wgb, bgb, out_gain.reshape(1, dv)]))


def _dft_tables(n):
    idx = jnp.arange(n, dtype=jnp.int32)
    ang = ((idx[:, None] * idx[None, :]) % n).astype(F32) * (2.0 * math.pi / n)
    return jnp.cos(ang).astype(BF16), jnp.sin(ang).astype(BF16)


def _fnet_chan_kernel(x_ref, gain_ref, shift_ref, scale_ref, csc_ref, p_ref, q_ref, xn_ref):
    gd = csc_ref.shape[0]
    _norm_mod_rows(x_ref, xn_ref, gain_ref[...], shift_ref[0], scale_ref[0])
    for g in range(x_ref.shape[1] // gd):
        cols = slice(g * gd, (g + 1) * gd)
        pq = jnp.dot(xn_ref[:, cols], csc_ref[...], preferred_element_type=F32)
        p_ref[:, cols] = pq[:, :gd].astype(p_ref.dtype)
        q_ref[:, cols] = pq[:, gd:].astype(q_ref.dtype)


def _fnet_seq_kernel(cs_ref, ss_ref, p_ref, q_ref, o_ref, *, inv_norm):
    acc = (jnp.dot(cs_ref[...], p_ref[...], preferred_element_type=F32)
           - jnp.dot(ss_ref[...], q_ref[...], preferred_element_type=F32))
    o_ref[...] = (acc * inv_norm).astype(o_ref.dtype)


def _fnet(x, gain, shift, scale, *, batch, tm):
    t, d = x.shape
    s = t // batch
    gd = d // FNET_GROUPS
    tiles_per_batch = s // tm
    cc, sc = _dft_tables(gd)
    cs, ss = _dft_tables(s)
    vec = pl.BlockSpec((1, 1, d), lambda i: (i // tiles_per_batch, 0, 0))
    blk = pl.BlockSpec((tm, d), lambda i: (i, 0))
    p, q = pl.pallas_call(
        _fnet_chan_kernel,
        grid=(t // tm,),
        in_specs=[blk, pl.BlockSpec((1, d), lambda i: (0, 0)), vec, vec,
                  pl.BlockSpec((gd, 2 * gd), lambda i: (0, 0))],
        out_specs=[blk, blk],
        out_shape=[jax.ShapeDtypeStruct((t, d), BF16)] * 2,
        scratch_shapes=[pltpu.VMEM((tm, d), BF16)],
        compiler_params=_params("parallel"),
        name="fnet_chan",
    )(x, gain.reshape(1, d), shift, scale, jnp.concatenate([cc, sc], axis=1))
    rows = pl.BlockSpec((tm, s), lambda i, b, j: (i, 0))
    cols = pl.BlockSpec((s, gd), lambda i, b, j: (b, j))
    return pl.pallas_call(
        functools.partial(_fnet_seq_kernel, inv_norm=float((s * gd) ** -0.5)),
        grid=(tiles_per_batch, batch, d // gd),
        in_specs=[rows, rows, cols, cols],
        out_specs=pl.BlockSpec((tm, gd), lambda i, b, j: (b * tiles_per_batch + i, j)),
        out_shape=jax.ShapeDtypeStruct((t, d), BF16),
        compiler_params=_params("parallel", "parallel", "parallel"),
        name="fnet_seq",
    )(cs, ss, p, q)


def _rope_tables(n_tokens):
    t = jnp.arange(n_tokens)
    row = (t // GRID_W).astype(F32)
    col = (t % GRID_W).astype(F32)
    half = HEAD_DIM // 2
    inv_freq = ROPE_THETA ** (-jnp.arange(0, half, 2, dtype=F32) / half)
    ang_r = row[:, None] * inv_freq[None, :]
    ang_c = col[:, None] * inv_freq[None, :]
    ang = jnp.concatenate([ang_r, ang_r, ang_c, ang_c], axis=-1)
    sign = jnp.concatenate([-jnp.ones((half // 2,), F32), jnp.ones((half // 2,), F32)] * 2)
    return jnp.cos(ang), jnp.sin(ang) * sign


def _column_gains(q_gain, k_gain, q_cols, k_cols, v_cols):
    qg = jnp.tile(q_gain.astype(F32) * (HEAD_DIM ** -0.5 * math.log2(math.e)), q_cols // HEAD_DIM)
    kg = jnp.tile(k_gain.astype(F32), k_cols // HEAD_DIM)
    return jnp.concatenate([qg, kg, jnp.ones((v_cols,), F32)])[None, :]


def _gate_weights(wg, bg, lane0, dk):
    r = wg.shape[0]
    w = wg.reshape(r, GLA_HEADS, dk).transpose(1, 0, 2)
    w = jnp.pad(w, ((0, 0), (lane0, LANES - lane0 - r), (0, 0))).astype(BF16)
    return w, bg.reshape(GLA_HEADS, 1, dk).astype(F32)


def kernel(x, c, ctx, c_ctx, l0_mod_w, l0_mod_b, l0_norm1, l0_gla_w_in, l0_gla_wg_f, l0_gla_bg_f, l0_gla_wg_b, l0_gla_bg_b, l0_gla_out_norm, l0_gla_w_out, l0_norm2, l0_ffn_w_in, l0_ffn_w_out, l1_mod_w, l1_mod_b, l1_norm1, l1_gqa_w_in, l1_gqa_q_norm, l1_gqa_k_norm, l1_gqa_w_out, l1_norm2, l1_ffn_w_in, l1_ffn_w_out, l2_mod_w, l2_mod_b, l2_norm1, l2_diff_w_in, l2_diff_q_norm, l2_diff_k_norm, l2_diff_lq1, l2_diff_lk1, l2_diff_lq2, l2_diff_lk2, l2_diff_out_norm, l2_diff_w_out, l2_norm2, l2_ffn_w_in, l2_ffn_w_out, l3_mod_w, l3_mod_b, l3_norm1, l3_fnet_w_out, l3_norm2, l3_ffn_w_in, l3_ffn_w_out):
    b, s, d = x.shape
    n_ctx = ctx.shape[1]
    xl = x.reshape(b * s, d)
    xc = ctx.reshape(b * n_ctx, d)
    tm = math.gcd(s, 1024)
    tm_c = math.gcd(b * n_ctx, 1024)
    tm_r = math.gcd(s, 512)
    tm_rc = math.gcd(b * n_ctx, 512)
    th = math.gcd(l0_ffn_w_out.shape[0], 512)
    tq_gqa = math.gcd(s, 4 * ATTN_SUBTILE)
    tq_diff = math.gcd(s, 4 * ATTN_SUBTILE)

    n_cond = -(-(b + 1) // BF16_ROWS) * BF16_ROWS
    cond = jnp.concatenate([c, c_ctx[None, :], jnp.zeros((n_cond - b - 1, d), F32)], axis=0)

    def modulation(mod_w, mod_b):
        m = _mm(cond, mod_w, tm=n_cond, tn=math.gcd(mod_w.shape[1], 1024), out_dtype=F32, prologue="silu", epilogue="bias", bias=mod_b,
                name="modulation")
        lat = [m[:b, k * d:(k + 1) * d].reshape(b, 1, d) for k in range(6)]
        cx = [m[b:b + 1, k * d:(k + 1) * d].reshape(1, 1, d) for k in range(6)]
        return lat, cx

    def tiling(rows, resident=False):
        if rows == s:
            return dict(tm=tm_r if resident else tm, rows_per_batch=s)
        return dict(tm=tm_rc if resident else tm_c, rows_per_batch=None)

    def proj(xs, w, n1, sh, sc, rows, tn, name, **kw):
        return _mm(xs, w, tn=tn, out_dtype=BF16, prologue="norm_mod", gain=n1, shift=sh, scale=sc, name=name,
                   **tiling(rows), **kw)

    def qkv_proj(xs, w, n1, sh, sc, col_gain, rows, name, **kw):
        return _qkv_proj(xs, w, n1, sh, sc, col_gain, tn=math.gcd(w.shape[1], 1024), name=name, **tiling(rows), **kw)

    def out_resid(y, w, xs, gate, rows):
        return _mm(y, w.astype(BF16), tn=w.shape[1], out_dtype=F32, epilogue="resid", res=xs, gate=gate,
                   name="out_resid", **tiling(rows, resident=True))

    def ffn(xs, n2, sh, sc, gate, w_bf16, rows, next_w=()):
        return _ffn(xs, n2, sh, sc, gate, *w_bf16, th=th, name="ffn", cast=next_w, **tiling(rows))

    ffn_w = (l0_ffn_w_in.astype(BF16), l0_ffn_w_out.astype(BF16))

    (sh1, sc1, g1, sh2, sc2, g2), (csh1, csc1, cg1, csh2, csc2, cg2) = modulation(l0_mod_w, l0_mod_b)
    dk = l0_gla_wg_f.shape[1] // GLA_HEADS
    dv = d // GLA_HEADS
    n_in = l0_gla_w_in.shape[1]
    tn0 = 1280
    n_pad = -(-(n_in - 2 * GLA_RANK + LANES) // tn0) * tn0
    w0 = jnp.pad(l0_gla_w_in, ((0, 0), (0, n_pad - n_in))).astype(BF16)
    pl0 = proj(xl, w0, l0_norm1, sh1, sc1, s, tn0, "gla_proj")
    pc0 = proj(xc, w0, l0_norm1, csh1, csc1, n_ctx, tn0, "gla_proj_ctx")
    wgf, bgf = _gate_weights(l0_gla_wg_f, l0_gla_bg_f, 0, dk)
    wgb, bgb = _gate_weights(l0_gla_wg_b, l0_gla_bg_b, GLA_RANK, dk)
    yc, yl = _gla(pc0, pl0, wgf, bgf, wgb, bgb, l0_gla_out_norm, batch=b, dk=dk, dv=dv)
    xl = out_resid(yl, l0_gla_w_out, xl, g1, s)
    xc = out_resid(yc, l0_gla_w_out, xc, cg1, n_ctx)
    xl, next_ffn_w = ffn(xl, l0_norm2, sh2, sc2, g2, ffn_w, s, next_w=(l1_ffn_w_in, l1_ffn_w_out))
    xc, _ = ffn(xc, l0_norm2, csh2, csc2, cg2, ffn_w, n_ctx)
    ffn_w = next_ffn_w

    (sh1, sc1, g1, sh2, sc2, g2), (csh1, csc1, cg1, csh2, csc2, cg2) = modulation(l1_mod_w, l1_mod_b)
    n_heads = d // HEAD_DIM
    cos, sin = _rope_tables(s)
    kv_cols = GQA_KV_HEADS * HEAD_DIM
    cg_gqa = _column_gains(l1_gqa_q_norm, l1_gqa_k_norm, d, kv_cols, kv_cols)
    w1 = l1_gqa_w_in.astype(BF16)
    pl1 = qkv_proj(xl, w1, l1_norm1, sh1, sc1, cg_gqa, s, "gqa_proj", normed_cols=d + kv_cols, cos=cos, sin=sin)
    pc1 = qkv_proj(xc, w1, l1_norm1, csh1, csc1, cg_gqa, n_ctx, "gqa_proj_ctx", normed_cols=d + kv_cols)
    yl = _gqa_attention(pl1, [pl1, pc1], batch=b, n_q_heads=n_heads, n_kv_heads=GQA_KV_HEADS, tq=tq_gqa, q_rows=s,
                        name="gqa_attn")
    yc = _gqa_attention(pc1, [pc1], batch=b, n_q_heads=n_heads, n_kv_heads=GQA_KV_HEADS, tq=n_ctx, q_rows=n_ctx,
                        name="gqa_attn_ctx")
    xl = out_resid(yl, l1_gqa_w_out, xl, g1, s)
    xc = out_resid(yc, l1_gqa_w_out, xc, cg1, n_ctx)
    xl, next_ffn_w = ffn(xl, l1_norm2, sh2, sc2, g2, ffn_w, s, next_w=(l2_ffn_w_in, l2_ffn_w_out))
    xc, _ = ffn(xc, l1_norm2, csh2, csc2, cg2, ffn_w, n_ctx)
    ffn_w = next_ffn_w

    (sh1, sc1, g1, sh2, sc2, g2), (csh1, csc1, _, _, _, _) = modulation(l2_mod_w, l2_mod_b)
    lam_init = 0.8 - 0.6 * math.exp(-0.3 * 2)
    cg_diff = _column_gains(l2_diff_q_norm, l2_diff_k_norm, d, d, d)
    w2 = l2_diff_w_in.astype(BF16)
    pl2 = qkv_proj(xl, w2, l2_norm1, sh1, sc1, cg_diff, s, "diff_proj", normed_cols=2 * d, cos=cos, sin=sin)
    pc2 = qkv_proj(xc, w2[:, d:], l2_norm1, csh1, csc1, cg_diff[:, d:], n_ctx, "diff_proj_ctx", normed_cols=d)
    yl = _diff_attention(pl2, [pl2, pc2], (l2_diff_lq1, l2_diff_lk1, l2_diff_lq2, l2_diff_lk2),
                         l2_diff_out_norm, batch=b, n_heads=n_heads // 2, tq=tq_diff, q_rows=s, lam_init=lam_init,
                         name="diff_attn")
    xl = out_resid(yl, l2_diff_w_out, xl, g1, s)
    xl, ffn_w = ffn(xl, l2_norm2, sh2, sc2, g2, ffn_w, s, next_w=(l3_ffn_w_in, l3_ffn_w_out))

    (sh1, sc1, g1, sh2, sc2, g2), _ = modulation(l3_mod_w, l3_mod_b)
    yl = _fnet(xl, l3_norm1, sh1, sc1, batch=b, tm=tm)
    xl = out_resid(yl, l3_fnet_w_out, xl, g1, s)
    xl, _ = ffn(xl, l3_norm2, sh2, sc2, g2, ffn_w, s)
    return xl.reshape(b, s, d)
```
